```python
import jax, jax.numpy as jnp
from jax import lax
import numpy as np

D_MODEL = 2048
BATCH = 2
SEQ = 4096
DEPTH = 4
DEC_BATCH = 8
DEC_SEQ = 4
PAST_LEN = 16384
PAGE_SIZE = 128

N_HEADS = 16
HEAD_DIM = D_MODEL // N_HEADS
ROT_DIM = HEAD_DIM // 4
ROPE_THETA = 500000.0
FFN_DIM = ((8 * D_MODEL // 3 + 127) // 128) * 128
NORM_EPS = 1e-6
Q_BLOCK = 128
N_MIXERS = 3
N_NSA = (DEPTH + 2) // 3
N_MOBA = (DEPTH + 1) // 3
N_FOX = DEPTH // 3

NSA_KV_HEADS = 4
NSA_GROUP = N_HEADS // NSA_KV_HEADS
CMP_LEN = 32
CMP_STRIDE = 16
CMP_HIDDEN = HEAD_DIM
SLC_LEN = 64
SLC_TOP = 16
SLC_LOCAL = 2
WINDOW = 512
N_BRANCH = 3
NSA_Q = N_HEADS * HEAD_DIM
NSA_KV = NSA_KV_HEADS * HEAD_DIM
NSA_IN = NSA_Q + 2 * N_BRANCH * NSA_KV + N_BRANCH * N_HEADS

MOBA_BLOCK = 256
MOBA_TOP = 3
MOBA_Q_BLOCK = 16

FOX_IN = 3 * N_HEADS * HEAD_DIM + N_HEADS
FORGET_BIAS = 3.0

kernel_name = 'hybrid_nsa_moba_fox_macaron_step'


def rmsnorm(x, g):
    xf = x.astype(jnp.float32)
    y = xf * lax.rsqrt(jnp.mean(xf * xf, axis=-1, keepdims=True) + NORM_EPS)
    return (y * g.astype(jnp.float32)).astype(x.dtype)


def swiglu(h, wi, wo):
    a, b = jnp.split(h @ wi, 2, axis=-1)
    return (jax.nn.silu(a) * b) @ wo


def partial_rope(x, pos):
    half = ROT_DIM // 2
    inv = ROPE_THETA ** (-jnp.arange(half, dtype=jnp.float32) / half)
    ang = pos.astype(jnp.float32)[:, None] * inv[None, :]
    cos = jnp.cos(ang)[:, None, :]
    sin = jnp.sin(ang)[:, None, :]
    x1 = x[..., :half].astype(jnp.float32)
    x2 = x[..., half:ROT_DIM].astype(jnp.float32)
    rot = jnp.concatenate([x1 * cos - x2 * sin, x2 * cos + x1 * sin], axis=-1).astype(x.dtype)
    return jnp.concatenate([rot, x[..., ROT_DIM:]], axis=-1)


def masked_softmax(s, mask):
    s = jnp.where(mask, s, -jnp.inf)
    m = jnp.max(s, axis=-1, keepdims=True)
    p = jnp.exp(s - jnp.where(jnp.isfinite(m), m, 0.0))
    d = jnp.sum(p, axis=-1, keepdims=True)
    return p / jnp.where(d > 0, d, 1.0)


def gather_pages(pool, layer, page_table):
    g = pool[layer, page_table]
    return g.reshape((g.shape[0], g.shape[1] * g.shape[2]) + g.shape[3:])


def concat_blocks(parts, blk):
    length = sum(p.shape[1] for p in parts)
    nb = -(-length // blk)
    pad = nb * blk - length
    if pad:
        parts = parts + [jnp.zeros((parts[0].shape[0], pad) + parts[0].shape[2:], parts[0].dtype)]
    kv = jnp.concatenate(parts, axis=1) if len(parts) > 1 else parts[0]
    return kv.reshape((kv.shape[0], nb, blk) + kv.shape[2:])


def gather_blocks(kv_blocks, idx):
    bi = jnp.arange(kv_blocks.shape[0])[:, None, None, None]
    hi = jnp.arange(kv_blocks.shape[4])[None, :, None, None]
    return kv_blocks[bi, idx, :, 0, hi, :], kv_blocks[bi, idx, :, 1, hi, :]


def stack_states(states, k):
    return jnp.stack([s[k] for s in states])


def slc_overlap(nc, nb):
    m = np.zeros((nc, nb), np.float32)
    j = np.arange(nb)
    for a in range(SLC_LEN // CMP_STRIDE):
        for b in range(CMP_LEN // CMP_STRIDE):
            i = (SLC_LEN // CMP_STRIDE) * j + a - b
            ok = (i >= 0) & (i < nc)
            np.add.at(m, (i[ok], j[ok]), 1.0)
    return jnp.asarray(m)


def nsa_project(h, w_in, pos):
    B, T, _ = h.shape
    z = h @ w_in
    q = partial_rope(z[..., :NSA_Q].reshape(B, T, N_HEADS, HEAD_DIM), pos)
    kv = z[..., NSA_Q:NSA_Q + 2 * N_BRANCH * NSA_KV].reshape(B, T, N_BRANCH, 2, NSA_KV_HEADS, HEAD_DIM)
    k = partial_rope(kv[:, :, :, 0].reshape(B, T, N_BRANCH * NSA_KV_HEADS, HEAD_DIM), pos)
    kv = jnp.stack([k.reshape(B, T, N_BRANCH, NSA_KV_HEADS, HEAD_DIM), kv[:, :, :, 1]], axis=3)
    gates = jax.nn.sigmoid(z[..., NSA_Q + 2 * N_BRANCH * NSA_KV:].astype(jnp.float32)).reshape(B, T, N_HEADS, N_BRANCH)
    return q, kv[:, :, 0], kv[:, :, 1], kv[:, :, 2], gates


def nsa_compress(kv_cmp, pe, w1, w2):
    B, L = kv_cmp.shape[:2]
    r = CMP_LEN // CMP_STRIDE
    nc = (L - CMP_LEN) // CMP_STRIDE + 1
    nchunk = nc + r - 1
    x = kv_cmp[:, :nchunk * CMP_STRIDE].reshape(B, nchunk, CMP_STRIDE, 2, NSA_KV_HEADS, HEAD_DIM)
    x = x.transpose(0, 3, 4, 1, 2, 5).reshape(B, 2, NSA_KV_HEADS, nchunk, CMP_STRIDE * HEAD_DIM)
    w1r = w1.reshape(2, r, CMP_STRIDE * HEAD_DIM, CMP_HIDDEN)
    part = jnp.einsum('bchnf,crfe->bchrne', x, w1r)
    pe_bias = jnp.einsum('cf,cfe->ce', pe.reshape(2, CMP_LEN * HEAD_DIM), w1)
    pre = pe_bias[None, :, None, None, :]
    for i in range(r):
        pre = pre + part[:, :, :, i, i:i + nc]
    comp = jnp.einsum('bchne,ced->bchnd', jax.nn.silu(pre), w2)
    return comp[:, 0], comp[:, 1]


def nsa_core(q, q_pos, gates, k_cmp, v_cmp, kv_slc_blocks, kv_win, win_pos, overlap):
    B, Q = q.shape[:2]
    scale = HEAD_DIM ** -0.5
    qg = q.reshape(B, Q, NSA_KV_HEADS, NSA_GROUP, HEAD_DIM).transpose(0, 2, 3, 1, 4)
    nc = k_cmp.shape[2]
    c_end = jnp.arange(nc) * CMP_STRIDE + (CMP_LEN - 1)
    s = jnp.einsum('bhgqd,bhnd->bhgqn', qg, k_cmp).astype(jnp.float32) * scale
    p_cmp = masked_softmax(s, c_end[None, :] <= q_pos[:, None])
    o_cmp = jnp.einsum('bhgqn,bhnd->bhgqd', p_cmp.astype(v_cmp.dtype), v_cmp)
    nb = kv_slc_blocks.shape[1]
    imp = jnp.einsum('bhgqn,nj->bhqj', p_cmp, overlap)
    blk = jnp.arange(nb)
    cur = q_pos // SLC_LEN
    lag = cur[:, None] - blk[None, :]
    valid = lag >= 0
    forced = (blk[None, :] == 0) | (valid & (lag < SLC_LOCAL))
    score = jnp.where(valid, jnp.where(forced, jnp.inf, imp), -jnp.inf)
    _, idx = lax.top_k(score, min(SLC_TOP, nb))
    k_s, v_s = gather_blocks(kv_slc_blocks, idx)
    n_keys = idx.shape[-1] * SLC_LEN
    kpos = idx[..., None] * SLC_LEN + jnp.arange(SLC_LEN)
    smask = (kpos <= q_pos[:, None, None]).reshape(B, NSA_KV_HEADS, 1, Q, n_keys)
    s = jnp.einsum('bhgqd,bhqnkd->bhgqnk', qg, k_s).astype(jnp.float32) * scale
    p = masked_softmax(s.reshape(B, NSA_KV_HEADS, NSA_GROUP, Q, n_keys), smask)
    o_slc = jnp.einsum('bhgqm,bhqmd->bhgqd', p.astype(v_s.dtype), v_s.reshape(B, NSA_KV_HEADS, Q, n_keys, HEAD_DIM))
    dpos = q_pos[:, None] - win_pos[None, :]
    wmask = (dpos >= 0) & (dpos < WINDOW) & (win_pos >= 0)[None, :]
    s = jnp.einsum('bhgqd,bkhd->bhgqk', qg, kv_win[:, :, 0]).astype(jnp.float32) * scale
    p = masked_softmax(s, wmask)
    o_win = jnp.einsum('bhgqk,bkhd->bhgqd', p.astype(kv_win.dtype), kv_win[:, :, 1])
    g = gates.astype(o_cmp.dtype).reshape(B, Q, NSA_KV_HEADS, NSA_GROUP, N_BRANCH).transpose(0, 2, 3, 1, 4)
    o = o_cmp * g[..., 0:1] + o_slc * g[..., 1:2] + o_win * g[..., 2:3]
    return o.transpose(0, 3, 1, 2, 4).reshape(B, Q, N_HEADS * HEAD_DIM)


def nsa_prompt(h, w_in, pe, w1, w2, w_out):
    B, T, _ = h.shape
    pos = jnp.arange(T, dtype=jnp.int32)
    q, kv_cmp, kv_slc, kv_win, gates = nsa_project(h, w_in, pos)
    k_cmp, v_cmp = nsa_compress(kv_cmp, pe, w1, w2)
    slc_blocks = concat_blocks([kv_slc], SLC_LEN)
    overlap = slc_overlap(k_cmp.shape[2], slc_blocks.shape[1])
    win_pad = jnp.pad(kv_win, ((0, 0), (WINDOW, 0), (0, 0), (0, 0), (0, 0)))

    def one_block(q0):
        qb = lax.dynamic_slice_in_dim(q, q0, Q_BLOCK, axis=1)
        gb = lax.dynamic_slice_in_dim(gates, q0, Q_BLOCK, axis=1)
        kw = lax.dynamic_slice_in_dim(win_pad, q0, WINDOW + Q_BLOCK, axis=1)
        q_pos = q0 + jnp.arange(Q_BLOCK, dtype=jnp.int32)
        w_pos = q0 - WINDOW + jnp.arange(WINDOW + Q_BLOCK, dtype=jnp.int32)
        return nsa_core(qb, q_pos, gb, k_cmp, v_cmp, slc_blocks, kw, w_pos, overlap)

    o = lax.map(one_block, jnp.arange(0, T, Q_BLOCK, dtype=jnp.int32))
    o = o.transpose(1, 0, 2, 3).reshape(B, T, N_HEADS * HEAD_DIM)
    return o @ w_out, (kv_cmp, kv_slc, kv_win[:, T - min(WINDOW, T):])


def nsa_sample(h, pool_cmp, pool_slc, win_buf, page_table, layer, w_in, pe, w1, w2, w_out):
    B, T, _ = h.shape
    past = page_table.shape[1] * PAGE_SIZE
    pos = past + jnp.arange(T, dtype=jnp.int32)
    q, kv_cmp, kv_slc, kv_win, gates = nsa_project(h, w_in, pos)
    full_cmp = jnp.concatenate([gather_pages(pool_cmp, layer, page_table), kv_cmp], axis=1)
    k_cmp, v_cmp = nsa_compress(full_cmp, pe, w1, w2)
    slc_blocks = concat_blocks([gather_pages(pool_slc, layer, page_table), kv_slc], SLC_LEN)
    overlap = slc_overlap(k_cmp.shape[2], slc_blocks.shape[1])
    buf = win_buf[layer]
    kw = jnp.concatenate([buf, kv_win], axis=1)
    w_pos = past - buf.shape[1] + jnp.arange(kw.shape[1], dtype=jnp.int32)
    o = nsa_core(q, pos, gates, k_cmp, v_cmp, slc_blocks, kw, w_pos, overlap)
    return o @ w_out, (kv_cmp, kv_slc, kw[:, T:])


def moba_project(h, w_in, pos):
    B, T, _ = h.shape
    z = (h @ w_in).reshape(B, T, 3, N_HEADS, HEAD_DIM)
    q = partial_rope(z[:, :, 0], pos)
    k = partial_rope(z[:, :, 1], pos)
    return q, jnp.stack([k, z[:, :, 2]], axis=2)


def block_key_means(kv_blocks):
    return jnp.mean(kv_blocks[:, :, :, 0].astype(jnp.float32), axis=2).astype(kv_blocks.dtype)


def moba_core(q, q_pos, kv_blocks, k_means):
    B, Q = q.shape[:2]
    nb = kv_blocks.shape[1]
    own = q_pos // MOBA_BLOCK
    gate = jnp.einsum('bqhd,bnhd->bhqn', q, k_means).astype(jnp.float32)
    gate = jnp.where(jnp.arange(nb)[None, :] < own[:, None], gate, -jnp.inf)
    g_val, g_idx = lax.top_k(gate, min(MOBA_TOP, nb))
    own_idx = jnp.broadcast_to(own[None, None, :, None], (B, N_HEADS, Q, 1)).astype(g_idx.dtype)
    idx = jnp.concatenate([g_idx, own_idx], axis=-1)
    ok = jnp.concatenate([jnp.isfinite(g_val), jnp.ones((B, N_HEADS, Q, 1), bool)], axis=-1)
    k_g, v_g = gather_blocks(kv_blocks, idx)
    n_keys = idx.shape[-1] * MOBA_BLOCK
    kpos = idx[..., None] * MOBA_BLOCK + jnp.arange(MOBA_BLOCK)
    mask = (ok[..., None] & (kpos <= q_pos[:, None, None])).reshape(B, N_HEADS, Q, n_keys)
    s = jnp.einsum('bqhd,bhqnkd->bhqnk', q, k_g).astype(jnp.float32) * HEAD_DIM ** -0.5
    p = masked_softmax(s.reshape(B, N_HEADS, Q, n_keys), mask)
    o = jnp.einsum('bhqm,bhqmd->bqhd', p.astype(v_g.dtype), v_g.reshape(B, N_HEADS, Q, n_keys, HEAD_DIM))
    return o.reshape(B, Q, N_HEADS * HEAD_DIM)


def moba_prompt(h, w_in, w_out):
    B, T, _ = h.shape
    pos = jnp.arange(T, dtype=jnp.int32)
    q, kv = moba_project(h, w_in, pos)
    kv_blocks = concat_blocks([kv], MOBA_BLOCK)
    k_means = block_key_means(kv_blocks)

    def one_block(q0):
        qb = lax.dynamic_slice_in_dim(q, q0, MOBA_Q_BLOCK, axis=1)
        return moba_core(qb, q0 + jnp.arange(MOBA_Q_BLOCK, dtype=jnp.int32), kv_blocks, k_means)

    o = lax.map(one_block, jnp.arange(0, T, MOBA_Q_BLOCK, dtype=jnp.int32))
    o = o.transpose(1, 0, 2, 3).reshape(B, T, N_HEADS * HEAD_DIM)
    return o @ w_out, (kv,)


def moba_sample(h, pool_kv, page_table, layer, w_in, w_out):
    B, T, _ = h.shape
    past = page_table.shape[1] * PAGE_SIZE
    pos = past + jnp.arange(T, dtype=jnp.int32)
    q, kv = moba_project(h, w_in, pos)
    kv_blocks = concat_blocks([gather_pages(pool_kv, layer, page_table), kv], MOBA_BLOCK)
    o = moba_core(q, pos, kv_blocks, block_key_means(kv_blocks))
    return o @ w_out, (kv,)


def fox_project(h, w_in, b_f):
    B, T, _ = h.shape
    z = h @ w_in
    qkv = z[..., :3 * N_HEADS * HEAD_DIM].reshape(B, T, 3, N_HEADS, HEAD_DIM)
    log_f = jax.nn.log_sigmoid(z[..., 3 * N_HEADS * HEAD_DIM:].astype(jnp.float32) + b_f.astype(jnp.float32))
    return qkv[:, :, 0], qkv[:, :, 1:], log_f


def fox_core(q, q_pos, c_q, kv, c_k):
    s = jnp.einsum('bqhd,bkhd->bhqk', q, kv[:, :, 0]).astype(jnp.float32) * HEAD_DIM ** -0.5
    s = s + c_q.transpose(0, 2, 1)[..., None] - c_k.transpose(0, 2, 1)[:, :, None, :]
    mask = jnp.arange(kv.shape[1])[None, :] <= q_pos[:, None]
    p = masked_softmax(s, mask)
    o = jnp.einsum('bhqk,bkhd->bqhd', p.astype(kv.dtype), kv[:, :, 1])
    return o.reshape(q.shape[0], q.shape[1], N_HEADS * HEAD_DIM)


def fox_prompt(h, w_in, b_f, w_out):
    B, T, _ = h.shape
    q, kv, log_f = fox_project(h, w_in, b_f)
    c = jnp.cumsum(log_f, axis=1)

    def one_block(q0):
        qb = lax.dynamic_slice_in_dim(q, q0, Q_BLOCK, axis=1)
        cq = lax.dynamic_slice_in_dim(c, q0, Q_BLOCK, axis=1)
        return fox_core(qb, q0 + jnp.arange(Q_BLOCK, dtype=jnp.int32), cq, kv, c)

    o = lax.map(one_block, jnp.arange(0, T, Q_BLOCK, dtype=jnp.int32))
    o = o.transpose(1, 0, 2, 3).reshape(B, T, N_HEADS * HEAD_DIM)
    return o @ w_out, (kv, log_f)


def fox_sample(h, pool_kv, pool_logf, page_table, layer, w_in, b_f, w_out):
    B, T, _ = h.shape
    past = page_table.shape[1] * PAGE_SIZE
    pos = past + jnp.arange(T, dtype=jnp.int32)
    q, kv, log_f = fox_project(h, w_in, b_f)
    full_logf = jnp.concatenate([gather_pages(pool_logf, layer, page_table).astype(jnp.float32), log_f], axis=1)
    c = jnp.cumsum(full_logf, axis=1)
    full_kv = jnp.concatenate([gather_pages(pool_kv, layer, page_table), kv], axis=1)
    o = fox_core(q, pos, c[:, past:], full_kv, c)
    return o @ w_out, (kv, log_f)


def setup_inputs(seed: int = 0) -> dict:
    key = jax.random.key(seed)
    ks = jax.random.split(key, 24)
    n_pages = PAST_LEN // PAGE_SIZE
    n_pool = (DEC_BATCH * n_pages * 5) // 4
    w_buf = min(WINDOW, PAST_LEN)
    hd = N_HEADS * HEAD_DIM

    def nrm(k, shape, scale):
        return jax.random.normal(k, shape, jnp.float32) * scale

    return {
        'x_prompt': jax.random.normal(ks[0], (BATCH, SEQ, D_MODEL), jnp.float32),
        'x_sample': jax.random.normal(ks[1], (DEC_BATCH, DEC_SEQ, D_MODEL), jnp.float32),
        'cache_nsa_cmp': jax.random.normal(ks[2], (N_NSA, n_pool, PAGE_SIZE, 2, NSA_KV_HEADS, HEAD_DIM), jnp.float32),
        'cache_nsa_slc': jax.random.normal(ks[3], (N_NSA, n_pool, PAGE_SIZE, 2, NSA_KV_HEADS, HEAD_DIM), jnp.float32),
        'cache_nsa_win': jax.random.normal(ks[4], (N_NSA, DEC_BATCH, w_buf, 2, NSA_KV_HEADS, HEAD_DIM), jnp.float32),
        'cache_moba_kv': jax.random.normal(ks[5], (N_MOBA, n_pool, PAGE_SIZE, 2, N_HEADS, HEAD_DIM), jnp.float32),
        'cache_fox_kv': jax.random.normal(ks[6], (N_FOX, n_pool, PAGE_SIZE, 2, N_HEADS, HEAD_DIM), jnp.float32),
        'cache_fox_logf': jax.nn.log_sigmoid(FORGET_BIAS + jax.random.normal(ks[7], (N_FOX, n_pool, PAGE_SIZE, N_HEADS), jnp.float32)),
        'page_table': jax.random.permutation(ks[8], n_pool)[:DEC_BATCH * n_pages].reshape(DEC_BATCH, n_pages).astype(jnp.int32),
        'norms': 1.0 + nrm(ks[9], (DEPTH, 3, D_MODEL), 0.05),
        'ffn_wi': nrm(ks[10], (DEPTH, 2, D_MODEL, 2 * FFN_DIM), D_MODEL ** -0.5),
        'ffn_wo': nrm(ks[11], (DEPTH, 2, FFN_DIM, D_MODEL), FFN_DIM ** -0.5),
        'final_norm': 1.0 + nrm(ks[12], (D_MODEL,), 0.05),
        'nsa_w_in': nrm(ks[13], (N_NSA, D_MODEL, NSA_IN), D_MODEL ** -0.5),
        'nsa_cmp_pe': nrm(ks[14], (N_NSA, 2, CMP_LEN, HEAD_DIM), 0.1),
        'nsa_cmp_w1': nrm(ks[15], (N_NSA, 2, CMP_LEN * HEAD_DIM, CMP_HIDDEN), (CMP_LEN * HEAD_DIM) ** -0.5),
        'nsa_cmp_w2': nrm(ks[16], (N_NSA, 2, CMP_HIDDEN, HEAD_DIM), CMP_HIDDEN ** -0.5),
        'nsa_w_out': nrm(ks[17], (N_NSA, hd, D_MODEL), hd ** -0.5),
        'moba_w_in': nrm(ks[18], (N_MOBA, D_MODEL, 3 * hd), D_MODEL ** -0.5),
        'moba_w_out': nrm(ks[19], (N_MOBA, hd, D_MODEL), hd ** -0.5),
        'fox_w_in': nrm(ks[20], (N_FOX, D_MODEL, FOX_IN), D_MODEL ** -0.5),
        'fox_b_f': FORGET_BIAS + nrm(ks[21], (N_FOX, N_HEADS), 0.1),
        'fox_w_out': nrm(ks[22], (N_FOX, hd, D_MODEL), hd ** -0.5),
    }


def reference(x_prompt, x_sample, cache_nsa_cmp, cache_nsa_slc, cache_nsa_win, cache_moba_kv, cache_fox_kv,
              cache_fox_logf, page_table, norms, ffn_wi, ffn_wo, final_norm, nsa_w_in, nsa_cmp_pe, nsa_cmp_w1,
              nsa_cmp_w2, nsa_w_out, moba_w_in, moba_w_out, fox_w_in, fox_b_f, fox_w_out):
    xp, xs = x_prompt, x_sample
    nsa_p, nsa_s, moba_p, moba_s, fox_p, fox_s = [], [], [], [], [], []
    for i in range(DEPTH):
        kind = i % N_MIXERS
        j = i // N_MIXERS
        xp = xp + 0.5 * swiglu(rmsnorm(xp, norms[i, 0]), ffn_wi[i, 0], ffn_wo[i, 0])
        xs = xs + 0.5 * swiglu(rmsnorm(xs, norms[i, 0]), ffn_wi[i, 0], ffn_wo[i, 0])
        hp = rmsnorm(xp, norms[i, 1])
        hs = rmsnorm(xs, norms[i, 1])
        if kind == 0:
            yp, st_p = nsa_prompt(hp, nsa_w_in[j], nsa_cmp_pe[j], nsa_cmp_w1[j], nsa_cmp_w2[j], nsa_w_out[j])
            ys, st_s = nsa_sample(hs, cache_nsa_cmp, cache_nsa_slc, cache_nsa_win, page_table, j,
                                  nsa_w_in[j], nsa_cmp_pe[j], nsa_cmp_w1[j], nsa_cmp_w2[j], nsa_w_out[j])
            nsa_p.append(st_p)
            nsa_s.append(st_s)
        elif kind == 1:
            yp, st_p = moba_prompt(hp, moba_w_in[j], moba_w_out[j])
            ys, st_s = moba_sample(hs, cache_moba_kv, page_table, j, moba_w_in[j], moba_w_out[j])
            moba_p.append(st_p)
            moba_s.append(st_s)
        else:
            yp, st_p = fox_prompt(hp, fox_w_in[j], fox_b_f[j], fox_w_out[j])
            ys, st_s = fox_sample(hs, cache_fox_kv, cache_fox_logf, page_table, j, fox_w_in[j], fox_b_f[j], fox_w_out[j])
            fox_p.append(st_p)
            fox_s.append(st_s)
        xp = xp + yp
        xs = xs + ys
        xp = xp + 0.5 * swiglu(rmsnorm(xp, norms[i, 2]), ffn_wi[i, 1], ffn_wo[i, 1])
        xs = xs + 0.5 * swiglu(rmsnorm(xs, norms[i, 2]), ffn_wi[i, 1], ffn_wo[i, 1])
    y_prompt = rmsnorm(xp, final_norm)
    y_sample = rmsnorm(xs, final_norm)
    return (y_prompt, y_sample,
            stack_states(nsa_p, 0), stack_states(nsa_p, 1), stack_states(nsa_p, 2),
            stack_states(nsa_s, 0), stack_states(nsa_s, 1), stack_states(nsa_s, 2),
            stack_states(moba_p, 0), stack_states(moba_s, 0),
            stack_states(fox_p, 0), stack_states(fox_p, 1), stack_states(fox_s, 0), stack_states(fox_s, 1))
```

```python
import functools

import numpy as np
import jax
import jax.numpy as jnp
from jax import lax
from jax.experimental import pallas as pl
from jax.experimental.pallas import tpu as pltpu

F32 = jnp.float32
BF16 = jnp.bfloat16
I32 = jnp.int32

LANES = 128
VMEM_LIMIT_BYTES = 56 << 20

N_HEADS = 16
HEAD_DIM = 128
ROT_DIM = HEAD_DIM // 4
ROPE_THETA = 500000.0
NORM_EPS = 1e-6
PAGE = 128
NSA_KV_HEADS = 4
NSA_GROUP = N_HEADS // NSA_KV_HEADS
CMP_LEN = 32
CMP_STRIDE = 16
SLC_LEN = 64
SLC_TOP = 16
SLC_LOCAL = 2
WINDOW = 512
N_BRANCH = 3
MOBA_BLOCK = 256
MOBA_TOP = 3
SCALE = HEAD_DIM ** -0.5
NEG = -1e30
NEG_INF = float("-inf")
POS_INF = float("inf")


def _cparams(*sem):
    return pltpu.CompilerParams(dimension_semantics=sem, vmem_limit_bytes=VMEM_LIMIT_BYTES)


def _nt(a, b):
    return lax.dot_general(a, b, (((1,), (1,)), ((), ())), preferred_element_type=F32)


def _dot(a, b):
    return jnp.dot(a, b, preferred_element_type=F32)


def _iota(shape, dim):
    return lax.broadcasted_iota(I32, shape, dim)


def _lane_col(x, idx):
    return jnp.sum(jnp.where(_iota(x.shape, 1) == idx, x, 0.0), axis=-1, keepdims=True)


def _rms_to_bf16(x_ref, g_ref):
    x = x_ref[...]
    var = jnp.mean(x * x, axis=-1, keepdims=True)
    return ((x * lax.rsqrt(var + NORM_EPS)) * g_ref[...]).astype(BF16)


def _log_sigmoid(x):
    return jnp.minimum(x, 0.0) - jnp.log1p(jnp.exp(-jnp.abs(x)))


def _norm_mm_kernel(*refs, epi, tn):
    if epi == "rope":
        flags_ref, x_ref, g_ref, w_ref, c_ref, s1_ref, s2_ref, o_ref, xn_ref = refs
    elif epi == "logsig":
        x_ref, g_ref, w_ref, b_ref, o_ref, xn_ref = refs
    else:
        x_ref, g_ref, w_ref, o_ref, xn_ref = refs
    j = pl.program_id(1)

    @pl.when(j == 0)
    def _():
        xn_ref[...] = _rms_to_bf16(x_ref, g_ref)

    z = _dot(xn_ref[...], w_ref[...].astype(BF16))
    if epi == "rope":
        @pl.when(flags_ref[j] == 1)
        def _():
            c, s1, s2 = c_ref[...], s1_ref[...], s2_ref[...]
            for hh in range(tn // HEAD_DIM):
                zs = z[:, hh * HEAD_DIM:(hh + 1) * HEAD_DIM]
                o_ref[:, hh * HEAD_DIM:(hh + 1) * HEAD_DIM] = (
                    zs * c + pltpu.roll(zs, HEAD_DIM - ROT_DIM // 2, 1) * s1 + pltpu.roll(zs, ROT_DIM // 2, 1) * s2)

        @pl.when(flags_ref[j] == 0)
        def _():
            o_ref[...] = z
    elif epi == "sigmoid":
        o_ref[...] = jax.nn.sigmoid(z)
    elif epi == "logsig":
        o_ref[...] = _log_sigmoid(z + b_ref[...])
    else:
        o_ref[...] = z


def _norm_mm(x, g, w, n_out, *, tm, tn=512, epi="none", rope=None, bias=None):
    M, D = x.shape
    tn = min(tn, n_out)
    assert M % tm == 0 and n_out % tn == 0
    grid = (M // tm, n_out // tn)
    g2 = g.reshape(1, D)
    kern = functools.partial(_norm_mm_kernel, epi=epi, tn=tn)
    scratch = [pltpu.VMEM((tm, D), BF16)]
    out_shape = jax.ShapeDtypeStruct((M, n_out), F32)
    if epi == "rope":
        flags, c, s1, s2 = rope
        nt = c.shape[0] // tm
        tab = pl.BlockSpec((tm, HEAD_DIM), lambda i, j, f: (i % nt, 0))
        gs = pltpu.PrefetchScalarGridSpec(
            num_scalar_prefetch=1, grid=grid,
            in_specs=[pl.BlockSpec((tm, D), lambda i, j, f: (i, 0)),
                      pl.BlockSpec((1, D), lambda i, j, f: (0, 0)),
                      pl.BlockSpec((D, tn), lambda i, j, f: (0, j)),
                      tab, tab, tab],
            out_specs=pl.BlockSpec((tm, tn), lambda i, j, f: (i, j)),
            scratch_shapes=scratch)
        return pl.pallas_call(kern, grid_spec=gs, out_shape=out_shape, name="norm_mm_rope",
                              compiler_params=_cparams("parallel", "arbitrary"))(flags, x, g2, w, c, s1, s2)
    in_specs = [pl.BlockSpec((tm, D), lambda i, j: (i, 0)),
                pl.BlockSpec((1, D), lambda i, j: (0, 0)),
                pl.BlockSpec((D, tn), lambda i, j: (0, j))]
    args = [x, g2, w]
    if epi == "logsig":
        in_specs.append(pl.BlockSpec((1, tn), lambda i, j: (0, j)))
        args.append(bias.reshape(1, n_out))
    return pl.pallas_call(kern, grid=grid, in_specs=in_specs,
                          out_specs=pl.BlockSpec((tm, tn), lambda i, j: (i, j)),
                          out_shape=out_shape, scratch_shapes=scratch, name="norm_mm_" + epi,
                          compiler_params=_cparams("parallel", "arbitrary"))(*args)


def _swiglu_up_kernel(x_ref, g_ref, wa_ref, *rest, tn, nsub):
    wb_refs = rest[:nsub]
    o_ref, xn_ref, wb_scr = rest[nsub:]

    @pl.when(pl.program_id(1) == 0)
    def _():
        xn_ref[...] = _rms_to_bf16(x_ref, g_ref)

    for r in range(nsub):
        wb_scr[:, r * LANES:(r + 1) * LANES] = wb_refs[r][...].astype(BF16)
    xn = xn_ref[...]
    a = _dot(xn, wa_ref[...].astype(BF16))
    b = _dot(xn, wb_scr[...])
    o_ref[...] = (a * jax.nn.sigmoid(a) * b).astype(o_ref.dtype)


def _swiglu_up(x, g, wi, *, tm, tn=512):
    M, D = x.shape
    F = wi.shape[1] // 2
    assert F % LANES == 0 and M % tm == 0
    nsub = tn // LANES
    off = F // LANES
    last = wi.shape[1] // LANES - 1
    nj = pl.cdiv(F, tn)
    in_specs = [pl.BlockSpec((tm, D), lambda i, j: (i, 0)),
                pl.BlockSpec((1, D), lambda i, j: (0, 0)),
                pl.BlockSpec((D, tn), lambda i, j: (0, j))]
    for r in range(nsub):
        in_specs.append(pl.BlockSpec((D, LANES), lambda i, j, r=r: (0, jnp.minimum(off + nsub * j + r, last))))
    return pl.pallas_call(
        functools.partial(_swiglu_up_kernel, tn=tn, nsub=nsub),
        grid=(M // tm, nj), in_specs=in_specs,
        out_specs=pl.BlockSpec((tm, tn), lambda i, j: (i, j)),
        out_shape=jax.ShapeDtypeStruct((M, F), BF16),
        scratch_shapes=[pltpu.VMEM((tm, D), BF16), pltpu.VMEM((D, tn), BF16)],
        name="swiglu_up",
        compiler_params=_cparams("parallel", "arbitrary"))(x, g.reshape(1, D), wi, *([wi] * nsub))


def _mm_res_kernel(u_ref, w_ref, r_ref, o_ref, *, scale):
    y = _dot(u_ref[...].astype(BF16), w_ref[...].astype(BF16))
    o_ref[...] = r_ref[...] + (y if scale == 1.0 else scale * y)


def _mm_res(u, w, res, scale, *, tm, tn):
    M, K = u.shape
    N = w.shape[1]
    assert M % tm == 0 and N % tn == 0
    return pl.pallas_call(
        functools.partial(_mm_res_kernel, scale=scale),
        grid=(M // tm, N // tn),
        in_specs=[pl.BlockSpec((tm, K), lambda i, j: (i, 0)),
                  pl.BlockSpec((K, tn), lambda i, j: (0, j)),
                  pl.BlockSpec((tm, tn), lambda i, j: (i, j))],
        out_specs=pl.BlockSpec((tm, tn), lambda i, j: (i, j)),
        out_shape=jax.ShapeDtypeStruct((M, N), F32), name="mm_res",
        compiler_params=_cparams("parallel", "arbitrary"))(u, w, res)


def _rmsnorm_kernel(x_ref, g_ref, o_ref):
    x = x_ref[...]
    var = jnp.mean(x * x, axis=-1, keepdims=True)
    o_ref[...] = (x * lax.rsqrt(var + NORM_EPS)) * g_ref[...]


def _rmsnorm(x, g, *, tm):
    M, D = x.shape
    return pl.pallas_call(
        _rmsnorm_kernel, grid=(M // tm,),
        in_specs=[pl.BlockSpec((tm, D), lambda i: (i, 0)), pl.BlockSpec((1, D), lambda i: (0, 0))],
        out_specs=pl.BlockSpec((tm, D), lambda i: (i, 0)),
        out_shape=jax.ShapeDtypeStruct((M, D), F32), name="rmsnorm",
        compiler_params=_cparams("parallel"))(x, g.reshape(1, D))


def _rope_tables(pos):
    half = ROT_DIM // 2
    inv = ROPE_THETA ** (-jnp.arange(half, dtype=F32) / half)
    ang = pos.astype(F32)[:, None] * inv[None, :]
    cos, sin = jnp.cos(ang), jnp.sin(ang)
    n = pos.shape[0]
    zeros = jnp.zeros((n, HEAD_DIM - ROT_DIM), F32)
    zh = jnp.zeros((n, half), F32)
    c = jnp.concatenate([cos, cos, jnp.ones((n, HEAD_DIM - ROT_DIM), F32)], axis=1)
    s1 = jnp.concatenate([-sin, zh, zeros], axis=1)
    s2 = jnp.concatenate([zh, sin, zeros], axis=1)
    return c, s1, s2


def _online_update(s, ok, m, l, acc, v):
    sm = jnp.where(ok, s, NEG)
    m_new = jnp.maximum(m, jnp.max(sm, axis=-1, keepdims=True))
    alpha = jnp.exp(m - m_new)
    p = jnp.where(ok, jnp.exp(sm - m_new), 0.0)
    l_new = alpha * l + jnp.sum(p, axis=-1, keepdims=True)
    acc_new = alpha * acc + _dot(p.astype(BF16), v)
    return m_new, l_new, acc_new


def _flash_init(rows, width=HEAD_DIM):
    return (jnp.full((rows, 1), NEG, F32), jnp.zeros((rows, 1), F32), jnp.zeros((rows, width), F32))


def _flash_finish(l, acc):
    return acc / jnp.where(l > 0.0, l, 1.0)


def _masked_softmax(s, ok):
    sm = jnp.where(ok, s, NEG)
    mx = jnp.max(sm, axis=-1, keepdims=True)
    e = jnp.where(ok, jnp.exp(sm - mx), 0.0)
    d = jnp.sum(e, axis=-1, keepdims=True)
    return e / jnp.where(d > 0.0, d, 1.0)


def _top_k_mask(score, k):
    lane = _iota(score.shape, 1)
    width = score.shape[1]
    sel = jnp.zeros(score.shape, F32)
    found = []
    for _ in range(k):
        mx = jnp.max(score, axis=-1, keepdims=True)
        first = jnp.min(jnp.where(score == mx, lane, width), axis=-1, keepdims=True)
        pick = lane == first
        sel = jnp.where(pick, 1.0, sel)
        score = jnp.where(pick, NEG_INF, score)
        found.append((mx, pick))
    return sel, found


def _tile_cumsum(x):
    lane = _iota(x.shape, 1)
    for s in (1, 2, 4, 8, 16, 32, 64):
        x = x + jnp.where(lane >= s, pltpu.roll(x, s, 1), 0.0)
    return x


def _cumsum_rows_kernel(x_ref, o_ref, *, n_tiles):
    carry = jnp.zeros((x_ref.shape[0], 1), F32)
    for j in range(n_tiles):
        cs = _tile_cumsum(x_ref[:, j * LANES:(j + 1) * LANES]) + carry
        o_ref[:, j * LANES:(j + 1) * LANES] = cs
        carry = cs[:, LANES - 1:LANES]


def _cumsum_rows(xt):
    B, H, T = xt.shape
    return pl.pallas_call(
        functools.partial(_cumsum_rows_kernel, n_tiles=T // LANES), grid=(B,),
        in_specs=[pl.BlockSpec((None, H, T), lambda b: (b, 0, 0))],
        out_specs=pl.BlockSpec((None, H, T), lambda b: (b, 0, 0)),
        out_shape=jax.ShapeDtypeStruct((B, H, T), F32), name="cumsum_rows",
        compiler_params=_cparams("parallel"))(xt)


def _fox_prompt_kernel(q_ref, k_ref, v_ref, c_ref, ct_ref, o_ref, *, tq, tk):
    h = pl.program_id(1)
    q0 = pl.program_id(2) * tq
    q = q_ref[...].astype(BF16)
    cq = _lane_col(c_ref[...], h)
    qpos = q0 + _iota((tq, tk), 0)
    kio = _iota((tq, tk), 1)

    def body(j, carry):
        k0 = pl.multiple_of(j * tk, tk)
        k = k_ref[pl.ds(k0, tk), :].astype(BF16)
        v = v_ref[pl.ds(k0, tk), :].astype(BF16)
        ck = ct_ref[pl.ds(h, 1), pl.ds(k0, tk)]
        s = _nt(q, k) * SCALE + cq - ck
        return _online_update(s, (k0 + kio) <= qpos, *carry, v)

    nk = (q0 + tq + tk - 1) // tk
    _, l, acc = lax.fori_loop(0, nk, body, _flash_init(tq))
    o_ref[...] = _flash_finish(l, acc).astype(o_ref.dtype)


def _fox_prompt_attn(z, c, ct, B, T, *, tq=256, tk=256):
    nq = T // tq
    H = N_HEADS
    return pl.pallas_call(
        functools.partial(_fox_prompt_kernel, tq=tq, tk=tk), grid=(B, H, nq),
        in_specs=[pl.BlockSpec((tq, HEAD_DIM), lambda b, h, i: (b * nq + i, h)),
                  pl.BlockSpec((T, HEAD_DIM), lambda b, h, i: (b, H + h)),
                  pl.BlockSpec((T, HEAD_DIM), lambda b, h, i: (b, 2 * H + h)),
                  pl.BlockSpec((tq, LANES), lambda b, h, i: (b * nq + i, 0)),
                  pl.BlockSpec((None, H, T), lambda b, h, i: (b, 0, 0))],
        out_specs=pl.BlockSpec((tq, HEAD_DIM), lambda b, h, i: (b * nq + i, h)),
        out_shape=jax.ShapeDtypeStruct((B * T, H * HEAD_DIM), BF16), name="fox_prompt",
        compiler_params=_cparams("parallel", "parallel", "arbitrary"))(z, z, z, c, ct)


def _moba_prompt_kernel(q_ref, k_ref, v_ref, o_ref, km_ref, *, tq, nb):
    qi = pl.program_id(2)
    q0 = qi * tq

    @pl.when(qi == 0)
    def _():
        km_ref[...] = jnp.zeros(km_ref.shape, F32)
        for n in range(nb):
            km_ref[n:n + 1, :] = jnp.mean(k_ref[n * MOBA_BLOCK:(n + 1) * MOBA_BLOCK, :], axis=0, keepdims=True)

    q = q_ref[...].astype(BF16)
    own = qi
    lane = _iota((tq, LANES), 1)
    gate = jnp.where(lane < own, _nt(q, km_ref[...].astype(BF16)), NEG_INF)
    _, found = _top_k_mask(gate, MOBA_TOP)
    sel = jnp.zeros((tq, LANES), F32)
    for mx, pick in found:
        sel = jnp.where(pick & (mx > NEG_INF), 1.0, sel)
    qpos = q0 + _iota((tq, MOBA_BLOCK), 0)
    kio = _iota((tq, MOBA_BLOCK), 1)

    def body(n, carry):
        k0 = pl.multiple_of(n * MOBA_BLOCK, MOBA_BLOCK)
        k = k_ref[pl.ds(k0, MOBA_BLOCK), :].astype(BF16)
        v = v_ref[pl.ds(k0, MOBA_BLOCK), :].astype(BF16)
        s = _nt(q, k) * SCALE
        causal = jnp.where((k0 + kio) <= qpos, 1.0, 0.0)
        ok = jnp.where(n == own, causal, _lane_col(sel, n)) > 0.5
        return _online_update(s, ok, *carry, v)

    _, l, acc = lax.fori_loop(0, own + 1, body, _flash_init(tq))
    o_ref[...] = _flash_finish(l, acc).astype(o_ref.dtype)


def _moba_prompt_attn(z, B, T):
    tq = MOBA_BLOCK
    assert T % MOBA_BLOCK == 0 and T // MOBA_BLOCK <= LANES
    nq = T // tq
    H = N_HEADS
    return pl.pallas_call(
        functools.partial(_moba_prompt_kernel, tq=tq, nb=T // MOBA_BLOCK), grid=(B, H, nq),
        in_specs=[pl.BlockSpec((tq, HEAD_DIM), lambda b, h, i: (b * nq + i, h)),
                  pl.BlockSpec((T, HEAD_DIM), lambda b, h, i: (b, H + h)),
                  pl.BlockSpec((T, HEAD_DIM), lambda b, h, i: (b, 2 * H + h))],
        out_specs=pl.BlockSpec((tq, HEAD_DIM), lambda b, h, i: (b * nq + i, h)),
        out_shape=jax.ShapeDtypeStruct((B * T, H * HEAD_DIM), BF16),
        scratch_shapes=[pltpu.VMEM((LANES, HEAD_DIM), F32)], name="moba_prompt",
        compiler_params=_cparams("parallel", "parallel", "arbitrary"))(z, z, z)


ROWS_PER_KV = 2 * NSA_KV_HEADS
CHUNKS_PER_PAGE = PAGE // CMP_STRIDE


def _compress_kernel(pt_ref, x_ref, w1_ref, w2_ref, pe_ref, o_ref, xcat_ref, carry_ref, *, ppg):
    del pt_ref
    p = pl.program_id(1)
    slot = p % ppg
    base = pl.multiple_of(slot * CHUNKS_PER_PAGE, CHUNKS_PER_PAGE)
    for combo in range(ROWS_PER_KV):
        for i in range(CMP_STRIDE):
            xcat_ref[combo, pl.ds(base, CHUNKS_PER_PAGE), i * HEAD_DIM:(i + 1) * HEAD_DIM] = (
                x_ref[pl.ds(i * ROWS_PER_KV + combo, CHUNKS_PER_PAGE, stride=CMP_STRIDE * ROWS_PER_KV), :])

    @pl.when(p == 0)
    def _():
        carry_ref[...] = jnp.zeros(carry_ref.shape, F32)

    @pl.when(slot == ppg - 1)
    def _():
        nch = ppg * CHUNKS_PER_PAGE
        first = _iota((nch, HEAD_DIM), 0) == 0
        for combo in range(ROWS_PER_KV):
            c = combo // NSA_KV_HEADS
            w1 = w1_ref[c]
            pb = _dot(pe_ref[c].astype(BF16), w1)
            bias = pb[0:1, :HEAD_DIM] + pb[1:2, HEAD_DIM:]
            part = _dot(xcat_ref[combo].astype(BF16), w1)
            p0, p1 = part[:, :HEAD_DIM], part[:, HEAD_DIM:]
            p0_prev = jnp.where(first, carry_ref[combo:combo + 1, :], pltpu.roll(p0, 1, 0))
            pre = bias + p0_prev + p1
            o_ref[:, combo * HEAD_DIM:(combo + 1) * HEAD_DIM] = _dot((pre * jax.nn.sigmoid(pre)).astype(BF16), w2_ref[c])
            carry_ref[combo:combo + 1, :] = p0[nch - 1:nch, :]


def _compress(rows, page_table, w1, w2, pe):
    B, P = page_table.shape
    ppg = min(32, P)
    assert P % ppg == 0
    half = CMP_LEN * HEAD_DIM // 2
    w1cat = jnp.concatenate([w1[:, :half], w1[:, half:]], axis=2).astype(BF16)
    pe2 = pe.reshape(2, 2, half)
    pe2 = jnp.concatenate([pe2, jnp.zeros((2, 6, half), F32)], axis=1)
    nch = ppg * CHUNKS_PER_PAGE
    gs = pltpu.PrefetchScalarGridSpec(
        num_scalar_prefetch=1, grid=(B, P),
        in_specs=[pl.BlockSpec((None, PAGE * ROWS_PER_KV, HEAD_DIM), lambda b, p, pt: (pt[b, p], 0, 0)),
                  pl.BlockSpec((2, half, 2 * HEAD_DIM), lambda b, p, pt: (0, 0, 0)),
                  pl.BlockSpec((2, HEAD_DIM, HEAD_DIM), lambda b, p, pt: (0, 0, 0)),
                  pl.BlockSpec((2, 8, half), lambda b, p, pt: (0, 0, 0))],
        out_specs=pl.BlockSpec((None, nch, ROWS_PER_KV * HEAD_DIM), lambda b, p, pt: (b, p // ppg, 0)),
        scratch_shapes=[pltpu.VMEM((ROWS_PER_KV, nch, CMP_STRIDE * HEAD_DIM), F32),
                        pltpu.VMEM((ROWS_PER_KV, HEAD_DIM), F32)])
    return pl.pallas_call(
        functools.partial(_compress_kernel, ppg=ppg), grid_spec=gs,
        out_shape=jax.ShapeDtypeStruct((B, P * CHUNKS_PER_PAGE, ROWS_PER_KV * HEAD_DIM), F32), name="nsa_compress",
        compiler_params=_cparams("parallel", "arbitrary"))(page_table, rows, w1cat, w2.astype(BF16), pe2)


def _overlap_shifted(nm, nb, width):
    nc = nm - 1
    m = np.zeros((nm, width), np.float32)
    j = np.arange(nb)
    for a in range(SLC_LEN // CMP_STRIDE):
        for b in range(CMP_LEN // CMP_STRIDE):
            i = (SLC_LEN // CMP_STRIDE) * j + a - b
            ok = (i >= 0) & (i < nc)
            np.add.at(m, (i[ok] + 1, j[ok]), 1.0)
    return jnp.asarray(m, BF16)


def _slc_scores(imp, qpos_blk, lane):
    lag = qpos_blk - lane
    valid = lag >= 0
    forced = (lane == 0) | (valid & (lag < SLC_LOCAL))
    return jnp.where(valid, jnp.where(forced, POS_INF, imp), NEG_INF)


def _nsa_prompt_kernel(q_ref, kc_ref, vc_ref, ks_ref, vs_ref, kw_ref, vw_ref, g_ref, ovl_ref, e_ref, o_ref,
                       *, tq, tk, nm):
    G = NSA_GROUP
    hkv = pl.program_id(1)
    qi = pl.program_id(2)
    q0 = qi * tq
    qs = jnp.concatenate([q_ref[:, g * HEAD_DIM:(g + 1) * HEAD_DIM] for g in range(G)], axis=0).astype(BF16)

    def rep(x):
        return jnp.concatenate([x] * G, axis=0)

    s = _nt(qs, kc_ref[...].astype(BF16)) * SCALE
    m_i = _iota((tq, nm), 1)
    c_ok = jnp.where((m_i >= 1) & ((m_i - 1) * CMP_STRIDE + (CMP_LEN - 1) <= q0 + _iota((tq, nm), 0)), 1.0, 0.0)
    p_cmp = _masked_softmax(s, rep(c_ok) > 0.5).astype(BF16)
    o_cmp = _dot(p_cmp, vc_ref[...].astype(BF16))
    pov = _dot(p_cmp, ovl_ref[...])
    imp = pov[0:tq]
    for g in range(1, G):
        imp = imp + pov[g * tq:(g + 1) * tq]

    lane = _iota((tq, LANES), 1)
    qrow = q0 + _iota((tq, LANES), 0)
    sel, _ = _top_k_mask(_slc_scores(imp, qrow // SLC_LEN, lane), SLC_TOP)
    sel_b = sel.astype(BF16)
    qpos = q0 + _iota((tq, tk), 0)
    kio = _iota((tq, tk), 1)

    def slc_body(j, carry):
        k0 = pl.multiple_of(j * tk, tk)
        k = ks_ref[pl.ds(k0, tk), :].astype(BF16)
        v = vs_ref[pl.ds(k0, tk), :].astype(BF16)
        sel_e = _dot(sel_b, e_ref[:, pl.ds(k0, tk)])
        ok = jnp.where((k0 + kio) <= qpos, sel_e, 0.0)
        return _online_update(_nt(qs, k) * SCALE, rep(ok) > 0.5, *carry, v)

    _, l, acc = lax.fori_loop(0, (q0 + tq + tk - 1) // tk, slc_body, _flash_init(G * tq))
    o_slc = _flash_finish(l, acc)

    qpos_w = q0 + _iota((tq, tq), 0)
    kio_w = _iota((tq, tq), 1)

    def win_body(j, carry):
        k0 = pl.multiple_of(j * tq, tq)
        k = kw_ref[pl.ds(k0, tq), :].astype(BF16)
        v = vw_ref[pl.ds(k0, tq), :].astype(BF16)
        d = qpos_w - (k0 + kio_w)
        ok = jnp.where((d >= 0) & (d < WINDOW), 1.0, 0.0)
        return _online_update(_nt(qs, k) * SCALE, rep(ok) > 0.5, *carry, v)

    _, l, acc = lax.fori_loop(jnp.maximum(qi - WINDOW // tq, 0), qi + 1, win_body, _flash_init(G * tq))
    o_win = _flash_finish(l, acc)

    gates = g_ref[...]
    for g in range(G):
        col = (hkv * G + g) * N_BRANCH
        r = slice(g * tq, (g + 1) * tq)
        o = (o_cmp[r] * _lane_col(gates, col) + o_slc[r] * _lane_col(gates, col + 1)
             + o_win[r] * _lane_col(gates, col + 2))
        o_ref[:, g * HEAD_DIM:(g + 1) * HEAD_DIM] = o.astype(o_ref.dtype)


def _nsa_prompt_attn(z, comp, gates, B, T, *, tq=128, tk=256):
    assert T % tk == 0 and WINDOW % tq == 0 and T // SLC_LEN <= LANES
    nq = T // tq
    nm = comp.shape[1]
    G = NSA_GROUP
    qw = G * HEAD_DIM
    kvb = N_HEADS
    ovl = _overlap_shifted(nm, T // SLC_LEN, LANES)
    e = jnp.asarray((np.arange(T)[None, :] // SLC_LEN) == np.arange(LANES)[:, None], BF16)

    def kv_spec(off):
        return pl.BlockSpec((T, HEAD_DIM), lambda b, h, i: (b, kvb + off + h))

    return pl.pallas_call(
        functools.partial(_nsa_prompt_kernel, tq=tq, tk=tk, nm=nm), grid=(B, NSA_KV_HEADS, nq),
        in_specs=[pl.BlockSpec((tq, qw), lambda b, h, i: (b * nq + i, h)),
                  pl.BlockSpec((None, nm, HEAD_DIM), lambda b, h, i: (b, 0, h)),
                  pl.BlockSpec((None, nm, HEAD_DIM), lambda b, h, i: (b, 0, NSA_KV_HEADS + h)),
                  kv_spec(8), kv_spec(12), kv_spec(16), kv_spec(20),
                  pl.BlockSpec((tq, LANES), lambda b, h, i: (b * nq + i, 0)),
                  pl.BlockSpec((nm, LANES), lambda b, h, i: (0, 0)),
                  pl.BlockSpec((LANES, T), lambda b, h, i: (0, 0))],
        out_specs=pl.BlockSpec((tq, qw), lambda b, h, i: (b * nq + i, h)),
        out_shape=jax.ShapeDtypeStruct((B * T, N_HEADS * HEAD_DIM), BF16), name="nsa_prompt",
        compiler_params=_cparams("parallel", "parallel", "arbitrary"))(z, comp, comp, z, z, z, z, gates, ovl, e)


NSA_Q = N_HEADS * HEAD_DIM
NSA_KV = NSA_KV_HEADS * HEAD_DIM
NSA_MAIN = NSA_Q + 2 * N_BRANCH * NSA_KV
HD = N_HEADS * HEAD_DIM


def _pad_cols(w, width=LANES):
    return jnp.pad(w, ((0, 0), (0, width - w.shape[1])))


def _nsa_rope_flags(tn=512):
    per = NSA_KV // tn
    flags = [1] * (NSA_Q // tn)
    for _ in range(N_BRANCH):
        flags += [1] * per + [0] * per
    return jnp.asarray(flags, I32)


def _nsa_project(x2d, g, w_in, pos_tables, tm):
    z = _norm_mm(x2d, g, w_in, NSA_MAIN, tm=tm, epi="rope", rope=(_nsa_rope_flags(),) + pos_tables)
    gates = _norm_mm(x2d, g, _pad_cols(w_in[:, NSA_MAIN:]), LANES, tm=tm, epi="sigmoid")
    return z, gates


def _nsa_prompt_mixer(x2d, g, w_in, pe, w1, w2, B, T, tm):
    z, gates = _nsa_project(x2d, g, w_in, _rope_tables(jnp.arange(T, dtype=I32)), tm)
    kv = [z[:, NSA_Q + 2 * NSA_KV * br:NSA_Q + 2 * NSA_KV * (br + 1)] for br in range(N_BRANCH)]
    rows = kv[0].reshape(B * T // PAGE, PAGE * ROWS_PER_KV, HEAD_DIM)
    pt = jnp.arange(B * T // PAGE, dtype=I32).reshape(B, T // PAGE)
    comp = _compress(rows, pt, w1, w2, pe)
    o = _nsa_prompt_attn(z, comp, gates, B, T)
    st = [a.reshape(B, T, 2, NSA_KV_HEADS, HEAD_DIM) for a in kv]
    return o, (st[0], st[1], st[2][:, T - min(WINDOW, T):])


def _moba_rope_flags(tn=512):
    return jnp.asarray([1] * (2 * HD // tn) + [0] * (HD // tn), I32)


def _moba_prompt_mixer(x2d, g, w_in, B, T, tm):
    z = _norm_mm(x2d, g, w_in, 3 * HD, tm=tm, epi="rope",
                 rope=(_moba_rope_flags(),) + _rope_tables(jnp.arange(T, dtype=I32)))
    o = _moba_prompt_attn(z, B, T)
    return o, (z[:, HD:].reshape(B, T, 2, N_HEADS, HEAD_DIM),)


def _fox_project(x2d, g, w_in, b_f, tm):
    z = _norm_mm(x2d, g, w_in, 3 * HD, tm=tm)
    logf = _norm_mm(x2d, g, _pad_cols(w_in[:, 3 * HD:]), LANES, tm=tm, epi="logsig",
                    bias=jnp.pad(b_f, (0, LANES - N_HEADS)))
    return z, logf


def _fox_prompt_mixer(x2d, g, w_in, b_f, B, T, tm):
    z, logf = _fox_project(x2d, g, w_in, b_f, tm)
    log_f = logf[:, :N_HEADS].reshape(B, T, N_HEADS)
    ct = _cumsum_rows(log_f.transpose(0, 2, 1))
    c = _pad_cols(ct.transpose(0, 2, 1).reshape(B * T, N_HEADS))
    o = _fox_prompt_attn(z, c, ct, B, T)
    return o, (z[:, HD:].reshape(B, T, 2, N_HEADS, HEAD_DIM), log_f)


def _ffn(x2d, g, wi, wo, tm):
    u = _swiglu_up(x2d, g, wi, tm=tm)
    return _mm_res(u, wo, x2d, 0.5, tm=tm, tn=512)


def _page_heads(x_ref, first, n, stride):
    return jnp.concatenate([x_ref[pl.ds(first + h, PAGE, stride=stride), :] for h in range(n)], axis=1)


def _diag_blocks(acc, nblk):
    rblk = _iota((acc.shape[0], HEAD_DIM), 0) % nblk
    out = jnp.zeros((acc.shape[0], HEAD_DIM), F32)
    for j in range(nblk):
        out = out + jnp.where(rblk == j, acc[:, j * HEAD_DIM:(j + 1) * HEAD_DIM], 0.0)
    return out


def _row_query(rows, mode):
    r = _iota((rows, LANES), 0)
    return (r % N_HEADS) // NSA_KV_HEADS if mode == "nsa" else r // N_HEADS


def _paged_attn_kernel(pt_ref, q_ref, x_ref, kn_ref, vn_ref, *rest, mode, nblk, n_pages):
    del pt_ref
    if mode == "fox":
        cq_ref, ct_ref, ctn_ref, o_ref, m_ref, l_ref, acc_ref = rest
    else:
        sel_ref, o_ref, m_ref, l_ref, acc_ref = rest
    p = pl.program_id(1)
    R = q_ref.shape[0]
    stride = 2 * nblk

    @pl.when(p == 0)
    def _():
        m_ref[...] = jnp.full(m_ref.shape, NEG, F32)
        l_ref[...] = jnp.zeros(l_ref.shape, F32)
        acc_ref[...] = jnp.zeros(acc_ref.shape, F32)

    q = q_ref[...]
    lane = _iota((R, LANES), 1)

    def sel_cols(blk):
        if isinstance(blk, int):
            tile = sel_ref[:, (blk // LANES) * LANES:(blk // LANES + 1) * LANES]
        else:
            tile = sel_ref[:, pl.ds(pl.multiple_of((blk // LANES) * LANES, LANES), LANES)]
        return _lane_col(tile, blk % LANES), _lane_col(tile, blk % LANES + 1)

    def update(s, ok, v):
        m, l, acc = _online_update(s, ok, m_ref[...], l_ref[...], acc_ref[...], v)
        m_ref[...] = m
        l_ref[...] = l
        acc_ref[...] = acc

    k = _page_heads(x_ref, 0, nblk, stride).astype(BF16)
    v = _page_heads(x_ref, nblk, nblk, stride).astype(BF16)
    s = _nt(q, k) * SCALE
    if mode == "fox":
        s = s + cq_ref[...] - jnp.concatenate([ct_ref[...]] * (R // N_HEADS), axis=0)
        ok = lane >= 0
    elif mode == "moba":
        c0, c1 = sel_cols((p * PAGE // MOBA_BLOCK) // 2 * 2)
        ok = jnp.where((p * PAGE // MOBA_BLOCK) % 2 == 0, c0, c1) > 0.5
    else:
        c0, c1 = sel_cols(p * (PAGE // SLC_LEN))
        ok = jnp.where(lane < SLC_LEN, c0, c1) > 0.5
    update(s, ok, v)

    @pl.when(p == n_pages - 1)
    def _():
        kn = kn_ref[...].astype(BF16)
        vn = vn_ref[...].astype(BF16)
        sn = _nt(q, kn) * SCALE
        causal = lane <= _row_query(R, mode)
        if mode == "fox":
            sn = sn + cq_ref[...] - jnp.concatenate([ctn_ref[...]] * (R // N_HEADS), axis=0)
            okn = causal
        elif mode == "moba":
            okn = causal
        else:
            c0, _ = sel_cols(n_pages * (PAGE // SLC_LEN))
            okn = causal & (c0 > 0.5)
        update(sn, okn, vn)
        o_ref[...] = _diag_blocks(_flash_finish(l_ref[...], acc_ref[...]), nblk).astype(o_ref.dtype)


def _paged_attn(qbd, pool, page_table, k_new, v_new, extras, *, mode, out_dtype):
    B, R, C = qbd.shape
    P = page_table.shape[1]
    nblk = C // HEAD_DIM
    in_specs = [pl.BlockSpec((None, R, C), lambda b, p, pt: (b, 0, 0)),
                pl.BlockSpec((None, PAGE * 2 * nblk, HEAD_DIM), lambda b, p, pt: (pt[b, p], 0, 0)),
                pl.BlockSpec((None, PAGE, C), lambda b, p, pt: (b, 0, 0)),
                pl.BlockSpec((None, PAGE, C), lambda b, p, pt: (b, 0, 0))]
    if mode == "fox":
        cq, ct = extras
        in_specs += [pl.BlockSpec((None, R, LANES), lambda b, p, pt: (b, 0, 0)),
                     pl.BlockSpec((None, N_HEADS, LANES), lambda b, p, pt: (b, 0, p)),
                     pl.BlockSpec((None, N_HEADS, LANES), lambda b, p, pt: (b, 0, P))]
        args = (cq, ct, ct)
    else:
        (sel,) = extras
        in_specs += [pl.BlockSpec((None, R, sel.shape[2]), lambda b, p, pt: (b, 0, 0))]
        args = (sel,)
    gs = pltpu.PrefetchScalarGridSpec(
        num_scalar_prefetch=1, grid=(B, P), in_specs=in_specs,
        out_specs=pl.BlockSpec((None, R, HEAD_DIM), lambda b, p, pt: (b, 0, 0)),
        scratch_shapes=[pltpu.VMEM((R, 1), F32), pltpu.VMEM((R, 1), F32), pltpu.VMEM((R, C), F32)])
    return pl.pallas_call(
        functools.partial(_paged_attn_kernel, mode=mode, nblk=nblk, n_pages=P), grid_spec=gs,
        out_shape=jax.ShapeDtypeStruct((B, R, HEAD_DIM), out_dtype), name="paged_attn_" + mode,
        compiler_params=_cparams("parallel", "arbitrary"))(page_table, qbd, pool, k_new, v_new, *args)


def _cumsum_paged_kernel(pt_ref, x_ref, xn_ref, o_ref, carry_ref, *, n_pages):
    del pt_ref
    p = pl.program_id(1)

    @pl.when(p == 0)
    def _():
        carry_ref[...] = jnp.zeros(carry_ref.shape, F32)

    x = jnp.where(p == n_pages, xn_ref[...], x_ref[...])
    cs = _tile_cumsum(x) + carry_ref[...]
    o_ref[...] = cs
    carry_ref[...] = jnp.broadcast_to(cs[:, LANES - 1:LANES], carry_ref.shape)


def _cumsum_paged(pool_t, page_table, new_t):
    B, P = page_table.shape
    H = pool_t.shape[1]
    gs = pltpu.PrefetchScalarGridSpec(
        num_scalar_prefetch=1, grid=(B, P + 1),
        in_specs=[pl.BlockSpec((None, H, PAGE), lambda b, p, pt: (pt[b, jnp.minimum(p, P - 1)], 0, 0)),
                  pl.BlockSpec((None, H, PAGE), lambda b, p, pt: (b, 0, 0))],
        out_specs=pl.BlockSpec((None, H, PAGE), lambda b, p, pt: (b, 0, p)),
        scratch_shapes=[pltpu.VMEM((H, LANES), F32)])
    return pl.pallas_call(
        functools.partial(_cumsum_paged_kernel, n_pages=P), grid_spec=gs,
        out_shape=jax.ShapeDtypeStruct((B, H, (P + 1) * PAGE), F32), name="cumsum_paged",
        compiler_params=_cparams("parallel", "arbitrary"))(page_table, pool_t, new_t)


def _kmeans_kernel(pt_ref, x_ref, o_ref, *, ppb):
    del pt_ref
    part = jnp.sum(x_ref[...], axis=0)

    @pl.when(pl.program_id(1) % ppb == 0)
    def _():
        o_ref[...] = part

    @pl.when(pl.program_id(1) % ppb != 0)
    def _():
        o_ref[...] = o_ref[...] + part

    @pl.when(pl.program_id(1) % ppb == ppb - 1)
    def _():
        o_ref[...] = o_ref[...] * (1.0 / MOBA_BLOCK)


def _moba_kmeans(pool5, page_table):
    B, P = page_table.shape
    ppb = MOBA_BLOCK // PAGE
    gs = pltpu.PrefetchScalarGridSpec(
        num_scalar_prefetch=1, grid=(B, P),
        in_specs=[pl.BlockSpec((None, PAGE, None, N_HEADS, HEAD_DIM), lambda b, p, pt: (pt[b, p], 0, 0, 0, 0))],
        out_specs=pl.BlockSpec((None, None, N_HEADS, HEAD_DIM), lambda b, p, pt: (b, p // ppb, 0, 0)))
    return pl.pallas_call(
        functools.partial(_kmeans_kernel, ppb=ppb), grid_spec=gs,
        out_shape=jax.ShapeDtypeStruct((B, P // ppb, N_HEADS, HEAD_DIM), F32), name="moba_kmeans",
        compiler_params=_cparams("parallel", "arbitrary"))(page_table, pool5)


def _moba_select_kernel(q_ref, km_ref, o_ref, *, n_past_blocks):
    gate = _nt(q_ref[...], km_ref[...].astype(BF16))
    lane = _iota(gate.shape, 1)
    gate = jnp.where(lane < n_past_blocks, gate, NEG_INF)
    _, found = _top_k_mask(gate, MOBA_TOP)
    sel = jnp.zeros(gate.shape, F32)
    for mx, pick in found:
        sel = jnp.where(pick & (mx > NEG_INF), 1.0, sel)
    o_ref[...] = sel


def _moba_select(qbd, kmeans, n_past_blocks):
    B, R, C = qbd.shape
    return pl.pallas_call(
        functools.partial(_moba_select_kernel, n_past_blocks=n_past_blocks), grid=(B,),
        in_specs=[pl.BlockSpec((None, R, C), lambda b: (b, 0, 0)), pl.BlockSpec((None, LANES, C), lambda b: (b, 0, 0))],
        out_specs=pl.BlockSpec((None, R, LANES), lambda b: (b, 0, 0)),
        out_shape=jax.ShapeDtypeStruct((B, R, LANES), F32), name="moba_select",
        compiler_params=_cparams("parallel"))(qbd, kmeans)


def _nsa_cmp_sample_kernel(q_ref, kc_ref, vc_ref, ovl_ref, o_ref, sel_ref, *, past):
    R = q_ref.shape[0]
    nm = kc_ref.shape[0]
    G = NSA_GROUP
    s = _nt(q_ref[...], kc_ref[...].astype(BF16)) * SCALE
    m_i = _iota((R, nm), 1)
    qpos = past + (_iota((R, nm), 0) % N_HEADS) // NSA_KV_HEADS
    ok = (m_i >= 1) & ((m_i - 1) * CMP_STRIDE + (CMP_LEN - 1) <= qpos)
    p_cmp = _masked_softmax(s, ok).astype(BF16)
    o_ref[...] = _diag_blocks(_dot(p_cmp, vc_ref[...].astype(BF16)), NSA_KV_HEADS)
    pov = _dot(p_cmp, ovl_ref[...])
    rg = R // G
    imp = pov[0:rg]
    for g in range(1, G):
        imp = imp + pov[g * rg:(g + 1) * rg]
    lane = _iota(imp.shape, 1)
    qblk = (past + _iota(imp.shape, 0) // NSA_KV_HEADS) // SLC_LEN
    sel, _ = _top_k_mask(_slc_scores(imp, qblk, lane), SLC_TOP)
    sel_ref[...] = jnp.concatenate([sel] * G, axis=0)


def _nsa_cmp_sample(qbd, comp, past):
    B, R, C = qbd.shape
    nm = comp.shape[1]
    nb = -(-(past + R // N_HEADS) // SLC_LEN)
    width = -(-nb // LANES) * LANES
    ovl = _overlap_shifted(nm, nb, width)
    return pl.pallas_call(
        functools.partial(_nsa_cmp_sample_kernel, past=past), grid=(B,),
        in_specs=[pl.BlockSpec((None, R, C), lambda b: (b, 0, 0)),
                  pl.BlockSpec((None, nm, C), lambda b: (b, 0, 0)),
                  pl.BlockSpec((None, nm, C), lambda b: (b, 0, 1)),
                  pl.BlockSpec((nm, width), lambda b: (0, 0))],
        out_specs=[pl.BlockSpec((None, R, HEAD_DIM), lambda b: (b, 0, 0)),
                   pl.BlockSpec((None, R, width), lambda b: (b, 0, 0))],
        out_shape=[jax.ShapeDtypeStruct((B, R, HEAD_DIM), F32), jax.ShapeDtypeStruct((B, R, width), F32)],
        name="nsa_cmp_sample", compiler_params=_cparams("parallel"))(qbd, comp, comp, ovl)


def _nsa_win_sample_kernel(q_ref, kw_ref, vw_ref, oc_ref, os_ref, g_ref, o_ref, *, past, wb):
    R = q_ref.shape[0]
    nk = kw_ref.shape[0]
    s = _nt(q_ref[...], kw_ref[...].astype(BF16)) * SCALE
    j = _iota((R, nk), 1)
    qpos = past + (_iota((R, nk), 0) % N_HEADS) // NSA_KV_HEADS
    wpos = past - wb + j
    d = qpos - wpos
    ok = (d >= 0) & (d < WINDOW) & (wpos >= 0)
    p = _masked_softmax(s, ok).astype(BF16)
    o_win = _diag_blocks(_dot(p, vw_ref[...].astype(BF16)), NSA_KV_HEADS)
    g = g_ref[...]
    o = oc_ref[...] * g[:, 0:1] + os_ref[...] * g[:, 1:2] + o_win * g[:, 2:3]
    o_ref[...] = o.astype(o_ref.dtype)


def _nsa_win_sample(qbd, kw, o_cmp, o_slc, gates, past, wb):
    B, R, C = qbd.shape
    nk = kw.shape[1]
    row = pl.BlockSpec((None, R, HEAD_DIM), lambda b: (b, 0, 0))
    return pl.pallas_call(
        functools.partial(_nsa_win_sample_kernel, past=past, wb=wb), grid=(B,),
        in_specs=[pl.BlockSpec((None, R, C), lambda b: (b, 0, 0)),
                  pl.BlockSpec((None, nk, C), lambda b: (b, 0, 0)),
                  pl.BlockSpec((None, nk, C), lambda b: (b, 0, 1)),
                  row, row, row],
        out_specs=row, out_shape=jax.ShapeDtypeStruct((B, R, HEAD_DIM), BF16), name="nsa_win_sample",
        compiler_params=_cparams("parallel"))(qbd, kw, kw, o_cmp, o_slc, gates)


def _qbd_heads(q):
    B, Q, H, dh = q.shape
    x = q[:, :, :, None, :] * jnp.eye(H, dtype=q.dtype)[None, None, :, :, None]
    return x.reshape(B, Q * H, H * dh).astype(BF16)


def _qbd_groups(q):
    B, Q, H, dh = q.shape
    x = q.reshape(B, Q, NSA_KV_HEADS, NSA_GROUP, dh).transpose(0, 3, 1, 2, 4)
    x = x[:, :, :, :, None, :] * jnp.eye(NSA_KV_HEADS, dtype=q.dtype)[None, None, None, :, :, None]
    return x.reshape(B, NSA_GROUP * Q * NSA_KV_HEADS, NSA_KV_HEADS * dh).astype(BF16)


def _pad_rows(x, rows):
    return jnp.pad(x, ((0, 0), (0, rows - x.shape[1]), (0, 0)))


def _sample_tables(B, Q, past):
    return _rope_tables(jnp.tile(past + jnp.arange(Q, dtype=I32), B))


def _nsa_sample_mixer(x2d, g, w_in, pe, w1, w2, pool_cmp, pool_slc, win_buf, page_table, layer, B, Q):
    P = page_table.shape[1]
    past = P * PAGE
    assert past % SLC_LEN == 0 and Q * N_HEADS % 8 == 0
    n_pool = pool_cmp.shape[1]
    pt = page_table + layer * n_pool
    z, gates = _nsa_project(x2d, g, w_in, _sample_tables(B, Q, past), B * Q)
    kv = [z[:, NSA_Q + 2 * NSA_KV * br:NSA_Q + 2 * NSA_KV * (br + 1)].reshape(B, Q, 2 * NSA_KV) for br in range(N_BRANCH)]
    assert (past + Q - CMP_LEN) // CMP_STRIDE + 1 == past // CMP_STRIDE - 1
    comp = _compress(pool_cmp.reshape(-1, PAGE * ROWS_PER_KV, HEAD_DIM), pt, w1, w2, pe)
    qbd = _qbd_groups(z[:, :NSA_Q].reshape(B, Q, N_HEADS, HEAD_DIM))
    o_cmp, sel = _nsa_cmp_sample(qbd, comp, past)
    o_slc = _paged_attn(qbd, pool_slc.reshape(-1, PAGE * ROWS_PER_KV, HEAD_DIM), pt,
                        _pad_rows(kv[1][..., :NSA_KV], PAGE), _pad_rows(kv[1][..., NSA_KV:], PAGE), (sel,),
                        mode="nsa", out_dtype=F32)
    wb = win_buf.shape[2]
    kw = jnp.concatenate([win_buf[layer].reshape(B, wb, 2 * NSA_KV), kv[2]], axis=1)
    g_rows = gates[:, :N_HEADS * N_BRANCH].reshape(B, Q, NSA_KV_HEADS, NSA_GROUP, N_BRANCH)
    g_rows = jnp.pad(g_rows.transpose(0, 3, 1, 2, 4).reshape(B, Q * N_HEADS, N_BRANCH), ((0, 0), (0, 0), (0, LANES - N_BRANCH)))
    o = _nsa_win_sample(qbd, _pad_rows(kw, -(-(wb + Q) // LANES) * LANES), o_cmp, o_slc, g_rows, past, wb)
    o = o.reshape(B, NSA_GROUP, Q, NSA_KV_HEADS, HEAD_DIM).transpose(0, 2, 3, 1, 4).reshape(B * Q, HD)
    shp = (B, -1, 2, NSA_KV_HEADS, HEAD_DIM)
    return o, (kv[0].reshape(shp), kv[1].reshape(shp), kw[:, Q:].reshape(shp))


def _moba_sample_mixer(x2d, g, w_in, pool, page_table, layer, B, Q):
    P = page_table.shape[1]
    past = P * PAGE
    assert past % MOBA_BLOCK == 0 and Q <= MOBA_BLOCK and past // MOBA_BLOCK <= LANES
    n_pool = pool.shape[1]
    pt = page_table + layer * n_pool
    z = _norm_mm(x2d, g, w_in, 3 * HD, tm=B * Q, epi="rope", rope=(_moba_rope_flags(),) + _sample_tables(B, Q, past))
    qbd = _qbd_heads(z[:, :HD].reshape(B, Q, N_HEADS, HEAD_DIM))
    nbp = past // MOBA_BLOCK
    km = _moba_kmeans(pool.reshape(-1, PAGE, 2, N_HEADS, HEAD_DIM), pt).reshape(B, nbp, HD)
    sel = _moba_select(qbd, _pad_rows(km, LANES), nbp)
    new = z[:, HD:].reshape(B, Q, 2 * HD)
    o = _paged_attn(qbd, pool.reshape(-1, PAGE * 2 * N_HEADS, HEAD_DIM), pt,
                    _pad_rows(new[..., :HD], PAGE), _pad_rows(new[..., HD:], PAGE), (sel,), mode="moba", out_dtype=BF16)
    return o.reshape(B * Q, HD), (new.reshape(B, Q, 2, N_HEADS, HEAD_DIM),)


def _fox_sample_mixer(x2d, g, w_in, b_f, pool, pool_logf, page_table, layer, B, Q):
    P = page_table.shape[1]
    n_pool = pool.shape[1]
    pt = page_table + layer * n_pool
    z, logf = _fox_project(x2d, g, w_in, b_f, B * Q)
    log_f = logf[:, :N_HEADS].reshape(B, Q, N_HEADS)
    pool_t = pool_logf.astype(F32).transpose(0, 1, 3, 2).reshape(-1, N_HEADS, PAGE)
    new_t = jnp.pad(log_f.transpose(0, 2, 1), ((0, 0), (0, 0), (0, PAGE - Q)))
    ct = _cumsum_paged(pool_t, pt, new_t)
    cq = ct[:, :, P * PAGE:P * PAGE + Q].transpose(0, 2, 1).reshape(B, Q * N_HEADS, 1)
    qbd = _qbd_heads(z[:, :HD].reshape(B, Q, N_HEADS, HEAD_DIM))
    new = z[:, HD:].reshape(B, Q, 2 * HD)
    o = _paged_attn(qbd, pool.reshape(-1, PAGE * 2 * N_HEADS, HEAD_DIM), pt,
                    _pad_rows(new[..., :HD], PAGE), _pad_rows(new[..., HD:], PAGE),
                    (jnp.broadcast_to(cq, (B, Q * N_HEADS, LANES)), ct), mode="fox", out_dtype=BF16)
    return o.reshape(B * Q, HD), (new.reshape(B, Q, 2, N_HEADS, HEAD_DIM), log_f)


PROMPT_TM = 512


def kernel(x_prompt, x_sample, cache_nsa_cmp, cache_nsa_slc, cache_nsa_win, cache_moba_kv, cache_fox_kv,
           cache_fox_logf, page_table, norms, ffn_wi, ffn_wo, final_norm, nsa_w_in, nsa_cmp_pe, nsa_cmp_w1,
           nsa_cmp_w2, nsa_w_out, moba_w_in, moba_w_out, fox_w_in, fox_b_f, fox_w_out):
    B, T, D = x_prompt.shape
    Bs, Q, _ = x_sample.shape
    depth = norms.shape[0]
    xp = x_prompt.reshape(B * T, D)
    xs = x_sample.reshape(Bs * Q, D)
    tp = min(PROMPT_TM, B * T)
    ts = Bs * Q
    st_p = {0: [], 1: [], 2: []}
    st_s = {0: [], 1: [], 2: []}
    for i in range(depth):
        kind, j = i % 3, i // 3
        xp = _ffn(xp, norms[i, 0], ffn_wi[i, 0], ffn_wo[i, 0], tp)
        xs = _ffn(xs, norms[i, 0], ffn_wi[i, 0], ffn_wo[i, 0], ts)
        g = norms[i, 1]
        if kind == 0:
            op, sp = _nsa_prompt_mixer(xp, g, nsa_w_in[j], nsa_cmp_pe[j], nsa_cmp_w1[j], nsa_cmp_w2[j], B, T, tp)
            os_, ss = _nsa_sample_mixer(xs, g, nsa_w_in[j], nsa_cmp_pe[j], nsa_cmp_w1[j], nsa_cmp_w2[j],
                                        cache_nsa_cmp, cache_nsa_slc, cache_nsa_win, page_table, j, Bs, Q)
            w_out = nsa_w_out[j]
        elif kind == 1:
            op, sp = _moba_prompt_mixer(xp, g, moba_w_in[j], B, T, tp)
            os_, ss = _moba_sample_mixer(xs, g, moba_w_in[j], cache_moba_kv, page_table, j, Bs, Q)
            w_out = moba_w_out[j]
        else:
            op, sp = _fox_prompt_mixer(xp, g, fox_w_in[j], fox_b_f[j], B, T, tp)
            os_, ss = _fox_sample_mixer(xs, g, fox_w_in[j], fox_b_f[j], cache_fox_kv, cache_fox_logf, page_table, j, Bs, Q)
            w_out = fox_w_out[j]
        st_p[kind].append(sp)
        st_s[kind].append(ss)
        xp = _mm_res(op, w_out, xp, 1.0, tm=tp, tn=512)
        xs = _mm_res(os_, w_out, xs, 1.0, tm=ts, tn=512)
        xp = _ffn(xp, norms[i, 2], ffn_wi[i, 1], ffn_wo[i, 1], tp)
        xs = _ffn(xs, norms[i, 2], ffn_wi[i, 1], ffn_wo[i, 1], ts)
    y_prompt = _rmsnorm(xp, final_norm, tm=tp).reshape(B, T, D)
    y_sample = _rmsnorm(xs, final_norm, tm=ts).reshape(Bs, Q, D)

    def stack(states, k):
        return jnp.stack([s[k] for s in states])

    return (y_prompt, y_sample,
            stack(st_p[0], 0), stack(st_p[0], 1), stack(st_p[0], 2),
            stack(st_s[0], 0), stack(st_s[0], 1), stack(st_s[0], 2),
            stack(st_p[1], 0), stack(st_s[1], 0),
            stack(st_p[2], 0), stack(st_p[2], 1), stack(st_s[2], 0), stack(st_s[2], 1))
```

```python
import functools

import numpy as np
import jax
import jax.numpy as jnp
from jax import lax
from jax.experimental import pallas as pl
from jax.experimental.pallas import tpu as pltpu

F32 = jnp.float32
BF16 = jnp.bfloat16
I32 = jnp.int32

LANES = 128
VMEM_LIMIT_BYTES = 56 << 20

N_HEADS = 16
HEAD_DIM = 128
ROT_DIM = HEAD_DIM // 4
ROPE_THETA = 500000.0
NORM_EPS = 1e-6
PAGE = 128
NSA_KV_HEADS = 4
NSA_GROUP = N_HEADS // NSA_KV_HEADS
CMP_LEN = 32
CMP_STRIDE = 16
SLC_LEN = 64
SLC_TOP = 16
SLC_LOCAL = 2
WINDOW = 512
N_BRANCH = 3
MOBA_BLOCK = 256
MOBA_TOP = 3
SCALE = HEAD_DIM ** -0.5
LOG2E = 1.4426950408889634
QK_LOG2 = SCALE * LOG2E
ROW_CHUNK = 64
NEG = -1e30
NEG_INF = float("-inf")
POS_INF = float("inf")


def _cparams(*sem):
    return pltpu.CompilerParams(dimension_semantics=sem, vmem_limit_bytes=VMEM_LIMIT_BYTES)


def _nt(a, b):
    return lax.dot_general(a, b, (((1,), (1,)), ((), ())), preferred_element_type=F32)


def _dot(a, b):
    return jnp.dot(a, b, preferred_element_type=F32)


def _iota(shape, dim):
    return lax.broadcasted_iota(I32, shape, dim)


def _lane_col(x, idx):
    return jnp.sum(jnp.where(_iota(x.shape, 1) == idx, x, 0.0), axis=-1, keepdims=True)


def _rms_to_bf16(x_ref, g_ref):
    x = x_ref[...]
    var = jnp.mean(x * x, axis=-1, keepdims=True)
    return ((x * lax.rsqrt(var + NORM_EPS)) * g_ref[...]).astype(BF16)


def _log_sigmoid(x):
    return jnp.minimum(x, 0.0) - jnp.log1p(jnp.exp(-jnp.abs(x)))


def _norm_mm_kernel(*refs, epi, tn):
    if epi == "rope":
        flags_ref, x_ref, g_ref, w_ref, c_ref, s1_ref, s2_ref, o_ref, xn_ref = refs
    elif epi == "logsig":
        x_ref, g_ref, w_ref, b_ref, o_ref, xn_ref = refs
    else:
        x_ref, g_ref, w_ref, o_ref, xn_ref = refs
    j = pl.program_id(1)

    @pl.when(j == 0)
    def _():
        xn_ref[...] = _rms_to_bf16(x_ref, g_ref)

    z = _dot(xn_ref[...], w_ref[...].astype(BF16))
    if epi == "rope":
        @pl.when(flags_ref[j] == 1)
        def _():
            c, s1, s2 = c_ref[...], s1_ref[...], s2_ref[...]
            for hh in range(tn // HEAD_DIM):
                zs = z[:, hh * HEAD_DIM:(hh + 1) * HEAD_DIM]
                o_ref[:, hh * HEAD_DIM:(hh + 1) * HEAD_DIM] = (
                    zs * c + pltpu.roll(zs, HEAD_DIM - ROT_DIM // 2, 1) * s1 + pltpu.roll(zs, ROT_DIM // 2, 1) * s2)

        @pl.when(flags_ref[j] == 0)
        def _():
            o_ref[...] = z
    elif epi == "sigmoid":
        o_ref[...] = jax.nn.sigmoid(z)
    elif epi == "logsig":
        o_ref[...] = _log_sigmoid(z + b_ref[...])
    else:
        o_ref[...] = z


def _norm_mm(x, g, w, n_out, *, tm, tn=512, epi="none", rope=None, bias=None):
    M, D = x.shape
    tn = min(tn, n_out)
    assert M % tm == 0 and n_out % tn == 0
    grid = (M // tm, n_out // tn)
    g2 = g.reshape(1, D)
    kern = functools.partial(_norm_mm_kernel, epi=epi, tn=tn)
    scratch = [pltpu.VMEM((tm, D), BF16)]
    out_shape = jax.ShapeDtypeStruct((M, n_out), F32)
    if epi == "rope":
        flags, c, s1, s2 = rope
        nt = c.shape[0] // tm
        tab = pl.BlockSpec((tm, HEAD_DIM), lambda i, j, f: (i % nt, 0))
        gs = pltpu.PrefetchScalarGridSpec(
            num_scalar_prefetch=1, grid=grid,
            in_specs=[pl.BlockSpec((tm, D), lambda i, j, f: (i, 0)),
                      pl.BlockSpec((1, D), lambda i, j, f: (0, 0)),
                      pl.BlockSpec((D, tn), lambda i, j, f: (0, j)),
                      tab, tab, tab],
            out_specs=pl.BlockSpec((tm, tn), lambda i, j, f: (i, j)),
            scratch_shapes=scratch)
        return pl.pallas_call(kern, grid_spec=gs, out_shape=out_shape, name="norm_mm_rope",
                              compiler_params=_cparams("parallel", "arbitrary"))(flags, x, g2, w, c, s1, s2)
    in_specs = [pl.BlockSpec((tm, D), lambda i, j: (i, 0)),
                pl.BlockSpec((1, D), lambda i, j: (0, 0)),
                pl.BlockSpec((D, tn), lambda i, j: (0, j))]
    args = [x, g2, w]
    if epi == "logsig":
        in_specs.append(pl.BlockSpec((1, tn), lambda i, j: (0, j)))
        args.append(bias.reshape(1, n_out))
    return pl.pallas_call(kern, grid=grid, in_specs=in_specs,
                          out_specs=pl.BlockSpec((tm, tn), lambda i, j: (i, j)),
                          out_shape=out_shape, scratch_shapes=scratch, name="norm_mm_" + epi,
                          compiler_params=_cparams("parallel", "arbitrary"))(*args)


def _swiglu_up_kernel(x_ref, g_ref, wa_ref, wb_ref, o_ref, xn_ref):
    @pl.when(pl.program_id(1) == 0)
    def _():
        xn_ref[...] = _rms_to_bf16(x_ref, g_ref)

    xn = xn_ref[...]
    a = _dot(xn, wa_ref[...])
    b = _dot(xn, wb_ref[...])
    o_ref[...] = (a * jax.nn.sigmoid(a) * b).astype(o_ref.dtype)


def _swiglu_up(x, g, wa, wb, *, tm, tn=512):
    M, D = x.shape
    F = wa.shape[1]
    assert M % tm == 0
    w_spec = pl.BlockSpec((D, tn), lambda i, j: (0, j))
    return pl.pallas_call(
        _swiglu_up_kernel,
        grid=(M // tm, pl.cdiv(F, tn)),
        in_specs=[pl.BlockSpec((tm, D), lambda i, j: (i, 0)), pl.BlockSpec((1, D), lambda i, j: (0, 0)), w_spec, w_spec],
        out_specs=pl.BlockSpec((tm, tn), lambda i, j: (i, j)),
        out_shape=jax.ShapeDtypeStruct((M, F), BF16),
        scratch_shapes=[pltpu.VMEM((tm, D), BF16)],
        name="swiglu_up",
        compiler_params=_cparams("parallel", "arbitrary"))(x, g.reshape(1, D), wa, wb)


def _mm_res_kernel(u_ref, w_ref, r_ref, o_ref, *, scale):
    y = _dot(u_ref[...].astype(BF16), w_ref[...].astype(BF16))
    o_ref[...] = r_ref[...] + (y if scale == 1.0 else scale * y)


def _mm_res(u, w, res, scale, *, tm, tn):
    M, K = u.shape
    N = w.shape[1]
    assert M % tm == 0 and N % tn == 0
    return pl.pallas_call(
        functools.partial(_mm_res_kernel, scale=scale),
        grid=(M // tm, N // tn),
        in_specs=[pl.BlockSpec((tm, K), lambda i, j: (i, 0)),
                  pl.BlockSpec((K, tn), lambda i, j: (0, j)),
                  pl.BlockSpec((tm, tn), lambda i, j: (i, j))],
        out_specs=pl.BlockSpec((tm, tn), lambda i, j: (i, j)),
        out_shape=jax.ShapeDtypeStruct((M, N), F32), name="mm_res",
        compiler_params=_cparams("parallel", "arbitrary"))(u, w, res)


def _rmsnorm_kernel(x_ref, g_ref, o_ref):
    x = x_ref[...]
    var = jnp.mean(x * x, axis=-1, keepdims=True)
    o_ref[...] = (x * lax.rsqrt(var + NORM_EPS)) * g_ref[...]


def _rmsnorm(x, g, *, tm):
    M, D = x.shape
    return pl.pallas_call(
        _rmsnorm_kernel, grid=(M // tm,),
        in_specs=[pl.BlockSpec((tm, D), lambda i: (i, 0)), pl.BlockSpec((1, D), lambda i: (0, 0))],
        out_specs=pl.BlockSpec((tm, D), lambda i: (i, 0)),
        out_shape=jax.ShapeDtypeStruct((M, D), F32), name="rmsnorm",
        compiler_params=_cparams("parallel"))(x, g.reshape(1, D))


def _rope_tables(pos):
    half = ROT_DIM // 2
    inv = ROPE_THETA ** (-jnp.arange(half, dtype=F32) / half)
    ang = pos.astype(F32)[:, None] * inv[None, :]
    cos, sin = jnp.cos(ang), jnp.sin(ang)
    n = pos.shape[0]
    zeros = jnp.zeros((n, HEAD_DIM - ROT_DIM), F32)
    zh = jnp.zeros((n, half), F32)
    c = jnp.concatenate([cos, cos, jnp.ones((n, HEAD_DIM - ROT_DIM), F32)], axis=1)
    s1 = jnp.concatenate([-sin, zh, zeros], axis=1)
    s2 = jnp.concatenate([zh, sin, zeros], axis=1)
    return c, s1, s2


def _online_update(t, m, l, acc, v, roff=None):
    ms, ls, als, ps = [], [], [], []
    for r0 in range(0, t.shape[0], ROW_CHUNK):
        r = slice(r0, r0 + ROW_CHUNK)
        tmax = jnp.max(t[r], axis=-1, keepdims=True)
        if roff is not None:
            tmax = tmax + roff[r]
        m_new = jnp.maximum(m[r], tmax)
        alpha = jnp.exp2(m[r] - m_new)
        p = jnp.exp2(t[r] + ((roff[r] - m_new) if roff is not None else -m_new))
        ms.append(m_new)
        als.append(alpha)
        ls.append(alpha * l[r] + jnp.sum(p, axis=-1, keepdims=True))
        ps.append(p.astype(BF16))
    cat = lambda xs: xs[0] if len(xs) == 1 else jnp.concatenate(xs, axis=0)
    return cat(ms), cat(ls), cat(als) * acc + _dot(cat(ps), v)


def _online_update_t(t, m, l, acc, vt, coff=None):
    tmax = jnp.max(t, axis=0, keepdims=True)
    if coff is not None:
        tmax = tmax + coff
    m_new = jnp.maximum(m, tmax)
    alpha = jnp.exp2(m - m_new)
    p = jnp.exp2(t + ((coff - m_new) if coff is not None else -m_new))
    l_new = alpha * l + jnp.sum(p, axis=0, keepdims=True)
    return m_new, l_new, alpha * acc + _dot(vt, p.astype(BF16))


def _flash_init_t(n_q, width=HEAD_DIM):
    return (jnp.full((1, n_q), NEG, F32), jnp.zeros((1, n_q), F32), jnp.zeros((width, n_q), F32))


def _transpose_into(dst_ref, src_ref, n_rows, dtype, chunk=512):
    for r0 in range(0, n_rows, chunk):
        r1 = min(r0 + chunk, n_rows)
        dst_ref[:, r0:r1] = src_ref[r0:r1, :].T.astype(dtype)


def _mask_bias(ok):
    return jnp.where(ok, 0.0, NEG)


def _flash_init(rows, width=HEAD_DIM):
    return (jnp.full((rows, 1), NEG, F32), jnp.zeros((rows, 1), F32), jnp.zeros((rows, width), F32))


def _flash_finish(l, acc):
    return acc / jnp.where(l > 0.0, l, 1.0)


def _masked_softmax(s, ok, axis=-1):
    sm = jnp.where(ok, s, NEG)
    mx = jnp.max(sm, axis=axis, keepdims=True)
    e = jnp.where(ok, jnp.exp(sm - mx), 0.0)
    d = jnp.sum(e, axis=axis, keepdims=True)
    return e / jnp.where(d > 0.0, d, 1.0)


def _top_k_mask(score, k, axis=-1, finite_only=False):
    idx = _iota(score.shape, axis % score.ndim)
    n = score.shape[axis]
    sel = jnp.zeros(score.shape, F32)
    for _ in range(k):
        mx = jnp.max(score, axis=axis, keepdims=True)
        first = jnp.min(jnp.where(score == mx, idx, n), axis=axis, keepdims=True)
        pick = idx == first
        sel = jnp.where((pick & (mx > NEG_INF)) if finite_only else pick, 1.0, sel)
        score = jnp.where(pick, NEG_INF, score)
    return sel


def _tile_cumsum(x):
    lane = _iota(x.shape, 1)
    for s in (1, 2, 4, 8, 16, 32, 64):
        x = x + jnp.where(lane >= s, pltpu.roll(x, s, 1), 0.0)
    return x


def _cumsum_rows_kernel(x_ref, o_ref, *, n_tiles):
    carry = jnp.zeros((x_ref.shape[0], 1), F32)
    for j in range(n_tiles):
        cs = _tile_cumsum(x_ref[:, j * LANES:(j + 1) * LANES]) + carry
        o_ref[:, j * LANES:(j + 1) * LANES] = cs
        carry = cs[:, LANES - 1:LANES]


def _cumsum_rows(xt):
    B, H, T = xt.shape
    return pl.pallas_call(
        functools.partial(_cumsum_rows_kernel, n_tiles=T // LANES), grid=(B,),
        in_specs=[pl.BlockSpec((None, H, T), lambda b: (b, 0, 0))],
        out_specs=pl.BlockSpec((None, H, T), lambda b: (b, 0, 0)),
        out_shape=jax.ShapeDtypeStruct((B, H, T), F32), name="cumsum_rows",
        compiler_params=_cparams("parallel"))(xt)


def _fox_prompt_kernel(q_ref, k_ref, v_ref, c_ref, ct_ref, o_ref, vt_ref, ck_ref, *, tq, T):
    h = pl.program_id(1)
    qi = pl.program_id(2)

    @pl.when(qi == 0)
    def _():
        _transpose_into(vt_ref, v_ref, T, BF16)
        for r0 in range(0, T, 512):
            col = _lane_col(c_ref[r0:r0 + 512, :], h) * LOG2E
            ck_ref[r0:r0 + 512, :] = jnp.broadcast_to(col, (512, LANES))

    q = q_ref[...].astype(BF16)
    q0 = pl.multiple_of(qi * tq, tq)
    cq2 = ct_ref[pl.ds(h, 1), pl.ds(q0, tq)] * LOG2E

    def step(j, carry, diagonal):
        k0 = pl.multiple_of(j * tq, tq)
        k = k_ref[pl.ds(k0, tq), :].astype(BF16)
        t = _nt(k, q) * QK_LOG2 - jnp.concatenate([ck_ref[pl.ds(k0, tq), :]] * (tq // LANES), axis=1)
        if diagonal:
            t = t + _mask_bias(_iota((tq, tq), 0) <= _iota((tq, tq), 1))
        return _online_update_t(t, *carry, vt_ref[:, pl.ds(k0, tq)], coff=cq2)

    carry = lax.fori_loop(0, qi, lambda j, c: step(j, c, False), _flash_init_t(tq))
    _, l, acc = step(qi, carry, True)
    o_ref[...] = _flash_finish(l, acc).T.astype(o_ref.dtype)


def _fox_prompt_attn(z, c, ct, B, T, *, tq=1024):
    tq = min(tq, T)
    assert T % tq == 0 and T % 512 == 0
    nq = T // tq
    H = N_HEADS
    return pl.pallas_call(
        functools.partial(_fox_prompt_kernel, tq=tq, T=T), grid=(B, H, nq),
        in_specs=[pl.BlockSpec((tq, HEAD_DIM), lambda b, h, i: (b * nq + i, h)),
                  pl.BlockSpec((T, HEAD_DIM), lambda b, h, i: (b, H + h)),
                  pl.BlockSpec((T, HEAD_DIM), lambda b, h, i: (b, 2 * H + h)),
                  pl.BlockSpec((T, LANES), lambda b, h, i: (b, 0)),
                  pl.BlockSpec((None, H, T), lambda b, h, i: (b, 0, 0))],
        out_specs=pl.BlockSpec((tq, HEAD_DIM), lambda b, h, i: (b * nq + i, h)),
        out_shape=jax.ShapeDtypeStruct((B * T, H * HEAD_DIM), BF16),
        scratch_shapes=[pltpu.VMEM((HEAD_DIM, T), BF16), pltpu.VMEM((T, LANES), F32)], name="fox_prompt",
        compiler_params=_cparams("parallel", "parallel", "arbitrary"))(z, z, z, c, ct)


def _moba_prompt_kernel(q_ref, k_ref, v_ref, o_ref, km_ref, vt_ref, *, tq, T):
    qi = pl.program_id(2)
    q0 = qi * tq
    nb = T // MOBA_BLOCK
    nsub = tq // MOBA_BLOCK
    nbr = km_ref.shape[0]

    @pl.when(qi == 0)
    def _():
        km_ref[...] = jnp.zeros(km_ref.shape, F32)
        for n in range(nb):
            km_ref[n:n + 1, :] = jnp.mean(k_ref[n * MOBA_BLOCK:(n + 1) * MOBA_BLOCK, :], axis=0, keepdims=True)
        _transpose_into(vt_ref, v_ref, T, BF16)

    q = q_ref[...].astype(BF16)
    blk = _iota((nbr, tq), 0)
    own = (q0 + _iota((nbr, tq), 1)) // MOBA_BLOCK
    gate_t = jnp.where(blk < own, _nt(km_ref[...].astype(BF16), q), NEG_INF)
    sel_bias_t = _mask_bias(_top_k_mask(gate_t, MOBA_TOP, axis=0, finite_only=True) > 0.5)

    def tile(j, diagonal):
        k0 = pl.multiple_of(j * tq, tq)
        t = _nt(k_ref[pl.ds(k0, tq), :].astype(BF16), q) * QK_LOG2
        parts = []
        for r in range(nsub):
            b = jnp.sum(jnp.where(blk == j * nsub + r, sel_bias_t, 0.0), axis=0, keepdims=True)
            b = jnp.broadcast_to(b, (MOBA_BLOCK, tq))
            if diagonal:
                kk = r * MOBA_BLOCK + _iota((MOBA_BLOCK, tq), 0)
                qq = _iota((MOBA_BLOCK, tq), 1)
                b = jnp.where(qq // MOBA_BLOCK == r, _mask_bias(kk <= qq), b)
            parts.append(b)
        return t + jnp.concatenate(parts, axis=0), vt_ref[:, pl.ds(k0, tq)]

    t, vt = tile(qi, True)
    carry = _online_update_t(t, *_flash_init_t(tq), vt)

    def body(j, carry):
        t, vt = tile(j, False)
        return _online_update_t(t, *carry, vt)

    _, l, acc = lax.fori_loop(0, qi, body, carry)
    o_ref[...] = _flash_finish(l, acc).T.astype(o_ref.dtype)


def _moba_prompt_attn(z, B, T, *, tq=1024):
    tq = min(tq, T)
    assert T % tq == 0 and tq % MOBA_BLOCK == 0
    nq = T // tq
    H = N_HEADS
    return pl.pallas_call(
        functools.partial(_moba_prompt_kernel, tq=tq, T=T), grid=(B, H, nq),
        in_specs=[pl.BlockSpec((tq, HEAD_DIM), lambda b, h, i: (b * nq + i, h)),
                  pl.BlockSpec((T, HEAD_DIM), lambda b, h, i: (b, H + h)),
                  pl.BlockSpec((T, HEAD_DIM), lambda b, h, i: (b, 2 * H + h))],
        out_specs=pl.BlockSpec((tq, HEAD_DIM), lambda b, h, i: (b * nq + i, h)),
        out_shape=jax.ShapeDtypeStruct((B * T, H * HEAD_DIM), BF16),
        scratch_shapes=[pltpu.VMEM((-(-(T // MOBA_BLOCK) // 8) * 8, HEAD_DIM), F32),
                        pltpu.VMEM((HEAD_DIM, T), BF16)], name="moba_prompt",
        compiler_params=_cparams("parallel", "parallel", "arbitrary"))(z, z, z)


ROWS_PER_KV = 2 * NSA_KV_HEADS
CHUNKS_PER_PAGE = PAGE // CMP_STRIDE


def _compress_kernel(pt_ref, x_ref, w1_ref, w2_ref, pe_ref, o_ref, xcat_ref, carry_ref, *, ppg):
    del pt_ref
    p = pl.program_id(1)
    slot = p % ppg
    base = pl.multiple_of(slot * CHUNKS_PER_PAGE, CHUNKS_PER_PAGE)
    for combo in range(ROWS_PER_KV):
        for i in range(CMP_STRIDE):
            xcat_ref[combo, pl.ds(base, CHUNKS_PER_PAGE), i * HEAD_DIM:(i + 1) * HEAD_DIM] = (
                x_ref[pl.ds(i * ROWS_PER_KV + combo, CHUNKS_PER_PAGE, stride=CMP_STRIDE * ROWS_PER_KV), :])

    @pl.when(p == 0)
    def _():
        carry_ref[...] = jnp.zeros(carry_ref.shape, F32)

    @pl.when(slot == ppg - 1)
    def _():
        nch = ppg * CHUNKS_PER_PAGE
        first = _iota((nch, HEAD_DIM), 0) == 0
        for combo in range(ROWS_PER_KV):
            c = combo // NSA_KV_HEADS
            w1 = w1_ref[c]
            pb = _dot(pe_ref[c].astype(BF16), w1)
            bias = pb[0:1, :HEAD_DIM] + pb[1:2, HEAD_DIM:]
            part = _dot(xcat_ref[combo].astype(BF16), w1)
            p0, p1 = part[:, :HEAD_DIM], part[:, HEAD_DIM:]
            p0_prev = jnp.where(first, carry_ref[combo:combo + 1, :], pltpu.roll(p0, 1, 0))
            pre = bias + p0_prev + p1
            o_ref[:, combo * HEAD_DIM:(combo + 1) * HEAD_DIM] = _dot((pre * jax.nn.sigmoid(pre)).astype(BF16), w2_ref[c])
            carry_ref[combo:combo + 1, :] = p0[nch - 1:nch, :]


def _compress(rows, page_table, w1, w2, pe):
    B, P = page_table.shape
    ppg = min(32, P)
    assert P % ppg == 0
    half = CMP_LEN * HEAD_DIM // 2
    w1cat = jnp.concatenate([w1[:, :half], w1[:, half:]], axis=2).astype(BF16)
    pe2 = pe.reshape(2, 2, half)
    pe2 = jnp.concatenate([pe2, jnp.zeros((2, 6, half), F32)], axis=1)
    nch = ppg * CHUNKS_PER_PAGE
    gs = pltpu.PrefetchScalarGridSpec(
        num_scalar_prefetch=1, grid=(B, P),
        in_specs=[pl.BlockSpec((None, PAGE * ROWS_PER_KV, HEAD_DIM), lambda b, p, pt: (pt[b, p], 0, 0)),
                  pl.BlockSpec((2, half, 2 * HEAD_DIM), lambda b, p, pt: (0, 0, 0)),
                  pl.BlockSpec((2, HEAD_DIM, HEAD_DIM), lambda b, p, pt: (0, 0, 0)),
                  pl.BlockSpec((2, 8, half), lambda b, p, pt: (0, 0, 0))],
        out_specs=pl.BlockSpec((None, nch, ROWS_PER_KV * HEAD_DIM), lambda b, p, pt: (b, p // ppg, 0)),
        scratch_shapes=[pltpu.VMEM((ROWS_PER_KV, nch, CMP_STRIDE * HEAD_DIM), F32),
                        pltpu.VMEM((ROWS_PER_KV, HEAD_DIM), F32)])
    return pl.pallas_call(
        functools.partial(_compress_kernel, ppg=ppg), grid_spec=gs,
        out_shape=jax.ShapeDtypeStruct((B, P * CHUNKS_PER_PAGE, ROWS_PER_KV * HEAD_DIM), F32), name="nsa_compress",
        compiler_params=_cparams("parallel", "arbitrary"))(page_table, rows, w1cat, w2.astype(BF16), pe2)


def _overlap_shifted(nm, nb, width):
    nc = nm - 1
    m = np.zeros((nm, width), np.float32)
    j = np.arange(nb)
    for a in range(SLC_LEN // CMP_STRIDE):
        for b in range(CMP_LEN // CMP_STRIDE):
            i = (SLC_LEN // CMP_STRIDE) * j + a - b
            ok = (i >= 0) & (i < nc)
            np.add.at(m, (i[ok] + 1, j[ok]), 1.0)
    return jnp.asarray(m, BF16)


def _slc_scores(imp, qpos_blk, lane):
    lag = qpos_blk - lane
    valid = lag >= 0
    forced = (lane == 0) | (valid & (lag < SLC_LOCAL))
    return jnp.where(valid, jnp.where(forced, POS_INF, imp), NEG_INF)


def _nsa_prompt_kernel(q_ref, kc_ref, vc_ref, ks_ref, vs_ref, kw_ref, vw_ref, gt_ref, ovl_ref, et_ref, o_ref,
                       vst_ref, vwt_ref, *, tq, tk, nm, T):
    G = NSA_GROUP
    hkv = pl.program_id(1)
    qi = pl.program_id(2)
    q0 = pl.multiple_of(qi * tq, tq)

    @pl.when(qi == 0)
    def _():
        _transpose_into(vst_ref, vs_ref, T, BF16)
        _transpose_into(vwt_ref, vw_ref, T, BF16)

    qs = jnp.concatenate([q_ref[:, g * HEAD_DIM:(g + 1) * HEAD_DIM] for g in range(G)], axis=0).astype(BF16)

    def rep(x):
        return jnp.concatenate([x] * G, axis=1)

    s = _nt(kc_ref[...].astype(BF16), qs) * SCALE
    m_i = _iota((nm, tq), 0)
    c_ok = jnp.where((m_i >= 1) & ((m_i - 1) * CMP_STRIDE + (CMP_LEN - 1) <= q0 + _iota((nm, tq), 1)), 1.0, 0.0)
    p_cmp = _masked_softmax(s, rep(c_ok) > 0.5, axis=0).astype(BF16)
    o_cmp = _dot(vc_ref[...].T.astype(BF16), p_cmp)
    pov = _dot(ovl_ref[...], p_cmp)
    imp = pov[:, 0:tq]
    for g in range(1, G):
        imp = imp + pov[:, g * tq:(g + 1) * tq]
    blk = _iota(imp.shape, 0)
    qblk = (q0 + _iota(imp.shape, 1)) // SLC_LEN
    sel_b = _top_k_mask(_slc_scores(imp, qblk, blk), SLC_TOP, axis=0).astype(BF16)

    krow = _iota((tk, tq), 0)
    qpos = q0 + _iota((tk, tq), 1)

    def slc_body(j, carry):
        k0 = pl.multiple_of(j * tk, tk)
        k = ks_ref[pl.ds(k0, tk), :].astype(BF16)
        sel_e = _dot(et_ref[pl.ds(k0, tk), :], sel_b)
        bias = jnp.where((k0 + krow) <= qpos, (sel_e - 1.0) * -NEG, NEG)
        return _online_update_t(_nt(k, qs) * QK_LOG2 + rep(bias), *carry, vst_ref[:, pl.ds(k0, tk)])

    _, l, acc = lax.fori_loop(0, (q0 + tq + tk - 1) // tk, slc_body, _flash_init_t(G * tq))
    o_slc = _flash_finish(l, acc)

    nw = WINDOW + tq
    w0 = pl.multiple_of(jnp.maximum(q0 - WINDOW, 0), tq)
    d = (q0 + _iota((nw, tq), 1)) - (w0 + _iota((nw, tq), 0))
    t = _nt(kw_ref[pl.ds(w0, nw), :].astype(BF16), qs) * QK_LOG2 + rep(_mask_bias((d >= 0) & (d < WINDOW)))
    _, l, acc = _online_update_t(t, *_flash_init_t(G * tq), vwt_ref[:, pl.ds(w0, nw)])
    o_win = _flash_finish(l, acc)

    for g in range(G):
        col = (hkv * G + g) * N_BRANCH
        c = slice(g * tq, (g + 1) * tq)
        o = (o_cmp[:, c] * gt_ref[pl.ds(col, 1), :] + o_slc[:, c] * gt_ref[pl.ds(col + 1, 1), :]
             + o_win[:, c] * gt_ref[pl.ds(col + 2, 1), :])
        o_ref[:, g * HEAD_DIM:(g + 1) * HEAD_DIM] = o.T.astype(o_ref.dtype)


def _nsa_prompt_attn(z, comp, gates_t, B, T, *, tq=256, tk=512):
    tk = min(tk, T)
    assert T % tk == 0 and WINDOW % tq == 0 and T >= WINDOW + tq and T % 512 == 0
    nq = T // tq
    nm = comp.shape[1]
    G = NSA_GROUP
    qw = G * HEAD_DIM
    kvb = N_HEADS
    nb = T // SLC_LEN
    nbr = -(-nb // 8) * 8
    ovl = _overlap_shifted(nm, nb, nbr).T
    et = jnp.asarray((np.arange(T)[:, None] // SLC_LEN) == np.arange(nbr)[None, :], BF16)

    def kv_spec(off):
        return pl.BlockSpec((T, HEAD_DIM), lambda b, h, i: (b, kvb + off + h))

    return pl.pallas_call(
        functools.partial(_nsa_prompt_kernel, tq=tq, tk=tk, nm=nm, T=T), grid=(B, NSA_KV_HEADS, nq),
        in_specs=[pl.BlockSpec((tq, qw), lambda b, h, i: (b * nq + i, h)),
                  pl.BlockSpec((None, nm, HEAD_DIM), lambda b, h, i: (b, 0, h)),
                  pl.BlockSpec((None, nm, HEAD_DIM), lambda b, h, i: (b, 0, NSA_KV_HEADS + h)),
                  kv_spec(8), kv_spec(12), kv_spec(16), kv_spec(20),
                  pl.BlockSpec((LANES, tq), lambda b, h, i: (0, b * nq + i)),
                  pl.BlockSpec((nbr, nm), lambda b, h, i: (0, 0)),
                  pl.BlockSpec((T, nbr), lambda b, h, i: (0, 0))],
        out_specs=pl.BlockSpec((tq, qw), lambda b, h, i: (b * nq + i, h)),
        out_shape=jax.ShapeDtypeStruct((B * T, N_HEADS * HEAD_DIM), BF16),
        scratch_shapes=[pltpu.VMEM((HEAD_DIM, T), BF16), pltpu.VMEM((HEAD_DIM, T), BF16)], name="nsa_prompt",
        compiler_params=_cparams("parallel", "parallel", "arbitrary"))(z, comp, comp, z, z, z, z, gates_t, ovl, et)


NSA_Q = N_HEADS * HEAD_DIM
NSA_KV = NSA_KV_HEADS * HEAD_DIM
NSA_MAIN = NSA_Q + 2 * N_BRANCH * NSA_KV
HD = N_HEADS * HEAD_DIM


def _pad_cols(w, width=LANES):
    return jnp.pad(w, ((0, 0), (0, width - w.shape[1])))


def _nsa_rope_flags(tn=512):
    per = NSA_KV // tn
    flags = [1] * (NSA_Q // tn)
    for _ in range(N_BRANCH):
        flags += [1] * per + [0] * per
    return jnp.asarray(flags, I32)


def _nsa_project(x2d, g, w_in, pos_tables, tm):
    z = _norm_mm(x2d, g, w_in, NSA_MAIN, tm=tm, epi="rope", rope=(_nsa_rope_flags(),) + pos_tables)
    gates = _norm_mm(x2d, g, _pad_cols(w_in[:, NSA_MAIN:]), LANES, tm=tm, epi="sigmoid")
    return z, gates


def _nsa_prompt_mixer(x2d, g, w_in, pe, w1, w2, B, T, tm):
    z, gates = _nsa_project(x2d, g, w_in, _rope_tables(jnp.arange(T, dtype=I32)), tm)
    kv = [z[:, NSA_Q + 2 * NSA_KV * br:NSA_Q + 2 * NSA_KV * (br + 1)] for br in range(N_BRANCH)]
    rows = kv[0].reshape(B * T // PAGE, PAGE * ROWS_PER_KV, HEAD_DIM)
    pt = jnp.arange(B * T // PAGE, dtype=I32).reshape(B, T // PAGE)
    comp = _compress(rows, pt, w1, w2, pe)
    o = _nsa_prompt_attn(z, comp, gates.T, B, T)
    st = [a.reshape(B, T, 2, NSA_KV_HEADS, HEAD_DIM) for a in kv]
    return o, (st[0], st[1], st[2][:, T - min(WINDOW, T):])


def _moba_rope_flags(tn=512):
    return jnp.asarray([1] * (2 * HD // tn) + [0] * (HD // tn), I32)


def _moba_prompt_mixer(x2d, g, w_in, B, T, tm):
    z = _norm_mm(x2d, g, w_in, 3 * HD, tm=tm, epi="rope",
                 rope=(_moba_rope_flags(),) + _rope_tables(jnp.arange(T, dtype=I32)))
    o = _moba_prompt_attn(z, B, T)
    return o, (z[:, HD:].reshape(B, T, 2, N_HEADS, HEAD_DIM),)


def _fox_project(x2d, g, w_in, b_f, tm):
    z = _norm_mm(x2d, g, w_in, 3 * HD, tm=tm)
    logf = _norm_mm(x2d, g, _pad_cols(w_in[:, 3 * HD:]), LANES, tm=tm, epi="logsig",
                    bias=jnp.pad(b_f, (0, LANES - N_HEADS)))
    return z, logf


def _fox_prompt_mixer(x2d, g, w_in, b_f, B, T, tm):
    z, logf = _fox_project(x2d, g, w_in, b_f, tm)
    log_f = logf[:, :N_HEADS].reshape(B, T, N_HEADS)
    ct = _cumsum_rows(log_f.transpose(0, 2, 1))
    c = _pad_cols(ct.transpose(0, 2, 1).reshape(B * T, N_HEADS))
    o = _fox_prompt_attn(z, c, ct, B, T)
    return o, (z[:, HD:].reshape(B, T, 2, N_HEADS, HEAD_DIM), log_f)


def _ffn(x2d, g, wa, wb, wo, tm):
    u = _swiglu_up(x2d, g, wa, wb, tm=tm)
    return _mm_res(u, wo, x2d, 0.5, tm=tm, tn=FFN_DOWN_TN)


def _page_heads(x_ref, first, n, stride):
    return jnp.concatenate([x_ref[pl.ds(first + h, PAGE, stride=stride), :] for h in range(n)], axis=1)


def _diag_blocks(acc, nblk):
    rblk = _iota((acc.shape[0], HEAD_DIM), 0) % nblk
    out = jnp.zeros((acc.shape[0], HEAD_DIM), F32)
    for j in range(nblk):
        out = out + jnp.where(rblk == j, acc[:, j * HEAD_DIM:(j + 1) * HEAD_DIM], 0.0)
    return out


def _row_query(rows, mode):
    r = _iota((rows, LANES), 0)
    return (r % N_HEADS) // NSA_KV_HEADS if mode == "nsa" else r // N_HEADS


def _paged_attn_kernel(pt_ref, q_ref, x_ref, kn_ref, vn_ref, *rest, mode, nblk, n_pages):
    del pt_ref
    if mode == "fox":
        cq_ref, ct_ref, ctn_ref, o_ref, m_ref, l_ref, acc_ref = rest
    else:
        sel_ref, o_ref, m_ref, l_ref, acc_ref = rest
    p = pl.program_id(1)
    R = q_ref.shape[0]
    stride = 2 * nblk

    q = q_ref[...]
    lane = _iota((R, LANES), 1)
    roff = cq_ref[:, 0:1] * LOG2E if mode == "fox" else None

    def sel_bias(blk):
        if isinstance(blk, int):
            tile = sel_ref[:, (blk // LANES) * LANES:(blk // LANES + 1) * LANES]
        else:
            tile = sel_ref[:, pl.ds(pl.multiple_of((blk // LANES) * LANES, LANES), LANES)]
        return _mask_bias(_lane_col(tile, blk % LANES) > 0.5), _mask_bias(_lane_col(tile, blk % LANES + 1) > 0.5)

    def ct_rows(ref):
        return jnp.concatenate([ref[...] * LOG2E] * (R // N_HEADS), axis=0)

    def update(t, v, row_off):
        m, l, acc = _online_update(t, m_ref[...], l_ref[...], acc_ref[...], v, roff=row_off)
        m_ref[...] = m
        l_ref[...] = l
        acc_ref[...] = acc

    @pl.when(p == 0)
    def _():
        m_ref[...] = jnp.full(m_ref.shape, NEG, F32)
        l_ref[...] = jnp.zeros(l_ref.shape, F32)
        acc_ref[...] = jnp.zeros(acc_ref.shape, F32)
        tn = _nt(q, kn_ref[...].astype(BF16)) * QK_LOG2 + _mask_bias(lane <= _row_query(R, mode))
        if mode == "fox":
            tn = tn - ct_rows(ctn_ref)
        elif mode == "nsa":
            tn = tn + sel_bias(n_pages * (PAGE // SLC_LEN))[0]
        update(tn, vn_ref[...].astype(BF16), roff)

    k = _page_heads(x_ref, 0, nblk, stride).astype(BF16)
    v = _page_heads(x_ref, nblk, nblk, stride).astype(BF16)
    t = _nt(q, k) * QK_LOG2
    if mode == "fox":
        update(t - ct_rows(ct_ref), v, roff)
    elif mode == "moba":
        b0, b1 = sel_bias((p * PAGE // MOBA_BLOCK) // 2 * 2)
        update(t, v, jnp.where((p * PAGE // MOBA_BLOCK) % 2 == 0, b0, b1))
    else:
        b0, b1 = sel_bias(p * (PAGE // SLC_LEN))
        update(t + jnp.where(lane < SLC_LEN, b0, b1), v, None)

    @pl.when(p == n_pages - 1)
    def _():
        o_ref[...] = _diag_blocks(_flash_finish(l_ref[...], acc_ref[...]), nblk).astype(o_ref.dtype)


def _paged_attn(qbd, pool, page_table, k_new, v_new, extras, *, mode, out_dtype):
    B, R, C = qbd.shape
    P = page_table.shape[1]
    nblk = C // HEAD_DIM
    in_specs = [pl.BlockSpec((None, R, C), lambda b, p, pt: (b, 0, 0)),
                pl.BlockSpec((None, PAGE * 2 * nblk, HEAD_DIM), lambda b, p, pt: (pt[b, p], 0, 0)),
                pl.BlockSpec((None, PAGE, C), lambda b, p, pt: (b, 0, 0)),
                pl.BlockSpec((None, PAGE, C), lambda b, p, pt: (b, 0, 0))]
    if mode == "fox":
        cq, ct = extras
        in_specs += [pl.BlockSpec((None, R, LANES), lambda b, p, pt: (b, 0, 0)),
                     pl.BlockSpec((None, N_HEADS, LANES), lambda b, p, pt: (b, 0, p)),
                     pl.BlockSpec((None, N_HEADS, LANES), lambda b, p, pt: (b, 0, P))]
        args = (cq, ct, ct)
    else:
        (sel,) = extras
        in_specs += [pl.BlockSpec((None, R, sel.shape[2]), lambda b, p, pt: (b, 0, 0))]
        args = (sel,)
    gs = pltpu.PrefetchScalarGridSpec(
        num_scalar_prefetch=1, grid=(B, P), in_specs=in_specs,
        out_specs=pl.BlockSpec((None, R, HEAD_DIM), lambda b, p, pt: (b, 0, 0)),
        scratch_shapes=[pltpu.VMEM((R, 1), F32), pltpu.VMEM((R, 1), F32), pltpu.VMEM((R, C), F32)])
    return pl.pallas_call(
        functools.partial(_paged_attn_kernel, mode=mode, nblk=nblk, n_pages=P), grid_spec=gs,
        out_shape=jax.ShapeDtypeStruct((B, R, HEAD_DIM), out_dtype), name="paged_attn_" + mode,
        compiler_params=_cparams("parallel", "arbitrary"))(page_table, qbd, pool, k_new, v_new, *args)


PAGES_PER_STEP = 8


def _cumsum_paged_kernel(pt_ref, *refs, n_groups):
    del pt_ref
    x_refs = refs[:PAGES_PER_STEP]
    xn_ref, o_ref, carry_ref = refs[PAGES_PER_STEP:]
    g = pl.program_id(1)

    @pl.when(g == 0)
    def _():
        carry_ref[...] = jnp.zeros(carry_ref.shape, F32)

    carry = carry_ref[...]
    for r in range(PAGES_PER_STEP):
        tail = xn_ref[...] if r == 0 else jnp.zeros(xn_ref.shape, F32)
        cs = _tile_cumsum(jnp.where(g == n_groups, tail, x_refs[r][...])) + carry
        o_ref[:, r * PAGE:(r + 1) * PAGE] = cs
        carry = jnp.broadcast_to(cs[:, LANES - 1:LANES], carry.shape)
    carry_ref[...] = carry


def _cumsum_paged(pool_t, page_table, new_t):
    B, P = page_table.shape
    H = pool_t.shape[1]
    n = PAGES_PER_STEP
    assert P % n == 0
    ng = P // n
    page = [pl.BlockSpec((None, H, PAGE), lambda b, g, pt, r=r: (pt[b, jnp.minimum(g, ng - 1) * n + r], 0, 0))
            for r in range(n)]
    gs = pltpu.PrefetchScalarGridSpec(
        num_scalar_prefetch=1, grid=(B, ng + 1),
        in_specs=page + [pl.BlockSpec((None, H, PAGE), lambda b, g, pt: (b, 0, 0))],
        out_specs=pl.BlockSpec((None, H, n * PAGE), lambda b, g, pt: (b, 0, g)),
        scratch_shapes=[pltpu.VMEM((H, LANES), F32)])
    return pl.pallas_call(
        functools.partial(_cumsum_paged_kernel, n_groups=ng), grid_spec=gs,
        out_shape=jax.ShapeDtypeStruct((B, H, (ng + 1) * n * PAGE), F32), name="cumsum_paged",
        compiler_params=_cparams("parallel", "arbitrary"))(page_table, *([pool_t] * n), new_t)


def _kmeans_kernel(pt_ref, *refs, ppb):
    del pt_ref
    x_refs, o_ref = refs[:PAGES_PER_STEP], refs[PAGES_PER_STEP]
    for blk in range(PAGES_PER_STEP // ppb):
        tot = jnp.sum(x_refs[blk * ppb][...], axis=0)
        for r in range(1, ppb):
            tot = tot + jnp.sum(x_refs[blk * ppb + r][...], axis=0)
        o_ref[blk] = tot * (1.0 / MOBA_BLOCK)


def _moba_kmeans(pool5, page_table):
    B, P = page_table.shape
    ppb = MOBA_BLOCK // PAGE
    n = PAGES_PER_STEP
    assert P % n == 0 and n % ppb == 0
    page = [pl.BlockSpec((None, PAGE, None, N_HEADS, HEAD_DIM), lambda b, g, pt, r=r: (pt[b, g * n + r], 0, 0, 0, 0))
            for r in range(n)]
    gs = pltpu.PrefetchScalarGridSpec(
        num_scalar_prefetch=1, grid=(B, P // n), in_specs=page,
        out_specs=pl.BlockSpec((None, n // ppb, N_HEADS, HEAD_DIM), lambda b, g, pt: (b, g, 0, 0)))
    return pl.pallas_call(
        functools.partial(_kmeans_kernel, ppb=ppb), grid_spec=gs,
        out_shape=jax.ShapeDtypeStruct((B, P // ppb, N_HEADS, HEAD_DIM), F32), name="moba_kmeans",
        compiler_params=_cparams("parallel", "parallel"))(page_table, *([pool5] * n))


def _moba_select_kernel(q_ref, km_ref, o_ref, *, n_past_blocks):
    gate = _nt(q_ref[...], km_ref[...].astype(BF16))
    lane = _iota(gate.shape, 1)
    gate = jnp.where(lane < n_past_blocks, gate, NEG_INF)
    o_ref[...] = _top_k_mask(gate, MOBA_TOP, finite_only=True)


def _moba_select(qbd, kmeans, n_past_blocks):
    B, R, C = qbd.shape
    return pl.pallas_call(
        functools.partial(_moba_select_kernel, n_past_blocks=n_past_blocks), grid=(B,),
        in_specs=[pl.BlockSpec((None, R, C), lambda b: (b, 0, 0)), pl.BlockSpec((None, LANES, C), lambda b: (b, 0, 0))],
        out_specs=pl.BlockSpec((None, R, LANES), lambda b: (b, 0, 0)),
        out_shape=jax.ShapeDtypeStruct((B, R, LANES), F32), name="moba_select",
        compiler_params=_cparams("parallel"))(qbd, kmeans)


def _nsa_cmp_sample_kernel(q_ref, kc_ref, vc_ref, ovl_ref, o_ref, sel_ref, *, past):
    R = q_ref.shape[0]
    nm = kc_ref.shape[0]
    G = NSA_GROUP
    s = _nt(q_ref[...], kc_ref[...].astype(BF16)) * SCALE
    m_i = _iota((R, nm), 1)
    qpos = past + (_iota((R, nm), 0) % N_HEADS) // NSA_KV_HEADS
    ok = (m_i >= 1) & ((m_i - 1) * CMP_STRIDE + (CMP_LEN - 1) <= qpos)
    p_cmp = _masked_softmax(s, ok).astype(BF16)
    o_ref[...] = _diag_blocks(_dot(p_cmp, vc_ref[...].astype(BF16)), NSA_KV_HEADS)
    pov = _dot(p_cmp, ovl_ref[...])
    rg = R // G
    imp = pov[0:rg]
    for g in range(1, G):
        imp = imp + pov[g * rg:(g + 1) * rg]
    lane = _iota(imp.shape, 1)
    qblk = (past + _iota(imp.shape, 0) // NSA_KV_HEADS) // SLC_LEN
    sel = _top_k_mask(_slc_scores(imp, qblk, lane), SLC_TOP)
    sel_ref[...] = jnp.concatenate([sel] * G, axis=0)


def _nsa_cmp_sample(qbd, comp, past):
    B, R, C = qbd.shape
    nm = comp.shape[1]
    nb = -(-(past + R // N_HEADS) // SLC_LEN)
    width = -(-nb // LANES) * LANES
    ovl = _overlap_shifted(nm, nb, width)
    return pl.pallas_call(
        functools.partial(_nsa_cmp_sample_kernel, past=past), grid=(B,),
        in_specs=[pl.BlockSpec((None, R, C), lambda b: (b, 0, 0)),
                  pl.BlockSpec((None, nm, C), lambda b: (b, 0, 0)),
                  pl.BlockSpec((None, nm, C), lambda b: (b, 0, 1)),
                  pl.BlockSpec((nm, width), lambda b: (0, 0))],
        out_specs=[pl.BlockSpec((None, R, HEAD_DIM), lambda b: (b, 0, 0)),
                   pl.BlockSpec((None, R, width), lambda b: (b, 0, 0))],
        out_shape=[jax.ShapeDtypeStruct((B, R, HEAD_DIM), F32), jax.ShapeDtypeStruct((B, R, width), F32)],
        name="nsa_cmp_sample", compiler_params=_cparams("parallel"))(qbd, comp, comp, ovl)


def _nsa_win_sample_kernel(q_ref, kw_ref, vw_ref, oc_ref, os_ref, g_ref, o_ref, *, past, wb):
    R = q_ref.shape[0]
    nk = kw_ref.shape[0]
    s = _nt(q_ref[...], kw_ref[...].astype(BF16)) * SCALE
    j = _iota((R, nk), 1)
    qpos = past + (_iota((R, nk), 0) % N_HEADS) // NSA_KV_HEADS
    wpos = past - wb + j
    d = qpos - wpos
    ok = (d >= 0) & (d < WINDOW) & (wpos >= 0)
    p = _masked_softmax(s, ok).astype(BF16)
    o_win = _diag_blocks(_dot(p, vw_ref[...].astype(BF16)), NSA_KV_HEADS)
    g = g_ref[...]
    o = oc_ref[...] * g[:, 0:1] + os_ref[...] * g[:, 1:2] + o_win * g[:, 2:3]
    o_ref[...] = o.astype(o_ref.dtype)


def _nsa_win_sample(qbd, kw, o_cmp, o_slc, gates, past, wb):
    B, R, C = qbd.shape
    nk = kw.shape[1]
    row = pl.BlockSpec((None, R, HEAD_DIM), lambda b: (b, 0, 0))
    return pl.pallas_call(
        functools.partial(_nsa_win_sample_kernel, past=past, wb=wb), grid=(B,),
        in_specs=[pl.BlockSpec((None, R, C), lambda b: (b, 0, 0)),
                  pl.BlockSpec((None, nk, C), lambda b: (b, 0, 0)),
                  pl.BlockSpec((None, nk, C), lambda b: (b, 0, 1)),
                  row, row, row],
        out_specs=row, out_shape=jax.ShapeDtypeStruct((B, R, HEAD_DIM), BF16), name="nsa_win_sample",
        compiler_params=_cparams("parallel"))(qbd, kw, kw, o_cmp, o_slc, gates)


def _qbd_heads(q):
    B, Q, H, dh = q.shape
    x = q[:, :, :, None, :] * jnp.eye(H, dtype=q.dtype)[None, None, :, :, None]
    return x.reshape(B, Q * H, H * dh).astype(BF16)


def _qbd_groups(q):
    B, Q, H, dh = q.shape
    x = q.reshape(B, Q, NSA_KV_HEADS, NSA_GROUP, dh).transpose(0, 3, 1, 2, 4)
    x = x[:, :, :, :, None, :] * jnp.eye(NSA_KV_HEADS, dtype=q.dtype)[None, None, None, :, :, None]
    return x.reshape(B, NSA_GROUP * Q * NSA_KV_HEADS, NSA_KV_HEADS * dh).astype(BF16)


def _pad_rows(x, rows):
    return jnp.pad(x, ((0, 0), (0, rows - x.shape[1]), (0, 0)))


def _sample_tables(B, Q, past):
    return _rope_tables(jnp.tile(past + jnp.arange(Q, dtype=I32), B))


def _nsa_sample_mixer(x2d, g, w_in, pe, w1, w2, pool_cmp, pool_slc, win_buf, page_table, layer, B, Q):
    P = page_table.shape[1]
    past = P * PAGE
    assert past % SLC_LEN == 0 and Q * N_HEADS % 8 == 0
    n_pool = pool_cmp.shape[1]
    pt = page_table + layer * n_pool
    z, gates = _nsa_project(x2d, g, w_in, _sample_tables(B, Q, past), B * Q)
    kv = [z[:, NSA_Q + 2 * NSA_KV * br:NSA_Q + 2 * NSA_KV * (br + 1)].reshape(B, Q, 2 * NSA_KV) for br in range(N_BRANCH)]
    assert (past + Q - CMP_LEN) // CMP_STRIDE + 1 == past // CMP_STRIDE - 1
    comp = _compress(pool_cmp.reshape(-1, PAGE * ROWS_PER_KV, HEAD_DIM), pt, w1, w2, pe)
    qbd = _qbd_groups(z[:, :NSA_Q].reshape(B, Q, N_HEADS, HEAD_DIM))
    o_cmp, sel = _nsa_cmp_sample(qbd, comp, past)
    o_slc = _paged_attn(qbd, pool_slc.reshape(-1, PAGE * ROWS_PER_KV, HEAD_DIM), pt,
                        _pad_rows(kv[1][..., :NSA_KV], PAGE), _pad_rows(kv[1][..., NSA_KV:], PAGE), (sel,),
                        mode="nsa", out_dtype=F32)
    wb = win_buf.shape[2]
    kw = jnp.concatenate([win_buf[layer].reshape(B, wb, 2 * NSA_KV), kv[2]], axis=1)
    g_rows = gates[:, :N_HEADS * N_BRANCH].reshape(B, Q, NSA_KV_HEADS, NSA_GROUP, N_BRANCH)
    g_rows = jnp.pad(g_rows.transpose(0, 3, 1, 2, 4).reshape(B, Q * N_HEADS, N_BRANCH), ((0, 0), (0, 0), (0, LANES - N_BRANCH)))
    o = _nsa_win_sample(qbd, _pad_rows(kw, -(-(wb + Q) // LANES) * LANES), o_cmp, o_slc, g_rows, past, wb)
    o = o.reshape(B, NSA_GROUP, Q, NSA_KV_HEADS, HEAD_DIM).transpose(0, 2, 3, 1, 4).reshape(B * Q, HD)
    shp = (B, -1, 2, NSA_KV_HEADS, HEAD_DIM)
    return o, (kv[0].reshape(shp), kv[1].reshape(shp), kw[:, Q:].reshape(shp))


def _moba_sample_mixer(x2d, g, w_in, pool, page_table, layer, B, Q):
    P = page_table.shape[1]
    past = P * PAGE
    assert past % MOBA_BLOCK == 0 and Q <= MOBA_BLOCK and past // MOBA_BLOCK <= LANES
    n_pool = pool.shape[1]
    pt = page_table + layer * n_pool
    z = _norm_mm(x2d, g, w_in, 3 * HD, tm=B * Q, epi="rope", rope=(_moba_rope_flags(),) + _sample_tables(B, Q, past))
    qbd = _qbd_heads(z[:, :HD].reshape(B, Q, N_HEADS, HEAD_DIM))
    nbp = past // MOBA_BLOCK
    km = _moba_kmeans(pool.reshape(-1, PAGE, 2, N_HEADS, HEAD_DIM), pt).reshape(B, nbp, HD)
    sel = _moba_select(qbd, _pad_rows(km, LANES), nbp)
    new = z[:, HD:].reshape(B, Q, 2 * HD)
    o = _paged_attn(qbd, pool.reshape(-1, PAGE * 2 * N_HEADS, HEAD_DIM), pt,
                    _pad_rows(new[..., :HD], PAGE), _pad_rows(new[..., HD:], PAGE), (sel,), mode="moba", out_dtype=BF16)
    return o.reshape(B * Q, HD), (new.reshape(B, Q, 2, N_HEADS, HEAD_DIM),)


def _fox_sample_mixer(x2d, g, w_in, b_f, pool, pool_logf, page_table, layer, B, Q):
    P = page_table.shape[1]
    n_pool = pool.shape[1]
    pt = page_table + layer * n_pool
    z, logf = _fox_project(x2d, g, w_in, b_f, B * Q)
    log_f = logf[:, :N_HEADS].reshape(B, Q, N_HEADS)
    pool_t = pool_logf.astype(F32).transpose(0, 1, 3, 2).reshape(-1, N_HEADS, PAGE)
    new_t = jnp.pad(log_f.transpose(0, 2, 1), ((0, 0), (0, 0), (0, PAGE - Q)))
    ct = _cumsum_paged(pool_t, pt, new_t)
    cq = ct[:, :, P * PAGE:P * PAGE + Q].transpose(0, 2, 1).reshape(B, Q * N_HEADS, 1)
    qbd = _qbd_heads(z[:, :HD].reshape(B, Q, N_HEADS, HEAD_DIM))
    new = z[:, HD:].reshape(B, Q, 2 * HD)
    o = _paged_attn(qbd, pool.reshape(-1, PAGE * 2 * N_HEADS, HEAD_DIM), pt,
                    _pad_rows(new[..., :HD], PAGE), _pad_rows(new[..., HD:], PAGE),
                    (jnp.broadcast_to(cq, (B, Q * N_HEADS, LANES)), ct), mode="fox", out_dtype=BF16)
    return o.reshape(B * Q, HD), (new.reshape(B, Q, 2, N_HEADS, HEAD_DIM), log_f)


PROMPT_TM = 1024
FFN_DOWN_TN = 256


def kernel(x_prompt, x_sample, cache_nsa_cmp, cache_nsa_slc, cache_nsa_win, cache_moba_kv, cache_fox_kv,
           cache_fox_logf, page_table, norms, ffn_wi, ffn_wo, final_norm, nsa_w_in, nsa_cmp_pe, nsa_cmp_w1,
           nsa_cmp_w2, nsa_w_out, moba_w_in, moba_w_out, fox_w_in, fox_b_f, fox_w_out):
    B, T, D = x_prompt.shape
    Bs, Q, _ = x_sample.shape
    depth = norms.shape[0]
    xp = x_prompt.reshape(B * T, D)
    xs = x_sample.reshape(Bs * Q, D)
    tp = min(PROMPT_TM, B * T)
    ts = Bs * Q
    ffn = ffn_wi.shape[-1] // 2
    ffn_wa, ffn_wb, ffn_wo = ffn_wi[..., :ffn].astype(BF16), ffn_wi[..., ffn:].astype(BF16), ffn_wo.astype(BF16)
    nsa_w_in, moba_w_in, fox_w_in = nsa_w_in.astype(BF16), moba_w_in.astype(BF16), fox_w_in.astype(BF16)
    nsa_w_out, moba_w_out, fox_w_out = nsa_w_out.astype(BF16), moba_w_out.astype(BF16), fox_w_out.astype(BF16)
    st_p = {0: [], 1: [], 2: []}
    st_s = {0: [], 1: [], 2: []}
    for i in range(depth):
        kind, j = i % 3, i // 3
        xp = _ffn(xp, norms[i, 0], ffn_wa[i, 0], ffn_wb[i, 0], ffn_wo[i, 0], tp)
        xs = _ffn(xs, norms[i, 0], ffn_wa[i, 0], ffn_wb[i, 0], ffn_wo[i, 0], ts)
        g = norms[i, 1]
        if kind == 0:
            op, sp = _nsa_prompt_mixer(xp, g, nsa_w_in[j], nsa_cmp_pe[j], nsa_cmp_w1[j], nsa_cmp_w2[j], B, T, tp)
            os_, ss = _nsa_sample_mixer(xs, g, nsa_w_in[j], nsa_cmp_pe[j], nsa_cmp_w1[j], nsa_cmp_w2[j],
                                        cache_nsa_cmp, cache_nsa_slc, cache_nsa_win, page_table, j, Bs, Q)
            w_out = nsa_w_out[j]
        elif kind == 1:
            op, sp = _moba_prompt_mixer(xp, g, moba_w_in[j], B, T, tp)
            os_, ss = _moba_sample_mixer(xs, g, moba_w_in[j], cache_moba_kv, page_table, j, Bs, Q)
            w_out = moba_w_out[j]
        else:
            op, sp = _fox_prompt_mixer(xp, g, fox_w_in[j], fox_b_f[j], B, T, tp)
            os_, ss = _fox_sample_mixer(xs, g, fox_w_in[j], fox_b_f[j], cache_fox_kv, cache_fox_logf, page_table, j, Bs, Q)
            w_out = fox_w_out[j]
        st_p[kind].append(sp)
        st_s[kind].append(ss)
        xp = _mm_res(op, w_out, xp, 1.0, tm=tp, tn=512)
        xs = _mm_res(os_, w_out, xs, 1.0, tm=ts, tn=512)
        xp = _ffn(xp, norms[i, 2], ffn_wa[i, 1], ffn_wb[i, 1], ffn_wo[i, 1], tp)
        xs = _ffn(xs, norms[i, 2], ffn_wa[i, 1], ffn_wb[i, 1], ffn_wo[i, 1], ts)
    y_prompt = _rmsnorm(xp, final_norm, tm=tp).reshape(B, T, D)
    y_sample = _rmsnorm(xs, final_norm, tm=ts).reshape(Bs, Q, D)

    def stack(states, k):
        return jnp.stack([s[k] for s in states])

    return (y_prompt, y_sample,
            stack(st_p[0], 0), stack(st_p[0], 1), stack(st_p[0], 2),
            stack(st_s[0], 0), stack(st_s[0], 1), stack(st_s[0], 2),
            stack(st_p[1], 0), stack(st_s[1], 0),
            stack(st_p[2], 0), stack(st_p[2], 1), stack(st_s[2], 0), stack(st_s[2], 1))
```

```python
import functools

import numpy as np
import jax
import jax.numpy as jnp
from jax import lax
from jax.experimental import pallas as pl
from jax.experimental.pallas import tpu as pltpu

F32 = jnp.float32
BF16 = jnp.bfloat16
I32 = jnp.int32

LANES = 128
VMEM_LIMIT_BYTES = 56 << 20

N_HEADS = 16
HEAD_DIM = 128
ROT_DIM = HEAD_DIM // 4
ROPE_THETA = 500000.0
NORM_EPS = 1e-6
PAGE = 128
NSA_KV_HEADS = 4
NSA_GROUP = N_HEADS // NSA_KV_HEADS
CMP_LEN = 32
CMP_STRIDE = 16
SLC_LEN = 64
SLC_TOP = 16
SLC_LOCAL = 2
WINDOW = 512
N_BRANCH = 3
MOBA_BLOCK = 256
MOBA_TOP = 3
SCALE = HEAD_DIM ** -0.5
LOG2E = 1.4426950408889634
QK_LOG2 = SCALE * LOG2E
ROW_CHUNK = 64
NEG = -1e30
NEG_INF = float("-inf")
POS_INF = float("inf")


def _cparams(*sem):
    return pltpu.CompilerParams(dimension_semantics=sem, vmem_limit_bytes=VMEM_LIMIT_BYTES)


def _nt(a, b):
    return lax.dot_general(a, b, (((1,), (1,)), ((), ())), preferred_element_type=F32)


def _dot(a, b):
    return jnp.dot(a, b, preferred_element_type=F32)


def _iota(shape, dim):
    return lax.broadcasted_iota(I32, shape, dim)


def _lane_col(x, idx):
    return jnp.sum(jnp.where(_iota(x.shape, 1) == idx, x, 0.0), axis=-1, keepdims=True)


def _rms_to_bf16(x_ref, g_ref):
    x = x_ref[...]
    var = jnp.mean(x * x, axis=-1, keepdims=True)
    return ((x * lax.rsqrt(var + NORM_EPS)) * g_ref[...]).astype(BF16)


def _log_sigmoid(x):
    return jnp.minimum(x, 0.0) - jnp.log1p(jnp.exp(-jnp.abs(x)))


def _norm_mm_kernel(*refs, epi, tn):
    if epi == "rope":
        flags_ref, x_ref, g_ref, w_ref, c_ref, s1_ref, s2_ref, o_ref, xn_ref = refs
    elif epi == "logsig":
        x_ref, g_ref, w_ref, b_ref, o_ref, xn_ref = refs
    else:
        x_ref, g_ref, w_ref, o_ref, xn_ref = refs
    j = pl.program_id(1)

    @pl.when(j == 0)
    def _():
        xn_ref[...] = _rms_to_bf16(x_ref, g_ref)

    z = _dot(xn_ref[...], w_ref[...].astype(BF16))
    if epi == "rope":
        @pl.when(flags_ref[j] == 1)
        def _():
            c, s1, s2 = c_ref[...], s1_ref[...], s2_ref[...]
            for hh in range(tn // HEAD_DIM):
                zs = z[:, hh * HEAD_DIM:(hh + 1) * HEAD_DIM]
                o_ref[:, hh * HEAD_DIM:(hh + 1) * HEAD_DIM] = (
                    zs * c + pltpu.roll(zs, HEAD_DIM - ROT_DIM // 2, 1) * s1 + pltpu.roll(zs, ROT_DIM // 2, 1) * s2)

        @pl.when(flags_ref[j] == 0)
        def _():
            o_ref[...] = z
    elif epi == "sigmoid":
        o_ref[...] = jax.nn.sigmoid(z)
    elif epi == "logsig":
        o_ref[...] = _log_sigmoid(z + b_ref[...])
    else:
        o_ref[...] = z


def _norm_mm(x, g, w, n_out, *, tm, tn=512, epi="none", rope=None, bias=None):
    M, D = x.shape
    tn = min(tn, n_out)
    assert M % tm == 0 and n_out % tn == 0
    grid = (M // tm, n_out // tn)
    g2 = g.reshape(1, D)
    kern = functools.partial(_norm_mm_kernel, epi=epi, tn=tn)
    scratch = [pltpu.VMEM((tm, D), BF16)]
    out_shape = jax.ShapeDtypeStruct((M, n_out), F32)
    if epi == "rope":
        flags, c, s1, s2 = rope
        nt = c.shape[0] // tm
        tab = pl.BlockSpec((tm, HEAD_DIM), lambda i, j, f: (i % nt, 0))
        gs = pltpu.PrefetchScalarGridSpec(
            num_scalar_prefetch=1, grid=grid,
            in_specs=[pl.BlockSpec((tm, D), lambda i, j, f: (i, 0)),
                      pl.BlockSpec((1, D), lambda i, j, f: (0, 0)),
                      pl.BlockSpec((D, tn), lambda i, j, f: (0, j)),
                      tab, tab, tab],
            out_specs=pl.BlockSpec((tm, tn), lambda i, j, f: (i, j)),
            scratch_shapes=scratch)
        return pl.pallas_call(kern, grid_spec=gs, out_shape=out_shape, name="norm_mm_rope",
                              compiler_params=_cparams("parallel", "arbitrary"))(flags, x, g2, w, c, s1, s2)
    in_specs = [pl.BlockSpec((tm, D), lambda i, j: (i, 0)),
                pl.BlockSpec((1, D), lambda i, j: (0, 0)),
                pl.BlockSpec((D, tn), lambda i, j: (0, j))]
    args = [x, g2, w]
    if epi == "logsig":
        in_specs.append(pl.BlockSpec((1, tn), lambda i, j: (0, j)))
        args.append(bias.reshape(1, n_out))
    return pl.pallas_call(kern, grid=grid, in_specs=in_specs,
                          out_specs=pl.BlockSpec((tm, tn), lambda i, j: (i, j)),
                          out_shape=out_shape, scratch_shapes=scratch, name="norm_mm_" + epi,
                          compiler_params=_cparams("parallel", "arbitrary"))(*args)


def _swiglu_up_kernel(x_ref, g_ref, wa_ref, wb_ref, o_ref, xn_ref):
    @pl.when(pl.program_id(1) == 0)
    def _():
        xn_ref[...] = _rms_to_bf16(x_ref, g_ref)

    xn = xn_ref[...]
    a = _dot(xn, wa_ref[...])
    b = _dot(xn, wb_ref[...])
    o_ref[...] = (a * jax.nn.sigmoid(a) * b).astype(o_ref.dtype)


def _swiglu_up(x, g, wa, wb, *, tm, tn=512):
    M, D = x.shape
    F = wa.shape[1]
    assert M % tm == 0
    w_spec = pl.BlockSpec((D, tn), lambda i, j: (0, j))
    return pl.pallas_call(
        _swiglu_up_kernel,
        grid=(M // tm, pl.cdiv(F, tn)),
        in_specs=[pl.BlockSpec((tm, D), lambda i, j: (i, 0)), pl.BlockSpec((1, D), lambda i, j: (0, 0)), w_spec, w_spec],
        out_specs=pl.BlockSpec((tm, tn), lambda i, j: (i, j)),
        out_shape=jax.ShapeDtypeStruct((M, F), BF16),
        scratch_shapes=[pltpu.VMEM((tm, D), BF16)],
        name="swiglu_up",
        compiler_params=_cparams("parallel", "arbitrary"))(x, g.reshape(1, D), wa, wb)


def _mm_res_kernel(u_ref, w_ref, r_ref, o_ref, *, scale):
    y = _dot(u_ref[...].astype(BF16), w_ref[...].astype(BF16))
    o_ref[...] = r_ref[...] + (y if scale == 1.0 else scale * y)


def _mm_res(u, w, res, scale, *, tm, tn):
    M, K = u.shape
    N = w.shape[1]
    assert M % tm == 0 and N % tn == 0
    return pl.pallas_call(
        functools.partial(_mm_res_kernel, scale=scale),
        grid=(M // tm, N // tn),
        in_specs=[pl.BlockSpec((tm, K), lambda i, j: (i, 0)),
                  pl.BlockSpec((K, tn), lambda i, j: (0, j)),
                  pl.BlockSpec((tm, tn), lambda i, j: (i, j))],
        out_specs=pl.BlockSpec((tm, tn), lambda i, j: (i, j)),
        out_shape=jax.ShapeDtypeStruct((M, N), F32), name="mm_res",
        compiler_params=_cparams("parallel", "arbitrary"))(u, w, res)


def _rmsnorm_kernel(x_ref, g_ref, o_ref):
    x = x_ref[...]
    var = jnp.mean(x * x, axis=-1, keepdims=True)
    o_ref[...] = (x * lax.rsqrt(var + NORM_EPS)) * g_ref[...]


def _rmsnorm(x, g, *, tm):
    M, D = x.shape
    return pl.pallas_call(
        _rmsnorm_kernel, grid=(M // tm,),
        in_specs=[pl.BlockSpec((tm, D), lambda i: (i, 0)), pl.BlockSpec((1, D), lambda i: (0, 0))],
        out_specs=pl.BlockSpec((tm, D), lambda i: (i, 0)),
        out_shape=jax.ShapeDtypeStruct((M, D), F32), name="rmsnorm",
        compiler_params=_cparams("parallel"))(x, g.reshape(1, D))


def _rope_tables(pos):
    half = ROT_DIM // 2
    inv = ROPE_THETA ** (-jnp.arange(half, dtype=F32) / half)
    ang = pos.astype(F32)[:, None] * inv[None, :]
    cos, sin = jnp.cos(ang), jnp.sin(ang)
    n = pos.shape[0]
    zeros = jnp.zeros((n, HEAD_DIM - ROT_DIM), F32)
    zh = jnp.zeros((n, half), F32)
    c = jnp.concatenate([cos, cos, jnp.ones((n, HEAD_DIM - ROT_DIM), F32)], axis=1)
    s1 = jnp.concatenate([-sin, zh, zeros], axis=1)
    s2 = jnp.concatenate([zh, sin, zeros], axis=1)
    return c, s1, s2


def _online_update(t, m, l, acc, v, roff=None, pv_roll=0):
    ms, ls, als, ps = [], [], [], []
    for r0 in range(0, t.shape[0], ROW_CHUNK):
        r = slice(r0, r0 + ROW_CHUNK)
        tmax = jnp.max(t[r], axis=-1, keepdims=True)
        if roff is not None:
            tmax = tmax + roff[r]
        m_new = jnp.maximum(m[r], tmax)
        alpha = jnp.exp2(m[r] - m_new)
        p = jnp.exp2(t[r] + ((roff[r] - m_new) if roff is not None else -m_new))
        ms.append(m_new)
        als.append(alpha)
        ls.append(alpha * l[r] + jnp.sum(p, axis=-1, keepdims=True))
        ps.append((pltpu.roll(p, pv_roll, 1) if pv_roll else p).astype(BF16))
    cat = lambda xs: xs[0] if len(xs) == 1 else jnp.concatenate(xs, axis=0)
    return cat(ms), cat(ls), cat(als) * acc + _dot(cat(ps), v)


def _online_update_t(t, m, l, acc, vt, coff=None):
    tmax = jnp.max(t, axis=0, keepdims=True)
    if coff is not None:
        tmax = tmax + coff
    m_new = jnp.maximum(m, tmax)
    alpha = jnp.exp2(m - m_new)
    p = jnp.exp2(t + ((coff - m_new) if coff is not None else -m_new))
    l_new = alpha * l + jnp.sum(p, axis=0, keepdims=True)
    return m_new, l_new, alpha * acc + _dot(vt, p.astype(BF16))


def _flash_init_t(n_q, width=HEAD_DIM):
    return (jnp.full((1, n_q), NEG, F32), jnp.zeros((1, n_q), F32), jnp.zeros((width, n_q), F32))


def _transpose_into(dst_ref, src_ref, n_rows, dtype, chunk=512):
    for r0 in range(0, n_rows, chunk):
        r1 = min(r0 + chunk, n_rows)
        dst_ref[:, r0:r1] = src_ref[r0:r1, :].T.astype(dtype)


def _mask_bias(ok):
    return jnp.where(ok, 0.0, NEG)


def _flash_init(rows, width=HEAD_DIM):
    return (jnp.full((rows, 1), NEG, F32), jnp.zeros((rows, 1), F32), jnp.zeros((rows, width), F32))


def _flash_finish(l, acc):
    return acc / jnp.where(l > 0.0, l, 1.0)


def _masked_softmax(s, ok, axis=-1):
    sm = jnp.where(ok, s, NEG)
    mx = jnp.max(sm, axis=axis, keepdims=True)
    e = jnp.where(ok, jnp.exp(sm - mx), 0.0)
    d = jnp.sum(e, axis=axis, keepdims=True)
    return e / jnp.where(d > 0.0, d, 1.0)


def _top_k_mask(score, k, axis=-1, finite_only=False):
    idx = _iota(score.shape, axis % score.ndim)
    n = score.shape[axis]
    sel = jnp.zeros(score.shape, F32)
    for _ in range(k):
        mx = jnp.max(score, axis=axis, keepdims=True)
        first = jnp.min(jnp.where(score == mx, idx, n), axis=axis, keepdims=True)
        pick = idx == first
        sel = jnp.where((pick & (mx > NEG_INF)) if finite_only else pick, 1.0, sel)
        score = jnp.where(pick, NEG_INF, score)
    return sel


def _tile_cumsum(x):
    lane = _iota(x.shape, 1)
    for s in (1, 2, 4, 8, 16, 32, 64):
        x = x + jnp.where(lane >= s, pltpu.roll(x, s, 1), 0.0)
    return x


def _cumsum_rows_kernel(x_ref, o_ref, *, n_tiles):
    carry = jnp.zeros((x_ref.shape[0], 1), F32)
    for j in range(n_tiles):
        cs = _tile_cumsum(x_ref[:, j * LANES:(j + 1) * LANES]) + carry
        o_ref[:, j * LANES:(j + 1) * LANES] = cs
        carry = cs[:, LANES - 1:LANES]


def _cumsum_rows(xt):
    B, H, T = xt.shape
    return pl.pallas_call(
        functools.partial(_cumsum_rows_kernel, n_tiles=T // LANES), grid=(B,),
        in_specs=[pl.BlockSpec((None, H, T), lambda b: (b, 0, 0))],
        out_specs=pl.BlockSpec((None, H, T), lambda b: (b, 0, 0)),
        out_shape=jax.ShapeDtypeStruct((B, H, T), F32), name="cumsum_rows",
        compiler_params=_cparams("parallel"))(xt)


def _fox_prompt_kernel(q_ref, k_ref, v_ref, c_ref, ct_ref, o_ref, vt_ref, ck_ref, *, tq, T):
    h = pl.program_id(1)
    qi = pl.program_id(2)

    @pl.when(qi == 0)
    def _():
        _transpose_into(vt_ref, v_ref, T, BF16)
        for r0 in range(0, T, 512):
            col = _lane_col(c_ref[r0:r0 + 512, :], h) * LOG2E
            ck_ref[r0:r0 + 512, :] = jnp.broadcast_to(col, (512, LANES))

    q = q_ref[...].astype(BF16)
    q0 = pl.multiple_of(qi * tq, tq)
    cq2 = ct_ref[pl.ds(h, 1), pl.ds(q0, tq)] * LOG2E

    def step(j, carry, diagonal):
        k0 = pl.multiple_of(j * tq, tq)
        k = k_ref[pl.ds(k0, tq), :].astype(BF16)
        t = _nt(k, q) * QK_LOG2 - jnp.concatenate([ck_ref[pl.ds(k0, tq), :]] * (tq // LANES), axis=1)
        if diagonal:
            t = t + _mask_bias(_iota((tq, tq), 0) <= _iota((tq, tq), 1))
        return _online_update_t(t, *carry, vt_ref[:, pl.ds(k0, tq)], coff=cq2)

    carry = lax.fori_loop(0, qi, lambda j, c: step(j, c, False), _flash_init_t(tq))
    _, l, acc = step(qi, carry, True)
    o_ref[...] = _flash_finish(l, acc).T.astype(o_ref.dtype)


def _fox_prompt_attn(z, c, ct, B, T, *, tq=1024):
    tq = min(tq, T)
    assert T % tq == 0 and T % 512 == 0
    nq = T // tq
    H = N_HEADS
    return pl.pallas_call(
        functools.partial(_fox_prompt_kernel, tq=tq, T=T), grid=(B, H, nq),
        in_specs=[pl.BlockSpec((tq, HEAD_DIM), lambda b, h, i: (b * nq + i, h)),
                  pl.BlockSpec((T, HEAD_DIM), lambda b, h, i: (b, H + h)),
                  pl.BlockSpec((T, HEAD_DIM), lambda b, h, i: (b, 2 * H + h)),
                  pl.BlockSpec((T, LANES), lambda b, h, i: (b, 0)),
                  pl.BlockSpec((None, H, T), lambda b, h, i: (b, 0, 0))],
        out_specs=pl.BlockSpec((tq, HEAD_DIM), lambda b, h, i: (b * nq + i, h)),
        out_shape=jax.ShapeDtypeStruct((B * T, H * HEAD_DIM), BF16),
        scratch_shapes=[pltpu.VMEM((HEAD_DIM, T), BF16), pltpu.VMEM((T, LANES), F32)], name="fox_prompt",
        compiler_params=_cparams("parallel", "parallel", "arbitrary"))(z, z, z, c, ct)


def _moba_prompt_kernel(q_ref, k_ref, v_ref, o_ref, km_ref, vt_ref, *, tq, T):
    qi = pl.program_id(2)
    q0 = qi * tq
    nb = T // MOBA_BLOCK
    nsub = tq // MOBA_BLOCK
    nbr = km_ref.shape[0]

    @pl.when(qi == 0)
    def _():
        km_ref[...] = jnp.zeros(km_ref.shape, F32)
        for n in range(nb):
            km_ref[n:n + 1, :] = jnp.mean(k_ref[n * MOBA_BLOCK:(n + 1) * MOBA_BLOCK, :], axis=0, keepdims=True)
        _transpose_into(vt_ref, v_ref, T, BF16)

    q = q_ref[...].astype(BF16)
    blk = _iota((nbr, tq), 0)
    own = (q0 + _iota((nbr, tq), 1)) // MOBA_BLOCK
    gate_t = jnp.where(blk < own, _nt(km_ref[...].astype(BF16), q), NEG_INF)
    sel_bias_t = _mask_bias(_top_k_mask(gate_t, MOBA_TOP, axis=0, finite_only=True) > 0.5)

    def tile(j, diagonal):
        k0 = pl.multiple_of(j * tq, tq)
        t = _nt(k_ref[pl.ds(k0, tq), :].astype(BF16), q) * QK_LOG2
        parts = []
        for r in range(nsub):
            b = jnp.sum(jnp.where(blk == j * nsub + r, sel_bias_t, 0.0), axis=0, keepdims=True)
            b = jnp.broadcast_to(b, (MOBA_BLOCK, tq))
            if diagonal:
                kk = r * MOBA_BLOCK + _iota((MOBA_BLOCK, tq), 0)
                qq = _iota((MOBA_BLOCK, tq), 1)
                b = jnp.where(qq // MOBA_BLOCK == r, _mask_bias(kk <= qq), b)
            parts.append(b)
        return t + jnp.concatenate(parts, axis=0), vt_ref[:, pl.ds(k0, tq)]

    t, vt = tile(qi, True)
    carry = _online_update_t(t, *_flash_init_t(tq), vt)

    def body(j, carry):
        t, vt = tile(j, False)
        return _online_update_t(t, *carry, vt)

    _, l, acc = lax.fori_loop(0, qi, body, carry)
    o_ref[...] = _flash_finish(l, acc).T.astype(o_ref.dtype)


def _moba_prompt_attn(z, B, T, *, tq=1024):
    tq = min(tq, T)
    assert T % tq == 0 and tq % MOBA_BLOCK == 0
    nq = T // tq
    H = N_HEADS
    return pl.pallas_call(
        functools.partial(_moba_prompt_kernel, tq=tq, T=T), grid=(B, H, nq),
        in_specs=[pl.BlockSpec((tq, HEAD_DIM), lambda b, h, i: (b * nq + i, h)),
                  pl.BlockSpec((T, HEAD_DIM), lambda b, h, i: (b, H + h)),
                  pl.BlockSpec((T, HEAD_DIM), lambda b, h, i: (b, 2 * H + h))],
        out_specs=pl.BlockSpec((tq, HEAD_DIM), lambda b, h, i: (b * nq + i, h)),
        out_shape=jax.ShapeDtypeStruct((B * T, H * HEAD_DIM), BF16),
        scratch_shapes=[pltpu.VMEM((-(-(T // MOBA_BLOCK) // 8) * 8, HEAD_DIM), F32),
                        pltpu.VMEM((HEAD_DIM, T), BF16)], name="moba_prompt",
        compiler_params=_cparams("parallel", "parallel", "arbitrary"))(z, z, z)


ROWS_PER_KV = 2 * NSA_KV_HEADS
CHUNKS_PER_PAGE = PAGE // CMP_STRIDE


def _compress_kernel(pt_ref, *refs, ppg, npp):
    del pt_ref
    x_refs = refs[:npp]
    w1_ref, w2_ref, pe_ref, o_ref, xcat_ref, carry_ref = refs[npp:]
    s = pl.program_id(1)
    slot = s % (ppg // npp)
    V = ROWS_PER_KV
    for r in range(npp):
        base = pl.multiple_of((slot * npp + r) * (CHUNKS_PER_PAGE * V), CHUNKS_PER_PAGE * V)
        for n in range(CHUNKS_PER_PAGE):
            for i in range(CMP_STRIDE):
                row = (n * CMP_STRIDE + i) * V
                xcat_ref[pl.ds(base + n * V, V), i * HEAD_DIM:(i + 1) * HEAD_DIM] = x_refs[r][row:row + V, :]

    @pl.when(s == 0)
    def _():
        carry_ref[...] = jnp.zeros(carry_ref.shape, F32)

    @pl.when(slot == ppg // npp - 1)
    def _():
        rows = ppg * CHUNKS_PER_PAGE * V
        H = HEAD_DIM
        is_v = (_iota((rows, H), 0) % V) >= NSA_KV_HEADS
        pick = lambda y: jnp.where(is_v, y[:, y.shape[1] // 2:y.shape[1] // 2 + H], y[:, :H])
        w1 = w1_ref[...]
        pb = _dot(pe_ref[...].astype(BF16), w1)
        bias = jnp.where(is_v, pb[2:3, 2 * H:3 * H] + pb[3:4, 3 * H:], pb[0:1, :H] + pb[1:2, H:2 * H])
        part = _dot(xcat_ref[...].astype(BF16), w1)
        p0 = pick(part)
        p1 = jnp.where(is_v, part[:, 3 * H:], part[:, H:2 * H])
        p0_prev = jnp.concatenate([carry_ref[...], p0[:rows - V]], axis=0)
        pre = bias + p0_prev + p1
        o_ref[...] = pick(_dot((pre * jax.nn.sigmoid(pre)).astype(BF16), w2_ref[...]))
        carry_ref[...] = p0[rows - V:]


def _compress(rows, page_table, w1, w2, pe):
    B, P = page_table.shape
    ppg = min(32, P)
    npp = min(4, ppg)
    assert P % ppg == 0 and ppg % npp == 0
    half = CMP_LEN * HEAD_DIM // 2
    w1cat = jnp.concatenate([w1[0, :half], w1[0, half:], w1[1, :half], w1[1, half:]], axis=1).astype(BF16)
    w2cat = jnp.concatenate([w2[0], w2[1]], axis=1).astype(BF16)
    pe4 = jnp.concatenate([pe.reshape(4, half), jnp.zeros((4, half), F32)], axis=0)
    grp = ppg * CHUNKS_PER_PAGE * ROWS_PER_KV
    page = [pl.BlockSpec((None, PAGE * ROWS_PER_KV, HEAD_DIM), lambda b, s, pt, r=r: (pt[b, s * npp + r], 0, 0))
            for r in range(npp)]
    const = lambda shape: pl.BlockSpec(shape, lambda b, s, pt: (0,) * len(shape))
    gs = pltpu.PrefetchScalarGridSpec(
        num_scalar_prefetch=1, grid=(B, P // npp),
        in_specs=page + [const(w1cat.shape), const(w2cat.shape), const(pe4.shape)],
        out_specs=pl.BlockSpec((None, grp, HEAD_DIM), lambda b, s, pt: (b, s // (ppg // npp), 0)),
        scratch_shapes=[pltpu.VMEM((grp, CMP_STRIDE * HEAD_DIM), F32), pltpu.VMEM((ROWS_PER_KV, HEAD_DIM), F32)])
    out = pl.pallas_call(
        functools.partial(_compress_kernel, ppg=ppg, npp=npp), grid_spec=gs,
        out_shape=jax.ShapeDtypeStruct((B, P * CHUNKS_PER_PAGE * ROWS_PER_KV, HEAD_DIM), F32), name="nsa_compress",
        compiler_params=_cparams("parallel", "arbitrary"))(page_table, *([rows] * npp), w1cat, w2cat, pe4)
    return out.reshape(B, P * CHUNKS_PER_PAGE, ROWS_PER_KV, HEAD_DIM)


def _overlap_shifted(nm, nb, width):
    nc = nm - 1
    m = np.zeros((nm, width), np.float32)
    j = np.arange(nb)
    for a in range(SLC_LEN // CMP_STRIDE):
        for b in range(CMP_LEN // CMP_STRIDE):
            i = (SLC_LEN // CMP_STRIDE) * j + a - b
            ok = (i >= 0) & (i < nc)
            np.add.at(m, (i[ok] + 1, j[ok]), 1.0)
    return jnp.asarray(m, BF16)


def _slc_scores(imp, qpos_blk, lane):
    lag = qpos_blk - lane
    valid = lag >= 0
    forced = (lane == 0) | (valid & (lag < SLC_LOCAL))
    return jnp.where(valid, jnp.where(forced, POS_INF, imp), NEG_INF)


def _nsa_prompt_kernel(q_ref, kc_ref, vc_ref, ks_ref, vs_ref, kw_ref, vw_ref, gt_ref, ovl_ref, et_ref, o_ref,
                       vst_ref, vwt_ref, *, tq, tk, nm, T):
    G = NSA_GROUP
    hkv = pl.program_id(1)
    qi = pl.program_id(2)
    q0 = pl.multiple_of(qi * tq, tq)

    @pl.when(qi == 0)
    def _():
        _transpose_into(vst_ref, vs_ref, T, BF16)
        _transpose_into(vwt_ref, vw_ref, T, BF16)

    qs = jnp.concatenate([q_ref[:, g * HEAD_DIM:(g + 1) * HEAD_DIM] for g in range(G)], axis=0).astype(BF16)

    def rep(x):
        return jnp.concatenate([x] * G, axis=1)

    s = _nt(kc_ref[...].astype(BF16), qs) * SCALE
    m_i = _iota((nm, tq), 0)
    c_ok = jnp.where((m_i >= 1) & ((m_i - 1) * CMP_STRIDE + (CMP_LEN - 1) <= q0 + _iota((nm, tq), 1)), 1.0, 0.0)
    p_cmp = _masked_softmax(s, rep(c_ok) > 0.5, axis=0).astype(BF16)
    o_cmp = _dot(vc_ref[...].T.astype(BF16), p_cmp)
    pov = _dot(ovl_ref[...], p_cmp)
    imp = pov[:, 0:tq]
    for g in range(1, G):
        imp = imp + pov[:, g * tq:(g + 1) * tq]
    blk = _iota(imp.shape, 0)
    qblk = (q0 + _iota(imp.shape, 1)) // SLC_LEN
    sel_b = _top_k_mask(_slc_scores(imp, qblk, blk), SLC_TOP, axis=0).astype(BF16)

    krow = _iota((tk, tq), 0)
    qpos = q0 + _iota((tk, tq), 1)

    def slc_body(j, carry):
        k0 = pl.multiple_of(j * tk, tk)
        k = ks_ref[pl.ds(k0, tk), :].astype(BF16)
        sel_e = _dot(et_ref[pl.ds(k0, tk), :], sel_b)
        bias = jnp.where((k0 + krow) <= qpos, (sel_e - 1.0) * -NEG, NEG)
        return _online_update_t(_nt(k, qs) * QK_LOG2 + rep(bias), *carry, vst_ref[:, pl.ds(k0, tk)])

    _, l, acc = lax.fori_loop(0, (q0 + tq + tk - 1) // tk, slc_body, _flash_init_t(G * tq))
    o_slc = _flash_finish(l, acc)

    nw = WINDOW + tq
    w0 = pl.multiple_of(jnp.maximum(q0 - WINDOW, 0), tq)
    d = (q0 + _iota((nw, tq), 1)) - (w0 + _iota((nw, tq), 0))
    t = _nt(kw_ref[pl.ds(w0, nw), :].astype(BF16), qs) * QK_LOG2 + rep(_mask_bias((d >= 0) & (d < WINDOW)))
    _, l, acc = _online_update_t(t, *_flash_init_t(G * tq), vwt_ref[:, pl.ds(w0, nw)])
    o_win = _flash_finish(l, acc)

    for g in range(G):
        col = (hkv * G + g) * N_BRANCH
        c = slice(g * tq, (g + 1) * tq)
        o = (o_cmp[:, c] * gt_ref[pl.ds(col, 1), :] + o_slc[:, c] * gt_ref[pl.ds(col + 1, 1), :]
             + o_win[:, c] * gt_ref[pl.ds(col + 2, 1), :])
        o_ref[:, g * HEAD_DIM:(g + 1) * HEAD_DIM] = o.T.astype(o_ref.dtype)


def _nsa_prompt_attn(z, comp, gates_t, B, T, *, tq=256, tk=512):
    tk = min(tk, T)
    assert T % tk == 0 and WINDOW % tq == 0 and T >= WINDOW + tq and T % 512 == 0
    nq = T // tq
    nm = comp.shape[2]
    G = NSA_GROUP
    qw = G * HEAD_DIM
    kvb = N_HEADS
    nb = T // SLC_LEN
    nbr = -(-nb // 8) * 8
    ovl = _overlap_shifted(nm, nb, nbr).T
    et = jnp.asarray((np.arange(T)[:, None] // SLC_LEN) == np.arange(nbr)[None, :], BF16)

    def kv_spec(off):
        return pl.BlockSpec((T, HEAD_DIM), lambda b, h, i: (b, kvb + off + h))

    return pl.pallas_call(
        functools.partial(_nsa_prompt_kernel, tq=tq, tk=tk, nm=nm, T=T), grid=(B, NSA_KV_HEADS, nq),
        in_specs=[pl.BlockSpec((tq, qw), lambda b, h, i: (b * nq + i, h)),
                  pl.BlockSpec((None, None, nm, HEAD_DIM), lambda b, h, i: (b, h, 0, 0)),
                  pl.BlockSpec((None, None, nm, HEAD_DIM), lambda b, h, i: (b, NSA_KV_HEADS + h, 0, 0)),
                  kv_spec(8), kv_spec(12), kv_spec(16), kv_spec(20),
                  pl.BlockSpec((LANES, tq), lambda b, h, i: (0, b * nq + i)),
                  pl.BlockSpec((nbr, nm), lambda b, h, i: (0, 0)),
                  pl.BlockSpec((T, nbr), lambda b, h, i: (0, 0))],
        out_specs=pl.BlockSpec((tq, qw), lambda b, h, i: (b * nq + i, h)),
        out_shape=jax.ShapeDtypeStruct((B * T, N_HEADS * HEAD_DIM), BF16),
        scratch_shapes=[pltpu.VMEM((HEAD_DIM, T), BF16), pltpu.VMEM((HEAD_DIM, T), BF16)], name="nsa_prompt",
        compiler_params=_cparams("parallel", "parallel", "arbitrary"))(z, comp, comp, z, z, z, z, gates_t, ovl, et)


NSA_Q = N_HEADS * HEAD_DIM
NSA_KV = NSA_KV_HEADS * HEAD_DIM
NSA_MAIN = NSA_Q + 2 * N_BRANCH * NSA_KV
HD = N_HEADS * HEAD_DIM


def _pad_cols(w, width=LANES):
    return jnp.pad(w, ((0, 0), (0, width - w.shape[1])))


def _nsa_rope_flags(tn=512):
    per = NSA_KV // tn
    flags = [1] * (NSA_Q // tn)
    for _ in range(N_BRANCH):
        flags += [1] * per + [0] * per
    return jnp.asarray(flags, I32)


def _nsa_project(x2d, g, w_in, pos_tables, tm):
    z = _norm_mm(x2d, g, w_in, NSA_MAIN, tm=tm, epi="rope", rope=(_nsa_rope_flags(),) + pos_tables)
    gates = _norm_mm(x2d, g, _pad_cols(w_in[:, NSA_MAIN:]), LANES, tm=tm, epi="sigmoid")
    return z, gates


def _nsa_prompt_mixer(x2d, g, w_in, pe, w1, w2, B, T, tm):
    z, gates = _nsa_project(x2d, g, w_in, _rope_tables(jnp.arange(T, dtype=I32)), tm)
    kv = [z[:, NSA_Q + 2 * NSA_KV * br:NSA_Q + 2 * NSA_KV * (br + 1)] for br in range(N_BRANCH)]
    rows = kv[0].reshape(B * T // PAGE, PAGE * ROWS_PER_KV, HEAD_DIM)
    pt = jnp.arange(B * T // PAGE, dtype=I32).reshape(B, T // PAGE)
    comp = _compress(rows, pt, w1, w2, pe).transpose(0, 2, 1, 3)
    o = _nsa_prompt_attn(z, comp, gates.T, B, T)
    st = [a.reshape(B, T, 2, NSA_KV_HEADS, HEAD_DIM) for a in kv]
    return o, (st[0], st[1], st[2][:, T - min(WINDOW, T):])


def _moba_rope_flags(tn=512):
    return jnp.asarray([1] * (2 * HD // tn) + [0] * (HD // tn), I32)


def _moba_prompt_mixer(x2d, g, w_in, B, T, tm):
    z = _norm_mm(x2d, g, w_in, 3 * HD, tm=tm, epi="rope",
                 rope=(_moba_rope_flags(),) + _rope_tables(jnp.arange(T, dtype=I32)))
    o = _moba_prompt_attn(z, B, T)
    return o, (z[:, HD:].reshape(B, T, 2, N_HEADS, HEAD_DIM),)


def _fox_project(x2d, g, w_in, b_f, tm):
    z = _norm_mm(x2d, g, w_in, 3 * HD, tm=tm)
    logf = _norm_mm(x2d, g, _pad_cols(w_in[:, 3 * HD:]), LANES, tm=tm, epi="logsig",
                    bias=jnp.pad(b_f, (0, LANES - N_HEADS)))
    return z, logf


def _fox_prompt_mixer(x2d, g, w_in, b_f, B, T, tm):
    z, logf = _fox_project(x2d, g, w_in, b_f, tm)
    log_f = logf[:, :N_HEADS].reshape(B, T, N_HEADS)
    ct = _cumsum_rows(log_f.transpose(0, 2, 1))
    c = _pad_cols(ct.transpose(0, 2, 1).reshape(B * T, N_HEADS))
    o = _fox_prompt_attn(z, c, ct, B, T)
    return o, (z[:, HD:].reshape(B, T, 2, N_HEADS, HEAD_DIM), log_f)


def _ffn(x2d, g, wa, wb, wo, tm):
    u = _swiglu_up(x2d, g, wa, wb, tm=tm)
    return _mm_res(u, wo, x2d, 0.5, tm=tm, tn=FFN_DOWN_TN)


def _diag_blocks(acc, nblk):
    rblk = _iota((acc.shape[0], HEAD_DIM), 0) % nblk
    out = jnp.zeros((acc.shape[0], HEAD_DIM), F32)
    for j in range(nblk):
        out = out + jnp.where(rblk == j, acc[:, j * HEAD_DIM:(j + 1) * HEAD_DIM], 0.0)
    return out


def _row_query(rows, mode):
    r = _iota((rows, LANES), 0)
    return (r % N_HEADS) // NSA_KV_HEADS if mode == "nsa" else r // N_HEADS


def _paged_attn_kernel(pt_ref, q_ref, hm_ref, hmn_ref, *rest, mode, nh, npp, n_steps):
    del pt_ref
    joint = mode == "nsa"
    if joint:
        xk_refs = xv_refs = rest[:npp]
        kn_ref = vn_ref = rest[npp]
        rest = rest[npp + 1:]
    else:
        xk_refs, xv_refs = rest[:npp], rest[npp:2 * npp]
        kn_ref, vn_ref = rest[2 * npp:2 * npp + 2]
        rest = rest[2 * npp + 2:]
    if mode == "fox":
        cq_ref, ck_ref, ckn_ref, o_ref, m_ref, l_ref, acc_ref = rest
    else:
        sel_ref, o_ref, m_ref, l_ref, acc_ref = rest
    p = pl.program_id(1)
    R = q_ref.shape[0]
    q = q_ref[...]
    roff = cq_ref[:, 0:1] * LOG2E if mode == "fox" else None
    pv_roll = nh if joint else 0

    def rows_of(refs):
        xs = [r[...].reshape(-1, HEAD_DIM).astype(BF16) for r in refs]
        return xs[0] if len(xs) == 1 else jnp.concatenate(xs, axis=0)

    def tiles(k_refs, v_refs):
        k = rows_of(k_refs)
        return k, (k if joint else rows_of(v_refs))

    def sel_bias(blk):
        if isinstance(blk, int):
            tile = sel_ref[:, (blk // LANES) * LANES:(blk // LANES + 1) * LANES]
        else:
            tile = sel_ref[:, pl.ds(pl.multiple_of((blk // LANES) * LANES, LANES), LANES)]
        return _mask_bias(_lane_col(tile, blk % LANES) > 0.5), _mask_bias(_lane_col(tile, blk % LANES + 1) > 0.5)

    def update(t, v, row_off):
        m, l, acc = _online_update(t, m_ref[...], l_ref[...], acc_ref[...], v, roff=row_off, pv_roll=pv_roll)
        m_ref[...] = m
        l_ref[...] = l
        acc_ref[...] = acc

    @pl.when(p == 0)
    def _():
        m_ref[...] = jnp.full(m_ref.shape, NEG, F32)
        l_ref[...] = jnp.zeros(l_ref.shape, F32)
        acc_ref[...] = jnp.zeros(acc_ref.shape, F32)
        kn, vn = tiles((kn_ref,), (vn_ref,))
        tn = _nt(q, kn) * QK_LOG2 + hmn_ref[...]
        if mode == "fox":
            tn = tn - ckn_ref[...] * LOG2E
        elif mode == "nsa":
            tn = tn + sel_bias(n_steps * npp * (PAGE // SLC_LEN))[0]
        update(tn, vn, roff)

    k, v = tiles(xk_refs, xv_refs)
    t = _nt(q, k) * QK_LOG2 + hm_ref[...]
    if mode == "fox":
        update(t - ck_ref[...] * LOG2E, v, roff)
    elif mode == "moba":
        b0, b1 = sel_bias(p // 2 * 2)
        update(t, v, jnp.where(p % 2 == 0, b0, b1))
    else:
        parts = []
        for j in range(0, npp * PAGE // SLC_LEN, 2):
            b0, b1 = sel_bias(p * (npp * PAGE // SLC_LEN) + j)
            width = SLC_LEN * 2 * nh
            parts += [jnp.broadcast_to(b0, (R, width)), jnp.broadcast_to(b1, (R, width))]
        update(t + jnp.concatenate(parts, axis=1), v, None)

    @pl.when(p == n_steps - 1)
    def _():
        o_ref[...] = _flash_finish(l_ref[...], acc_ref[...]).astype(o_ref.dtype)


def _head_mask(rows, cols, vecs, nh, causal_q=None):
    r = np.arange(rows)[:, None]
    c = np.arange(cols)[None, :]
    ok = (c % vecs) == (r % nh)
    if causal_q is not None:
        ok &= (c // vecs) <= causal_q(r)
    return jnp.asarray(np.where(ok, 0.0, NEG), F32)


def _paged_attn(q_rows, pools, page_table, new_rows, extras, *, mode, out_dtype):
    B, R, _ = q_rows.shape
    P = page_table.shape[1]
    joint = mode == "nsa"
    nh = NSA_KV_HEADS if joint else N_HEADS
    vecs = 2 * nh if joint else nh
    n_new = new_rows[0].shape[1]
    q_of_row = (lambda r: (r % N_HEADS) // NSA_KV_HEADS) if joint else (lambda r: r // N_HEADS)
    npp = 4 if joint else MOBA_BLOCK // PAGE
    assert P % npp == 0
    hm = _head_mask(R, npp * PAGE * vecs, vecs, nh)
    hmn = _head_mask(R, n_new, vecs, nh, causal_q=q_of_row)
    const = lambda shape: pl.BlockSpec(shape, lambda b, p, pt: (0,) * len(shape))
    per_b = lambda shape: pl.BlockSpec((None,) + shape, lambda b, p, pt: (b,) + (0,) * len(shape))
    in_specs = [per_b((R, HEAD_DIM)), const(hm.shape), const(hmn.shape)]
    if joint:
        in_specs += [pl.BlockSpec((None, PAGE * vecs, HEAD_DIM), lambda b, p, pt, r=r: (pt[b, p * npp + r], 0, 0))
                     for r in range(npp)]
        in_specs += [per_b((n_new, HEAD_DIM))]
        args = (pools[0],) * npp + (new_rows[0],)
    else:
        for c in range(2):
            in_specs += [pl.BlockSpec((None, PAGE, None, nh, HEAD_DIM),
                                      lambda b, p, pt, r=r, c=c: (pt[b, p * npp + r], 0, c, 0, 0)) for r in range(npp)]
        in_specs += [per_b((n_new, HEAD_DIM)), per_b((n_new, HEAD_DIM))]
        args = (pools[0],) * (2 * npp) + (new_rows[0], new_rows[1])
    if mode == "fox":
        cq, ck = extras
        in_specs += [per_b((R, LANES)),
                     pl.BlockSpec((None, 1, npp * PAGE * nh), lambda b, p, pt: (b, 0, p)),
                     pl.BlockSpec((None, 1, n_new), lambda b, p, pt: (b, 0, P * PAGE * nh // n_new))]
        args += (cq, ck, ck)
    else:
        (sel,) = extras
        in_specs += [per_b((R, sel.shape[2]))]
        args += (sel,)
    gs = pltpu.PrefetchScalarGridSpec(
        num_scalar_prefetch=1, grid=(B, P // npp), in_specs=in_specs,
        out_specs=pl.BlockSpec((None, R, HEAD_DIM), lambda b, p, pt: (b, 0, 0)),
        scratch_shapes=[pltpu.VMEM((R, 1), F32), pltpu.VMEM((R, 1), F32), pltpu.VMEM((R, HEAD_DIM), F32)])
    return pl.pallas_call(
        functools.partial(_paged_attn_kernel, mode=mode, nh=nh, npp=npp, n_steps=P // npp), grid_spec=gs,
        out_shape=jax.ShapeDtypeStruct((B, R, HEAD_DIM), out_dtype), name="paged_attn_" + mode,
        compiler_params=_cparams("parallel", "arbitrary"))(page_table, q_rows, hm, hmn, *args)


PAGES_PER_STEP = 8


def _cumsum_paged_kernel(pt_ref, *refs, n_groups):
    del pt_ref
    x_refs = refs[:PAGES_PER_STEP]
    xn_ref, o_ref, carry_ref = refs[PAGES_PER_STEP:]
    g = pl.program_id(1)

    @pl.when(g == 0)
    def _():
        carry_ref[...] = jnp.zeros(carry_ref.shape, F32)

    carry = carry_ref[...]
    for r in range(PAGES_PER_STEP):
        tail = xn_ref[...] if r == 0 else jnp.zeros(xn_ref.shape, F32)
        cs = _tile_cumsum(jnp.where(g == n_groups, tail, x_refs[r][...])) + carry
        o_ref[:, r * PAGE:(r + 1) * PAGE] = cs
        carry = jnp.broadcast_to(cs[:, LANES - 1:LANES], carry.shape)
    carry_ref[...] = carry


def _cumsum_paged(pool_t, page_table, new_t):
    B, P = page_table.shape
    H = pool_t.shape[1]
    n = PAGES_PER_STEP
    assert P % n == 0
    ng = P // n
    page = [pl.BlockSpec((None, H, PAGE), lambda b, g, pt, r=r: (pt[b, jnp.minimum(g, ng - 1) * n + r], 0, 0))
            for r in range(n)]
    gs = pltpu.PrefetchScalarGridSpec(
        num_scalar_prefetch=1, grid=(B, ng + 1),
        in_specs=page + [pl.BlockSpec((None, H, PAGE), lambda b, g, pt: (b, 0, 0))],
        out_specs=pl.BlockSpec((None, H, n * PAGE), lambda b, g, pt: (b, 0, g)),
        scratch_shapes=[pltpu.VMEM((H, LANES), F32)])
    return pl.pallas_call(
        functools.partial(_cumsum_paged_kernel, n_groups=ng), grid_spec=gs,
        out_shape=jax.ShapeDtypeStruct((B, H, (ng + 1) * n * PAGE), F32), name="cumsum_paged",
        compiler_params=_cparams("parallel", "arbitrary"))(page_table, *([pool_t] * n), new_t)


def _kmeans_kernel(pt_ref, *refs, ppb):
    del pt_ref
    x_refs, o_ref = refs[:PAGES_PER_STEP], refs[PAGES_PER_STEP]
    for blk in range(PAGES_PER_STEP // ppb):
        tot = jnp.sum(x_refs[blk * ppb][...], axis=0)
        for r in range(1, ppb):
            tot = tot + jnp.sum(x_refs[blk * ppb + r][...], axis=0)
        o_ref[blk] = tot * (1.0 / MOBA_BLOCK)


def _moba_kmeans(pool5, page_table):
    B, P = page_table.shape
    ppb = MOBA_BLOCK // PAGE
    n = PAGES_PER_STEP
    assert P % n == 0 and n % ppb == 0
    page = [pl.BlockSpec((None, PAGE, None, N_HEADS, HEAD_DIM), lambda b, g, pt, r=r: (pt[b, g * n + r], 0, 0, 0, 0))
            for r in range(n)]
    gs = pltpu.PrefetchScalarGridSpec(
        num_scalar_prefetch=1, grid=(B, P // n), in_specs=page,
        out_specs=pl.BlockSpec((None, n // ppb, N_HEADS, HEAD_DIM), lambda b, g, pt: (b, g, 0, 0)))
    return pl.pallas_call(
        functools.partial(_kmeans_kernel, ppb=ppb), grid_spec=gs,
        out_shape=jax.ShapeDtypeStruct((B, P // ppb, N_HEADS, HEAD_DIM), F32), name="moba_kmeans",
        compiler_params=_cparams("parallel", "parallel"))(page_table, *([pool5] * n))


def _moba_select_kernel(q_ref, km_ref, o_ref, *, n_past_blocks):
    gate = _nt(q_ref[...], km_ref[...].astype(BF16))
    lane = _iota(gate.shape, 1)
    gate = jnp.where(lane < n_past_blocks, gate, NEG_INF)
    o_ref[...] = _top_k_mask(gate, MOBA_TOP, finite_only=True)


def _moba_select(qbd, kmeans, n_past_blocks):
    B, R, C = qbd.shape
    return pl.pallas_call(
        functools.partial(_moba_select_kernel, n_past_blocks=n_past_blocks), grid=(B,),
        in_specs=[pl.BlockSpec((None, R, C), lambda b: (b, 0, 0)), pl.BlockSpec((None, LANES, C), lambda b: (b, 0, 0))],
        out_specs=pl.BlockSpec((None, R, LANES), lambda b: (b, 0, 0)),
        out_shape=jax.ShapeDtypeStruct((B, R, LANES), F32), name="moba_select",
        compiler_params=_cparams("parallel"))(qbd, kmeans)


def _nsa_cmp_sample_kernel(q_ref, kc_ref, vc_ref, ovl_ref, o_ref, sel_ref, *, past):
    R = q_ref.shape[0]
    nm = kc_ref.shape[0]
    G = NSA_GROUP
    s = _nt(q_ref[...], kc_ref[...].astype(BF16)) * SCALE
    m_i = _iota((R, nm), 1)
    qpos = past + (_iota((R, nm), 0) % N_HEADS) // NSA_KV_HEADS
    ok = (m_i >= 1) & ((m_i - 1) * CMP_STRIDE + (CMP_LEN - 1) <= qpos)
    p_cmp = _masked_softmax(s, ok).astype(BF16)
    o_ref[...] = _diag_blocks(_dot(p_cmp, vc_ref[...].astype(BF16)), NSA_KV_HEADS)
    pov = _dot(p_cmp, ovl_ref[...])
    rg = R // G
    imp = pov[0:rg]
    for g in range(1, G):
        imp = imp + pov[g * rg:(g + 1) * rg]
    lane = _iota(imp.shape, 1)
    qblk = (past + _iota(imp.shape, 0) // NSA_KV_HEADS) // SLC_LEN
    sel = _top_k_mask(_slc_scores(imp, qblk, lane), SLC_TOP)
    sel_ref[...] = jnp.concatenate([sel] * G, axis=0)


def _nsa_cmp_sample(qbd, comp, past):
    B, R, C = qbd.shape
    nm = comp.shape[1]
    nb = -(-(past + R // N_HEADS) // SLC_LEN)
    width = -(-nb // LANES) * LANES
    ovl = _overlap_shifted(nm, nb, width)
    return pl.pallas_call(
        functools.partial(_nsa_cmp_sample_kernel, past=past), grid=(B,),
        in_specs=[pl.BlockSpec((None, R, C), lambda b: (b, 0, 0)),
                  pl.BlockSpec((None, nm, C), lambda b: (b, 0, 0)),
                  pl.BlockSpec((None, nm, C), lambda b: (b, 0, 1)),
                  pl.BlockSpec((nm, width), lambda b: (0, 0))],
        out_specs=[pl.BlockSpec((None, R, HEAD_DIM), lambda b: (b, 0, 0)),
                   pl.BlockSpec((None, R, width), lambda b: (b, 0, 0))],
        out_shape=[jax.ShapeDtypeStruct((B, R, HEAD_DIM), F32), jax.ShapeDtypeStruct((B, R, width), F32)],
        name="nsa_cmp_sample", compiler_params=_cparams("parallel"))(qbd, comp, comp, ovl)


def _nsa_win_sample_kernel(q_ref, kw_ref, vw_ref, oc_ref, os_ref, g_ref, o_ref, *, past, wb):
    R = q_ref.shape[0]
    nk = kw_ref.shape[0]
    s = _nt(q_ref[...], kw_ref[...].astype(BF16)) * SCALE
    j = _iota((R, nk), 1)
    qpos = past + (_iota((R, nk), 0) % N_HEADS) // NSA_KV_HEADS
    wpos = past - wb + j
    d = qpos - wpos
    ok = (d >= 0) & (d < WINDOW) & (wpos >= 0)
    p = _masked_softmax(s, ok).astype(BF16)
    o_win = _diag_blocks(_dot(p, vw_ref[...].astype(BF16)), NSA_KV_HEADS)
    g = g_ref[...]
    o = oc_ref[...] * g[:, 0:1] + os_ref[...] * g[:, 1:2] + o_win * g[:, 2:3]
    o_ref[...] = o.astype(o_ref.dtype)


def _nsa_win_sample(qbd, kw, o_cmp, o_slc, gates, past, wb):
    B, R, C = qbd.shape
    nk = kw.shape[1]
    row = pl.BlockSpec((None, R, HEAD_DIM), lambda b: (b, 0, 0))
    return pl.pallas_call(
        functools.partial(_nsa_win_sample_kernel, past=past, wb=wb), grid=(B,),
        in_specs=[pl.BlockSpec((None, R, C), lambda b: (b, 0, 0)),
                  pl.BlockSpec((None, nk, C), lambda b: (b, 0, 0)),
                  pl.BlockSpec((None, nk, C), lambda b: (b, 0, 1)),
                  row, row, row],
        out_specs=row, out_shape=jax.ShapeDtypeStruct((B, R, HEAD_DIM), BF16), name="nsa_win_sample",
        compiler_params=_cparams("parallel"))(qbd, kw, kw, o_cmp, o_slc, gates)


def _qbd_heads(q):
    B, Q, H, dh = q.shape
    x = q[:, :, :, None, :] * jnp.eye(H, dtype=q.dtype)[None, None, :, :, None]
    return x.reshape(B, Q * H, H * dh).astype(BF16)


def _qbd_groups(q):
    B, Q, H, dh = q.shape
    x = q.reshape(B, Q, NSA_KV_HEADS, NSA_GROUP, dh).transpose(0, 3, 1, 2, 4)
    x = x[:, :, :, :, None, :] * jnp.eye(NSA_KV_HEADS, dtype=q.dtype)[None, None, None, :, :, None]
    return x.reshape(B, NSA_GROUP * Q * NSA_KV_HEADS, NSA_KV_HEADS * dh).astype(BF16)


def _pad_rows(x, rows):
    return jnp.pad(x, ((0, 0), (0, rows - x.shape[1]), (0, 0)))


def _new_kv_rows(new):
    B, Q, _, H, dh = new.shape
    return (_pad_rows(new[:, :, 0].reshape(B, Q * H, dh), LANES), _pad_rows(new[:, :, 1].reshape(B, Q * H, dh), LANES))


def _sample_tables(B, Q, past):
    return _rope_tables(jnp.tile(past + jnp.arange(Q, dtype=I32), B))


def _nsa_sample_mixer(x2d, g, w_in, pe, w1, w2, pool_cmp, pool_slc, win_buf, page_table, layer, B, Q):
    P = page_table.shape[1]
    past = P * PAGE
    assert past % SLC_LEN == 0 and Q * N_HEADS % 8 == 0
    n_pool = pool_cmp.shape[1]
    pt = page_table + layer * n_pool
    z, gates = _nsa_project(x2d, g, w_in, _sample_tables(B, Q, past), B * Q)
    kv = [z[:, NSA_Q + 2 * NSA_KV * br:NSA_Q + 2 * NSA_KV * (br + 1)].reshape(B, Q, 2 * NSA_KV) for br in range(N_BRANCH)]
    assert (past + Q - CMP_LEN) // CMP_STRIDE + 1 == past // CMP_STRIDE - 1
    comp = _compress(pool_cmp.reshape(-1, PAGE * ROWS_PER_KV, HEAD_DIM), pt, w1, w2, pe)
    comp = comp.reshape(B, -1, ROWS_PER_KV * HEAD_DIM)
    q4 = z[:, :NSA_Q].reshape(B, Q, N_HEADS, HEAD_DIM)
    qbd = _qbd_groups(q4)
    o_cmp, sel = _nsa_cmp_sample(qbd, comp, past)
    q_rows = q4.reshape(B, Q, NSA_KV_HEADS, NSA_GROUP, HEAD_DIM).transpose(0, 3, 1, 2, 4)
    q_rows = q_rows.reshape(B, Q * N_HEADS, HEAD_DIM).astype(BF16)
    x_new = _pad_rows(kv[1].reshape(B, Q * ROWS_PER_KV, HEAD_DIM), LANES)
    o_slc = _paged_attn(q_rows, (pool_slc.reshape(-1, PAGE * ROWS_PER_KV, HEAD_DIM),), pt, (x_new,), (sel,),
                        mode="nsa", out_dtype=F32)
    wb = win_buf.shape[2]
    kw = jnp.concatenate([win_buf[layer].reshape(B, wb, 2 * NSA_KV), kv[2]], axis=1)
    g_rows = gates[:, :N_HEADS * N_BRANCH].reshape(B, Q, NSA_KV_HEADS, NSA_GROUP, N_BRANCH)
    g_rows = jnp.pad(g_rows.transpose(0, 3, 1, 2, 4).reshape(B, Q * N_HEADS, N_BRANCH), ((0, 0), (0, 0), (0, LANES - N_BRANCH)))
    o = _nsa_win_sample(qbd, _pad_rows(kw, -(-(wb + Q) // LANES) * LANES), o_cmp, o_slc, g_rows, past, wb)
    o = o.reshape(B, NSA_GROUP, Q, NSA_KV_HEADS, HEAD_DIM).transpose(0, 2, 3, 1, 4).reshape(B * Q, HD)
    shp = (B, -1, 2, NSA_KV_HEADS, HEAD_DIM)
    return o, (kv[0].reshape(shp), kv[1].reshape(shp), kw[:, Q:].reshape(shp))


def _moba_sample_mixer(x2d, g, w_in, pool, page_table, layer, B, Q):
    P = page_table.shape[1]
    past = P * PAGE
    assert past % MOBA_BLOCK == 0 and Q <= MOBA_BLOCK and past // MOBA_BLOCK <= LANES
    n_pool = pool.shape[1]
    pt = page_table + layer * n_pool
    z = _norm_mm(x2d, g, w_in, 3 * HD, tm=B * Q, epi="rope", rope=(_moba_rope_flags(),) + _sample_tables(B, Q, past))
    q4 = z[:, :HD].reshape(B, Q, N_HEADS, HEAD_DIM)
    nbp = past // MOBA_BLOCK
    pool5 = pool.reshape(-1, PAGE, 2, N_HEADS, HEAD_DIM)
    km = _moba_kmeans(pool5, pt).reshape(B, nbp, HD)
    sel = _moba_select(_qbd_heads(q4), _pad_rows(km, LANES), nbp)
    new = z[:, HD:].reshape(B, Q, 2, N_HEADS, HEAD_DIM)
    o = _paged_attn(q4.reshape(B, Q * N_HEADS, HEAD_DIM).astype(BF16), (pool5,), pt, _new_kv_rows(new), (sel,),
                    mode="moba", out_dtype=BF16)
    return o.reshape(B * Q, HD), (new,)


def _fox_sample_mixer(x2d, g, w_in, b_f, pool, pool_logf, page_table, layer, B, Q):
    P = page_table.shape[1]
    n_pool = pool.shape[1]
    pt = page_table + layer * n_pool
    z, logf = _fox_project(x2d, g, w_in, b_f, B * Q)
    log_f = logf[:, :N_HEADS].reshape(B, Q, N_HEADS)
    pool_t = pool_logf.astype(F32).transpose(0, 1, 3, 2).reshape(-1, N_HEADS, PAGE)
    new_t = jnp.pad(log_f.transpose(0, 2, 1), ((0, 0), (0, 0), (0, PAGE - Q)))
    ct = _cumsum_paged(pool_t, pt, new_t)
    cq = ct[:, :, P * PAGE:P * PAGE + Q].transpose(0, 2, 1).reshape(B, Q * N_HEADS, 1)
    ck = ct.transpose(0, 2, 1).reshape(B, 1, -1)
    q_rows = z[:, :HD].reshape(B, Q * N_HEADS, HEAD_DIM).astype(BF16)
    new = z[:, HD:].reshape(B, Q, 2, N_HEADS, HEAD_DIM)
    o = _paged_attn(q_rows, (pool.reshape(-1, PAGE, 2, N_HEADS, HEAD_DIM),), pt, _new_kv_rows(new),
                    (jnp.broadcast_to(cq, (B, Q * N_HEADS, LANES)), ck), mode="fox", out_dtype=BF16)
    return o.reshape(B * Q, HD), (new, log_f)


PROMPT_TM = 1024
FFN_DOWN_TN = 256


def kernel(x_prompt, x_sample, cache_nsa_cmp, cache_nsa_slc, cache_nsa_win, cache_moba_kv, cache_fox_kv,
           cache_fox_logf, page_table, norms, ffn_wi, ffn_wo, final_norm, nsa_w_in, nsa_cmp_pe, nsa_cmp_w1,
           nsa_cmp_w2, nsa_w_out, moba_w_in, moba_w_out, fox_w_in, fox_b_f, fox_w_out):
    B, T, D = x_prompt.shape
    Bs, Q, _ = x_sample.shape
    depth = norms.shape[0]
    xp = x_prompt.reshape(B * T, D)
    xs = x_sample.reshape(Bs * Q, D)
    tp = min(PROMPT_TM, B * T)
    ts = Bs * Q
    ffn = ffn_wi.shape[-1] // 2
    ffn_wa, ffn_wb, ffn_wo = ffn_wi[..., :ffn].astype(BF16), ffn_wi[..., ffn:].astype(BF16), ffn_wo.astype(BF16)
    nsa_w_in, moba_w_in, fox_w_in = nsa_w_in.astype(BF16), moba_w_in.astype(BF16), fox_w_in.astype(BF16)
    nsa_w_out, moba_w_out, fox_w_out = nsa_w_out.astype(BF16), moba_w_out.astype(BF16), fox_w_out.astype(BF16)
    st_p = {0: [], 1: [], 2: []}
    st_s = {0: [], 1: [], 2: []}
    for i in range(depth):
        kind, j = i % 3, i // 3
        xp = _ffn(xp, norms[i, 0], ffn_wa[i, 0], ffn_wb[i, 0], ffn_wo[i, 0], tp)
        xs = _ffn(xs, norms[i, 0], ffn_wa[i, 0], ffn_wb[i, 0], ffn_wo[i, 0], ts)
        g = norms[i, 1]
        if kind == 0:
            op, sp = _nsa_prompt_mixer(xp, g, nsa_w_in[j], nsa_cmp_pe[j], nsa_cmp_w1[j], nsa_cmp_w2[j], B, T, tp)
            os_, ss = _nsa_sample_mixer(xs, g, nsa_w_in[j], nsa_cmp_pe[j], nsa_cmp_w1[j], nsa_cmp_w2[j],
                                        cache_nsa_cmp, cache_nsa_slc, cache_nsa_win, page_table, j, Bs, Q)
            w_out = nsa_w_out[j]
        elif kind == 1:
            op, sp = _moba_prompt_mixer(xp, g, moba_w_in[j], B, T, tp)
            os_, ss = _moba_sample_mixer(xs, g, moba_w_in[j], cache_moba_kv, page_table, j, Bs, Q)
            w_out = moba_w_out[j]
        else:
            op, sp = _fox_prompt_mixer(xp, g, fox_w_in[j], fox_b_f[j], B, T, tp)
            os_, ss = _fox_sample_mixer(xs, g, fox_w_in[j], fox_b_f[j], cache_fox_kv, cache_fox_logf, page_table, j, Bs, Q)
            w_out = fox_w_out[j]
        st_p[kind].append(sp)
        st_s[kind].append(ss)
        xp = _mm_res(op, w_out, xp, 1.0, tm=tp, tn=512)
        xs = _mm_res(os_, w_out, xs, 1.0, tm=ts, tn=512)
        xp = _ffn(xp, norms[i, 2], ffn_wa[i, 1], ffn_wb[i, 1], ffn_wo[i, 1], tp)
        xs = _ffn(xs, norms[i, 2], ffn_wa[i, 1], ffn_wb[i, 1], ffn_wo[i, 1], ts)
    y_prompt = _rmsnorm(xp, final_norm, tm=tp).reshape(B, T, D)
    y_sample = _rmsnorm(xs, final_norm, tm=ts).reshape(Bs, Q, D)

    def stack(states, k):
        return jnp.stack([s[k] for s in states])

    return (y_prompt, y_sample,
            stack(st_p[0], 0), stack(st_p[0], 1), stack(st_p[0], 2),
            stack(st_s[0], 0), stack(st_s[0], 1), stack(st_s[0], 2),
            stack(st_p[1], 0), stack(st_s[1], 0),
            stack(st_p[2], 0), stack(st_p[2], 1), stack(st_s[2], 0), stack(st_s[2], 1))
```

```python
import functools
from typing import NamedTuple

import numpy as np
import jax
import jax.numpy as jnp
from jax import lax
from jax.experimental import pallas as pl
from jax.experimental.pallas import tpu as pltpu

F32 = jnp.float32
BF16 = jnp.bfloat16
I32 = jnp.int32

LANES = 128
VMEM_LIMIT_BYTES = 56 << 20

N_HEADS = 16
HEAD_DIM = 128
ROT_DIM = HEAD_DIM // 4
ROPE_THETA = 500000.0
NORM_EPS = 1e-6
PAGE = 128
NSA_KV_HEADS = 4
NSA_GROUP = N_HEADS // NSA_KV_HEADS
CMP_LEN = 32
CMP_STRIDE = 16
SLC_LEN = 64
SLC_TOP = 16
SLC_LOCAL = 2
WINDOW = 512
N_BRANCH = 3
MOBA_BLOCK = 256
MOBA_TOP = 3
SCALE = HEAD_DIM ** -0.5
LOG2E = 1.4426950408889634
QK_LOG2 = SCALE * LOG2E
ROW_CHUNK = 64
NEG = -1e30
NEG_INF = float("-inf")
POS_INF = float("inf")


def _cparams(*sem):
    return pltpu.CompilerParams(dimension_semantics=sem, vmem_limit_bytes=VMEM_LIMIT_BYTES)


def _nt(a, b):
    return lax.dot_general(a, b, (((1,), (1,)), ((), ())), preferred_element_type=F32)


def _dot(a, b):
    return jnp.dot(a, b, preferred_element_type=F32)


def _iota(shape, dim):
    return lax.broadcasted_iota(I32, shape, dim)


def _lane_col(x, idx):
    return jnp.sum(jnp.where(_iota(x.shape, 1) == idx, x, 0.0), axis=-1, keepdims=True)


class _W(NamedTuple):
    a: jax.Array
    lead: tuple = ()


def _w_parts(w):
    return (w.a, tuple(w.lead)) if isinstance(w, _W) else (w, ())


def _w_cols(w, start):
    a, lead = _w_parts(w)
    return a[lead + (slice(None), slice(start, None))]


def _w_spec(lead, rows, tn, col_of=lambda j: j):
    return pl.BlockSpec((None,) * len(lead) + (rows, tn), lambda i, j, *_: tuple(lead) + (0, col_of(j)))


def _rms_to_bf16(x_ref, g_ref):
    x = x_ref[...]
    var = jnp.mean(x * x, axis=-1, keepdims=True)
    return ((x * lax.rsqrt(var + NORM_EPS)) * g_ref[...]).astype(BF16)


def _log_sigmoid(x):
    return jnp.minimum(x, 0.0) - jnp.log1p(jnp.exp(-jnp.abs(x)))


def _norm_mm_kernel(*refs, epi, tn):
    if epi == "rope":
        flags_ref, x_ref, g_ref, w_ref, c_ref, s1_ref, s2_ref, o_ref, xn_ref = refs
    elif epi == "logsig":
        x_ref, g_ref, w_ref, b_ref, o_ref, xn_ref = refs
    else:
        x_ref, g_ref, w_ref, o_ref, xn_ref = refs
    j = pl.program_id(1)

    @pl.when(j == 0)
    def _():
        xn_ref[...] = _rms_to_bf16(x_ref, g_ref)

    z = _dot(xn_ref[...], w_ref[...].astype(BF16))
    if epi == "rope":
        @pl.when(flags_ref[j] == 1)
        def _():
            c, s1, s2 = c_ref[...], s1_ref[...], s2_ref[...]
            for hh in range(tn // HEAD_DIM):
                zs = z[:, hh * HEAD_DIM:(hh + 1) * HEAD_DIM]
                o_ref[:, hh * HEAD_DIM:(hh + 1) * HEAD_DIM] = (
                    zs * c + pltpu.roll(zs, HEAD_DIM - ROT_DIM // 2, 1) * s1 + pltpu.roll(zs, ROT_DIM // 2, 1) * s2)

        @pl.when(flags_ref[j] == 0)
        def _():
            o_ref[...] = z
    elif epi == "sigmoid":
        o_ref[...] = jax.nn.sigmoid(z)
    elif epi == "logsig":
        o_ref[...] = _log_sigmoid(z + b_ref[...])
    else:
        o_ref[...] = z


def _norm_mm(x, g, w, n_out, *, tm, tn=512, epi="none", rope=None, bias=None):
    M, D = x.shape
    w, lead = _w_parts(w)
    tn = min(tn, n_out)
    assert M % tm == 0 and n_out % tn == 0
    grid = (M // tm, n_out // tn)
    g2 = g.reshape(1, D)
    kern = functools.partial(_norm_mm_kernel, epi=epi, tn=tn)
    scratch = [pltpu.VMEM((tm, D), BF16)]
    out_shape = jax.ShapeDtypeStruct((M, n_out), F32)
    if epi == "rope":
        flags, c, s1, s2 = rope
        nt = c.shape[0] // tm
        tab = pl.BlockSpec((tm, HEAD_DIM), lambda i, j, f: (i % nt, 0))
        gs = pltpu.PrefetchScalarGridSpec(
            num_scalar_prefetch=1, grid=grid,
            in_specs=[pl.BlockSpec((tm, D), lambda i, j, f: (i, 0)),
                      pl.BlockSpec((1, D), lambda i, j, f: (0, 0)),
                      _w_spec(lead, D, tn),
                      tab, tab, tab],
            out_specs=pl.BlockSpec((tm, tn), lambda i, j, f: (i, j)),
            scratch_shapes=scratch)
        return pl.pallas_call(kern, grid_spec=gs, out_shape=out_shape, name="norm_mm_rope",
                              compiler_params=_cparams("parallel", "arbitrary"))(flags, x, g2, w, c, s1, s2)
    in_specs = [pl.BlockSpec((tm, D), lambda i, j: (i, 0)),
                pl.BlockSpec((1, D), lambda i, j: (0, 0)),
                _w_spec(lead, D, tn)]
    args = [x, g2, w]
    if epi == "logsig":
        in_specs.append(pl.BlockSpec((1, tn), lambda i, j: (0, j)))
        args.append(bias.reshape(1, n_out))
    return pl.pallas_call(kern, grid=grid, in_specs=in_specs,
                          out_specs=pl.BlockSpec((tm, tn), lambda i, j: (i, j)),
                          out_shape=out_shape, scratch_shapes=scratch, name="norm_mm_" + epi,
                          compiler_params=_cparams("parallel", "arbitrary"))(*args)


def _swiglu_up_kernel(x_ref, g_ref, wa_ref, *rest, nsub):
    wb_refs = rest[:nsub]
    o_ref, xn_ref, wb_scr = rest[nsub:]

    @pl.when(pl.program_id(1) == 0)
    def _():
        xn_ref[...] = _rms_to_bf16(x_ref, g_ref)

    for r in range(nsub):
        wb_scr[:, r * LANES:(r + 1) * LANES] = wb_refs[r][...]
    xn = xn_ref[...]
    a = _dot(xn, wa_ref[...])
    b = _dot(xn, wb_scr[...])
    o_ref[...] = (a * jax.nn.sigmoid(a) * b).astype(o_ref.dtype)


def _swiglu_up(x, g, wi, *, tm, tn=512):
    M, D = x.shape
    wi, lead = _w_parts(wi)
    F = wi.shape[-1] // 2
    assert F % LANES == 0 and M % tm == 0
    nsub = tn // LANES
    off = F // LANES
    last = wi.shape[-1] // LANES - 1
    in_specs = [pl.BlockSpec((tm, D), lambda i, j: (i, 0)), pl.BlockSpec((1, D), lambda i, j: (0, 0)),
                _w_spec(lead, D, tn)]
    in_specs += [_w_spec(lead, D, LANES, lambda j, r=r: jnp.minimum(off + nsub * j + r, last)) for r in range(nsub)]
    return pl.pallas_call(
        functools.partial(_swiglu_up_kernel, nsub=nsub),
        grid=(M // tm, pl.cdiv(F, tn)), in_specs=in_specs,
        out_specs=pl.BlockSpec((tm, tn), lambda i, j: (i, j)),
        out_shape=jax.ShapeDtypeStruct((M, F), BF16),
        scratch_shapes=[pltpu.VMEM((tm, D), BF16), pltpu.VMEM((D, tn), BF16)],
        name="swiglu_up",
        compiler_params=_cparams("parallel", "arbitrary"))(x, g.reshape(1, D), wi, *([wi] * nsub))


def _mm_res_kernel(u_ref, w_ref, r_ref, o_ref, *, scale):
    y = _dot(u_ref[...].astype(BF16), w_ref[...].astype(BF16))
    o_ref[...] = r_ref[...] + (y if scale == 1.0 else scale * y)


def _mm_res(u, w, res, scale, *, tm, tn):
    M, K = u.shape
    w, lead = _w_parts(w)
    N = w.shape[-1]
    assert M % tm == 0 and N % tn == 0
    return pl.pallas_call(
        functools.partial(_mm_res_kernel, scale=scale),
        grid=(M // tm, N // tn),
        in_specs=[pl.BlockSpec((tm, K), lambda i, j: (i, 0)),
                  _w_spec(lead, K, tn),
                  pl.BlockSpec((tm, tn), lambda i, j: (i, j))],
        out_specs=pl.BlockSpec((tm, tn), lambda i, j: (i, j)),
        out_shape=jax.ShapeDtypeStruct((M, N), F32), name="mm_res",
        compiler_params=_cparams("parallel", "arbitrary"))(u, w, res)


def _rmsnorm_kernel(x_ref, g_ref, o_ref):
    x = x_ref[...]
    var = jnp.mean(x * x, axis=-1, keepdims=True)
    o_ref[...] = (x * lax.rsqrt(var + NORM_EPS)) * g_ref[...]


def _rmsnorm(x, g, *, tm):
    M, D = x.shape
    return pl.pallas_call(
        _rmsnorm_kernel, grid=(M // tm,),
        in_specs=[pl.BlockSpec((tm, D), lambda i: (i, 0)), pl.BlockSpec((1, D), lambda i: (0, 0))],
        out_specs=pl.BlockSpec((tm, D), lambda i: (i, 0)),
        out_shape=jax.ShapeDtypeStruct((M, D), F32), name="rmsnorm",
        compiler_params=_cparams("parallel"))(x, g.reshape(1, D))


def _rope_tables(pos):
    half = ROT_DIM // 2
    inv = ROPE_THETA ** (-jnp.arange(half, dtype=F32) / half)
    ang = pos.astype(F32)[:, None] * inv[None, :]
    cos, sin = jnp.cos(ang), jnp.sin(ang)
    n = pos.shape[0]
    zeros = jnp.zeros((n, HEAD_DIM - ROT_DIM), F32)
    zh = jnp.zeros((n, half), F32)
    c = jnp.concatenate([cos, cos, jnp.ones((n, HEAD_DIM - ROT_DIM), F32)], axis=1)
    s1 = jnp.concatenate([-sin, zh, zeros], axis=1)
    s2 = jnp.concatenate([zh, sin, zeros], axis=1)
    return c, s1, s2


def _online_update(t, m, l, acc, v, roff=None, pv_roll=0):
    ms, ls, als, ps = [], [], [], []
    for r0 in range(0, t.shape[0], ROW_CHUNK):
        r = slice(r0, r0 + ROW_CHUNK)
        tmax = jnp.max(t[r], axis=-1, keepdims=True)
        if roff is not None:
            tmax = tmax + roff[r]
        m_new = jnp.maximum(m[r], tmax)
        alpha = jnp.exp2(m[r] - m_new)
        p = jnp.exp2(t[r] + ((roff[r] - m_new) if roff is not None else -m_new))
        ms.append(m_new)
        als.append(alpha)
        ls.append(alpha * l[r] + jnp.sum(p, axis=-1, keepdims=True))
        ps.append((pltpu.roll(p, pv_roll, 1) if pv_roll else p).astype(BF16))
    cat = lambda xs: xs[0] if len(xs) == 1 else jnp.concatenate(xs, axis=0)
    return cat(ms), cat(ls), cat(als) * acc + _dot(cat(ps), v)


def _online_update_t(t, m, l, acc, vt, coff=None):
    tmax = jnp.max(t, axis=0, keepdims=True)
    if coff is not None:
        tmax = tmax + coff
    m_new = jnp.maximum(m, tmax)
    alpha = jnp.exp2(m - m_new)
    p = jnp.exp2(t + ((coff - m_new) if coff is not None else -m_new))
    l_new = alpha * l + jnp.sum(p, axis=0, keepdims=True)
    return m_new, l_new, alpha * acc + _dot(vt, p.astype(BF16))


def _flash_init_t(n_q, width=HEAD_DIM):
    return (jnp.full((1, n_q), NEG, F32), jnp.zeros((1, n_q), F32), jnp.zeros((width, n_q), F32))


def _transpose_into(dst_ref, src_ref, n_rows, dtype, chunk=512):
    for r0 in range(0, n_rows, chunk):
        r1 = min(r0 + chunk, n_rows)
        dst_ref[:, r0:r1] = src_ref[r0:r1, :].T.astype(dtype)


def _mask_bias(ok):
    return jnp.where(ok, 0.0, NEG)


def _flash_init(rows, width=HEAD_DIM):
    return (jnp.full((rows, 1), NEG, F32), jnp.zeros((rows, 1), F32), jnp.zeros((rows, width), F32))


def _flash_finish(l, acc):
    return acc / jnp.where(l > 0.0, l, 1.0)


def _masked_softmax(s, ok, axis=-1):
    sm = jnp.where(ok, s, NEG)
    mx = jnp.max(sm, axis=axis, keepdims=True)
    e = jnp.where(ok, jnp.exp(sm - mx), 0.0)
    d = jnp.sum(e, axis=axis, keepdims=True)
    return e / jnp.where(d > 0.0, d, 1.0)


def _top_k_mask(score, k, axis=-1, finite_only=False):
    idx = _iota(score.shape, axis % score.ndim)
    n = score.shape[axis]
    sel = jnp.zeros(score.shape, F32)
    for _ in range(k):
        mx = jnp.max(score, axis=axis, keepdims=True)
        first = jnp.min(jnp.where(score == mx, idx, n), axis=axis, keepdims=True)
        pick = idx == first
        sel = jnp.where((pick & (mx > NEG_INF)) if finite_only else pick, 1.0, sel)
        score = jnp.where(pick, NEG_INF, score)
    return sel


def _tile_cumsum(x):
    lane = _iota(x.shape, 1)
    for s in (1, 2, 4, 8, 16, 32, 64):
        x = x + jnp.where(lane >= s, pltpu.roll(x, s, 1), 0.0)
    return x


def _cumsum_rows_kernel(x_ref, o_ref, *, n_tiles):
    carry = jnp.zeros((x_ref.shape[0], 1), F32)
    for j in range(n_tiles):
        cs = _tile_cumsum(x_ref[:, j * LANES:(j + 1) * LANES]) + carry
        o_ref[:, j * LANES:(j + 1) * LANES] = cs
        carry = cs[:, LANES - 1:LANES]


def _cumsum_rows(xt):
    B, H, T = xt.shape
    return pl.pallas_call(
        functools.partial(_cumsum_rows_kernel, n_tiles=T // LANES), grid=(B,),
        in_specs=[pl.BlockSpec((None, H, T), lambda b: (b, 0, 0))],
        out_specs=pl.BlockSpec((None, H, T), lambda b: (b, 0, 0)),
        out_shape=jax.ShapeDtypeStruct((B, H, T), F32), name="cumsum_rows",
        compiler_params=_cparams("parallel"))(xt)


def _fox_prompt_kernel(q_ref, k_ref, v_ref, c_ref, ct_ref, o_ref, vt_ref, ck_ref, *, tq, T):
    h = pl.program_id(1)
    qi = pl.program_id(2)

    @pl.when(qi == 0)
    def _():
        _transpose_into(vt_ref, v_ref, T, BF16)
        for r0 in range(0, T, 512):
            col = _lane_col(c_ref[r0:r0 + 512, :], h) * LOG2E
            ck_ref[r0:r0 + 512, :] = jnp.broadcast_to(col, (512, LANES))

    q = q_ref[...].astype(BF16)
    q0 = pl.multiple_of(qi * tq, tq)
    cq2 = ct_ref[pl.ds(h, 1), pl.ds(q0, tq)] * LOG2E

    def step(j, carry, diagonal):
        k0 = pl.multiple_of(j * tq, tq)
        k = k_ref[pl.ds(k0, tq), :].astype(BF16)
        t = _nt(k, q) * QK_LOG2 - jnp.concatenate([ck_ref[pl.ds(k0, tq), :]] * (tq // LANES), axis=1)
        if diagonal:
            t = t + _mask_bias(_iota((tq, tq), 0) <= _iota((tq, tq), 1))
        return _online_update_t(t, *carry, vt_ref[:, pl.ds(k0, tq)], coff=cq2)

    carry = lax.fori_loop(0, qi, lambda j, c: step(j, c, False), _flash_init_t(tq))
    _, l, acc = step(qi, carry, True)
    o_ref[...] = _flash_finish(l, acc).T.astype(o_ref.dtype)


def _fox_prompt_attn(z, c, ct, B, T, *, tq=1024):
    tq = min(tq, T)
    assert T % tq == 0 and T % 512 == 0
    nq = T // tq
    H = N_HEADS
    return pl.pallas_call(
        functools.partial(_fox_prompt_kernel, tq=tq, T=T), grid=(B, H, nq),
        in_specs=[pl.BlockSpec((tq, HEAD_DIM), lambda b, h, i: (b * nq + i, h)),
                  pl.BlockSpec((T, HEAD_DIM), lambda b, h, i: (b, H + h)),
                  pl.BlockSpec((T, HEAD_DIM), lambda b, h, i: (b, 2 * H + h)),
                  pl.BlockSpec((T, LANES), lambda b, h, i: (b, 0)),
                  pl.BlockSpec((None, H, T), lambda b, h, i: (b, 0, 0))],
        out_specs=pl.BlockSpec((tq, HEAD_DIM), lambda b, h, i: (b * nq + i, h)),
        out_shape=jax.ShapeDtypeStruct((B * T, H * HEAD_DIM), BF16),
        scratch_shapes=[pltpu.VMEM((HEAD_DIM, T), BF16), pltpu.VMEM((T, LANES), F32)], name="fox_prompt",
        compiler_params=_cparams("parallel", "parallel", "arbitrary"))(z, z, z, c, ct)


def _moba_prompt_kernel(q_ref, k_ref, v_ref, o_ref, km_ref, vt_ref, *, tq, T):
    qi = pl.program_id(2)
    q0 = qi * tq
    nb = T // MOBA_BLOCK
    nsub = tq // MOBA_BLOCK
    nbr = km_ref.shape[0]

    @pl.when(qi == 0)
    def _():
        km_ref[...] = jnp.zeros(km_ref.shape, F32)
        for n in range(nb):
            km_ref[n:n + 1, :] = jnp.mean(k_ref[n * MOBA_BLOCK:(n + 1) * MOBA_BLOCK, :], axis=0, keepdims=True)
        _transpose_into(vt_ref, v_ref, T, BF16)

    q = q_ref[...].astype(BF16)
    blk = _iota((nbr, tq), 0)
    own = (q0 + _iota((nbr, tq), 1)) // MOBA_BLOCK
    gate_t = jnp.where(blk < own, _nt(km_ref[...].astype(BF16), q), NEG_INF)
    sel_bias_t = _mask_bias(_top_k_mask(gate_t, MOBA_TOP, axis=0, finite_only=True) > 0.5)

    def tile(j, diagonal):
        k0 = pl.multiple_of(j * tq, tq)
        t = _nt(k_ref[pl.ds(k0, tq), :].astype(BF16), q) * QK_LOG2
        parts = []
        for r in range(nsub):
            b = jnp.sum(jnp.where(blk == j * nsub + r, sel_bias_t, 0.0), axis=0, keepdims=True)
            b = jnp.broadcast_to(b, (MOBA_BLOCK, tq))
            if diagonal:
                kk = r * MOBA_BLOCK + _iota((MOBA_BLOCK, tq), 0)
                qq = _iota((MOBA_BLOCK, tq), 1)
                b = jnp.where(qq // MOBA_BLOCK == r, _mask_bias(kk <= qq), b)
            parts.append(b)
        return t + jnp.concatenate(parts, axis=0), vt_ref[:, pl.ds(k0, tq)]

    t, vt = tile(qi, True)
    carry = _online_update_t(t, *_flash_init_t(tq), vt)

    def body(j, carry):
        t, vt = tile(j, False)
        return _online_update_t(t, *carry, vt)

    _, l, acc = lax.fori_loop(0, qi, body, carry)
    o_ref[...] = _flash_finish(l, acc).T.astype(o_ref.dtype)


def _moba_prompt_attn(z, B, T, *, tq=1024):
    tq = min(tq, T)
    assert T % tq == 0 and tq % MOBA_BLOCK == 0
    nq = T // tq
    H = N_HEADS
    return pl.pallas_call(
        functools.partial(_moba_prompt_kernel, tq=tq, T=T), grid=(B, H, nq),
        in_specs=[pl.BlockSpec((tq, HEAD_DIM), lambda b, h, i: (b * nq + i, h)),
                  pl.BlockSpec((T, HEAD_DIM), lambda b, h, i: (b, H + h)),
                  pl.BlockSpec((T, HEAD_DIM), lambda b, h, i: (b, 2 * H + h))],
        out_specs=pl.BlockSpec((tq, HEAD_DIM), lambda b, h, i: (b * nq + i, h)),
        out_shape=jax.ShapeDtypeStruct((B * T, H * HEAD_DIM), BF16),
        scratch_shapes=[pltpu.VMEM((-(-(T // MOBA_BLOCK) // 8) * 8, HEAD_DIM), F32),
                        pltpu.VMEM((HEAD_DIM, T), BF16)], name="moba_prompt",
        compiler_params=_cparams("parallel", "parallel", "arbitrary"))(z, z, z)


ROWS_PER_KV = 2 * NSA_KV_HEADS
CHUNKS_PER_PAGE = PAGE // CMP_STRIDE


def _compress_kernel(pt_ref, *refs, ppg, npp):
    del pt_ref
    x_refs = refs[:npp]
    w1_ref, w2_ref, pe_ref, o_ref, xcat_ref, carry_ref = refs[npp:]
    s = pl.program_id(1)
    slot = s % (ppg // npp)
    V = ROWS_PER_KV
    for r in range(npp):
        base = pl.multiple_of((slot * npp + r) * (CHUNKS_PER_PAGE * V), CHUNKS_PER_PAGE * V)
        for n in range(CHUNKS_PER_PAGE):
            for i in range(CMP_STRIDE):
                row = (n * CMP_STRIDE + i) * V
                xcat_ref[pl.ds(base + n * V, V), i * HEAD_DIM:(i + 1) * HEAD_DIM] = x_refs[r][row:row + V, :]

    @pl.when(s == 0)
    def _():
        carry_ref[...] = jnp.zeros(carry_ref.shape, F32)

    @pl.when(slot == ppg // npp - 1)
    def _():
        rows = ppg * CHUNKS_PER_PAGE * V
        H = HEAD_DIM
        is_v = (_iota((rows, H), 0) % V) >= NSA_KV_HEADS
        pick = lambda y: jnp.where(is_v, y[:, y.shape[1] // 2:y.shape[1] // 2 + H], y[:, :H])
        w1 = w1_ref[...]
        pb = _dot(pe_ref[...].astype(BF16), w1)
        bias = jnp.where(is_v, pb[2:3, 2 * H:3 * H] + pb[3:4, 3 * H:], pb[0:1, :H] + pb[1:2, H:2 * H])
        part = _dot(xcat_ref[...].astype(BF16), w1)
        p0 = pick(part)
        p1 = jnp.where(is_v, part[:, 3 * H:], part[:, H:2 * H])
        p0_prev = jnp.concatenate([carry_ref[...], p0[:rows - V]], axis=0)
        pre = bias + p0_prev + p1
        o_ref[...] = pick(_dot((pre * jax.nn.sigmoid(pre)).astype(BF16), w2_ref[...]))
        carry_ref[...] = p0[rows - V:]


def _compress(rows, page_table, w1, w2, pe):
    B, P = page_table.shape
    ppg = min(32, P)
    npp = min(4, ppg)
    assert P % ppg == 0 and ppg % npp == 0
    half = CMP_LEN * HEAD_DIM // 2
    w1cat = jnp.concatenate([w1[0, :half], w1[0, half:], w1[1, :half], w1[1, half:]], axis=1).astype(BF16)
    w2cat = jnp.concatenate([w2[0], w2[1]], axis=1).astype(BF16)
    pe4 = jnp.concatenate([pe.reshape(4, half), jnp.zeros((4, half), F32)], axis=0)
    grp = ppg * CHUNKS_PER_PAGE * ROWS_PER_KV
    page = [pl.BlockSpec((None, PAGE * ROWS_PER_KV, HEAD_DIM), lambda b, s, pt, r=r: (pt[b, s * npp + r], 0, 0))
            for r in range(npp)]
    const = lambda shape: pl.BlockSpec(shape, lambda b, s, pt: (0,) * len(shape))
    gs = pltpu.PrefetchScalarGridSpec(
        num_scalar_prefetch=1, grid=(B, P // npp),
        in_specs=page + [const(w1cat.shape), const(w2cat.shape), const(pe4.shape)],
        out_specs=pl.BlockSpec((None, grp, HEAD_DIM), lambda b, s, pt: (b, s // (ppg // npp), 0)),
        scratch_shapes=[pltpu.VMEM((grp, CMP_STRIDE * HEAD_DIM), F32), pltpu.VMEM((ROWS_PER_KV, HEAD_DIM), F32)])
    out = pl.pallas_call(
        functools.partial(_compress_kernel, ppg=ppg, npp=npp), grid_spec=gs,
        out_shape=jax.ShapeDtypeStruct((B, P * CHUNKS_PER_PAGE * ROWS_PER_KV, HEAD_DIM), F32), name="nsa_compress",
        compiler_params=_cparams("parallel", "arbitrary"))(page_table, *([rows] * npp), w1cat, w2cat, pe4)
    return out.reshape(B, P * CHUNKS_PER_PAGE, ROWS_PER_KV, HEAD_DIM)


def _overlap_shifted(nm, nb, width):
    nc = nm - 1
    m = np.zeros((nm, width), np.float32)
    j = np.arange(nb)
    for a in range(SLC_LEN // CMP_STRIDE):
        for b in range(CMP_LEN // CMP_STRIDE):
            i = (SLC_LEN // CMP_STRIDE) * j + a - b
            ok = (i >= 0) & (i < nc)
            np.add.at(m, (i[ok] + 1, j[ok]), 1.0)
    return jnp.asarray(m, BF16)


def _slc_scores(imp, qpos_blk, lane):
    lag = qpos_blk - lane
    valid = lag >= 0
    forced = (lane == 0) | (valid & (lag < SLC_LOCAL))
    return jnp.where(valid, jnp.where(forced, POS_INF, imp), NEG_INF)


def _nsa_prompt_kernel(q_ref, kc_ref, vc_ref, ks_ref, vs_ref, kw_ref, vw_ref, gt_ref, ovl_ref, et_ref, o_ref,
                       vst_ref, vwt_ref, *, tq, tk, nm, T):
    G = NSA_GROUP
    hkv = pl.program_id(1)
    qi = pl.program_id(2)
    q0 = pl.multiple_of(qi * tq, tq)

    @pl.when(qi == 0)
    def _():
        _transpose_into(vst_ref, vs_ref, T, BF16)
        _transpose_into(vwt_ref, vw_ref, T, BF16)

    qs = jnp.concatenate([q_ref[:, g * HEAD_DIM:(g + 1) * HEAD_DIM] for g in range(G)], axis=0).astype(BF16)

    def rep(x):
        return jnp.concatenate([x] * G, axis=1)

    s = _nt(kc_ref[...].astype(BF16), qs) * SCALE
    m_i = _iota((nm, tq), 0)
    c_ok = jnp.where((m_i >= 1) & ((m_i - 1) * CMP_STRIDE + (CMP_LEN - 1) <= q0 + _iota((nm, tq), 1)), 1.0, 0.0)
    p_cmp = _masked_softmax(s, rep(c_ok) > 0.5, axis=0).astype(BF16)
    o_cmp = _dot(vc_ref[...].T.astype(BF16), p_cmp)
    pov = _dot(ovl_ref[...], p_cmp)
    imp = pov[:, 0:tq]
    for g in range(1, G):
        imp = imp + pov[:, g * tq:(g + 1) * tq]
    blk = _iota(imp.shape, 0)
    qblk = (q0 + _iota(imp.shape, 1)) // SLC_LEN
    sel_b = _top_k_mask(_slc_scores(imp, qblk, blk), SLC_TOP, axis=0).astype(BF16)

    krow = _iota((tk, tq), 0)
    qpos = q0 + _iota((tk, tq), 1)

    def slc_body(j, carry):
        k0 = pl.multiple_of(j * tk, tk)
        k = ks_ref[pl.ds(k0, tk), :].astype(BF16)
        sel_e = _dot(et_ref[pl.ds(k0, tk), :], sel_b)
        bias = jnp.where((k0 + krow) <= qpos, (sel_e - 1.0) * -NEG, NEG)
        return _online_update_t(_nt(k, qs) * QK_LOG2 + rep(bias), *carry, vst_ref[:, pl.ds(k0, tk)])

    _, l, acc = lax.fori_loop(0, (q0 + tq + tk - 1) // tk, slc_body, _flash_init_t(G * tq))
    o_slc = _flash_finish(l, acc)

    nw = WINDOW + tq
    w0 = pl.multiple_of(jnp.maximum(q0 - WINDOW, 0), tq)
    d = (q0 + _iota((nw, tq), 1)) - (w0 + _iota((nw, tq), 0))
    t = _nt(kw_ref[pl.ds(w0, nw), :].astype(BF16), qs) * QK_LOG2 + rep(_mask_bias((d >= 0) & (d < WINDOW)))
    _, l, acc = _online_update_t(t, *_flash_init_t(G * tq), vwt_ref[:, pl.ds(w0, nw)])
    o_win = _flash_finish(l, acc)

    for g in range(G):
        col = (hkv * G + g) * N_BRANCH
        c = slice(g * tq, (g + 1) * tq)
        o = (o_cmp[:, c] * gt_ref[pl.ds(col, 1), :] + o_slc[:, c] * gt_ref[pl.ds(col + 1, 1), :]
             + o_win[:, c] * gt_ref[pl.ds(col + 2, 1), :])
        o_ref[:, g * HEAD_DIM:(g + 1) * HEAD_DIM] = o.T.astype(o_ref.dtype)


def _nsa_prompt_attn(z, comp, gates_t, B, T, *, tq=256, tk=512):
    tk = min(tk, T)
    assert T % tk == 0 and WINDOW % tq == 0 and T >= WINDOW + tq and T % 512 == 0
    nq = T // tq
    nm = comp.shape[2]
    G = NSA_GROUP
    qw = G * HEAD_DIM
    kvb = N_HEADS
    nb = T // SLC_LEN
    nbr = -(-nb // 8) * 8
    ovl = _overlap_shifted(nm, nb, nbr).T
    et = jnp.asarray((np.arange(T)[:, None] // SLC_LEN) == np.arange(nbr)[None, :], BF16)

    def kv_spec(off):
        return pl.BlockSpec((T, HEAD_DIM), lambda b, h, i: (b, kvb + off + h))

    return pl.pallas_call(
        functools.partial(_nsa_prompt_kernel, tq=tq, tk=tk, nm=nm, T=T), grid=(B, NSA_KV_HEADS, nq),
        in_specs=[pl.BlockSpec((tq, qw), lambda b, h, i: (b * nq + i, h)),
                  pl.BlockSpec((None, None, nm, HEAD_DIM), lambda b, h, i: (b, h, 0, 0)),
                  pl.BlockSpec((None, None, nm, HEAD_DIM), lambda b, h, i: (b, NSA_KV_HEADS + h, 0, 0)),
                  kv_spec(8), kv_spec(12), kv_spec(16), kv_spec(20),
                  pl.BlockSpec((LANES, tq), lambda b, h, i: (0, b * nq + i)),
                  pl.BlockSpec((nbr, nm), lambda b, h, i: (0, 0)),
                  pl.BlockSpec((T, nbr), lambda b, h, i: (0, 0))],
        out_specs=pl.BlockSpec((tq, qw), lambda b, h, i: (b * nq + i, h)),
        out_shape=jax.ShapeDtypeStruct((B * T, N_HEADS * HEAD_DIM), BF16),
        scratch_shapes=[pltpu.VMEM((HEAD_DIM, T), BF16), pltpu.VMEM((HEAD_DIM, T), BF16)], name="nsa_prompt",
        compiler_params=_cparams("parallel", "parallel", "arbitrary"))(z, comp, comp, z, z, z, z, gates_t, ovl, et)


NSA_Q = N_HEADS * HEAD_DIM
NSA_KV = NSA_KV_HEADS * HEAD_DIM
NSA_MAIN = NSA_Q + 2 * N_BRANCH * NSA_KV
HD = N_HEADS * HEAD_DIM


def _pad_cols(w, width=LANES):
    return jnp.pad(w, ((0, 0), (0, width - w.shape[1])))


def _nsa_rope_flags(tn=512):
    per = NSA_KV // tn
    flags = [1] * (NSA_Q // tn)
    for _ in range(N_BRANCH):
        flags += [1] * per + [0] * per
    return jnp.asarray(flags, I32)


def _nsa_project(x2d, g, w_in, pos_tables, tm):
    z = _norm_mm(x2d, g, w_in, NSA_MAIN, tm=tm, epi="rope", rope=(_nsa_rope_flags(),) + pos_tables)
    gates = _norm_mm(x2d, g, _pad_cols(_w_cols(w_in, NSA_MAIN)), LANES, tm=tm, epi="sigmoid")
    return z, gates


def _nsa_prompt_mixer(x2d, g, w_in, pe, w1, w2, B, T, tm):
    z, gates = _nsa_project(x2d, g, w_in, _rope_tables(jnp.arange(T, dtype=I32)), tm)
    kv = [z[:, NSA_Q + 2 * NSA_KV * br:NSA_Q + 2 * NSA_KV * (br + 1)] for br in range(N_BRANCH)]
    rows = kv[0].reshape(B * T // PAGE, PAGE * ROWS_PER_KV, HEAD_DIM)
    pt = jnp.arange(B * T // PAGE, dtype=I32).reshape(B, T // PAGE)
    comp = _compress(rows, pt, w1, w2, pe).transpose(0, 2, 1, 3)
    o = _nsa_prompt_attn(z, comp, gates.T, B, T)
    st = [a.reshape(B, T, 2, NSA_KV_HEADS, HEAD_DIM) for a in kv]
    return o, (st[0], st[1], st[2][:, T - min(WINDOW, T):])


def _moba_rope_flags(tn=512):
    return jnp.asarray([1] * (2 * HD // tn) + [0] * (HD // tn), I32)


def _moba_prompt_mixer(x2d, g, w_in, B, T, tm):
    z = _norm_mm(x2d, g, w_in, 3 * HD, tm=tm, epi="rope",
                 rope=(_moba_rope_flags(),) + _rope_tables(jnp.arange(T, dtype=I32)))
    o = _moba_prompt_attn(z, B, T)
    return o, (z[:, HD:].reshape(B, T, 2, N_HEADS, HEAD_DIM),)


def _fox_project(x2d, g, w_in, b_f, tm):
    z = _norm_mm(x2d, g, w_in, 3 * HD, tm=tm)
    logf = _norm_mm(x2d, g, _pad_cols(_w_cols(w_in, 3 * HD)), LANES, tm=tm, epi="logsig",
                    bias=jnp.pad(b_f, (0, LANES - N_HEADS)))
    return z, logf


def _fox_prompt_mixer(x2d, g, w_in, b_f, B, T, tm):
    z, logf = _fox_project(x2d, g, w_in, b_f, tm)
    log_f = logf[:, :N_HEADS].reshape(B, T, N_HEADS)
    ct = _cumsum_rows(log_f.transpose(0, 2, 1))
    c = _pad_cols(ct.transpose(0, 2, 1).reshape(B * T, N_HEADS))
    o = _fox_prompt_attn(z, c, ct, B, T)
    return o, (z[:, HD:].reshape(B, T, 2, N_HEADS, HEAD_DIM), log_f)


def _ffn(x2d, g, wi, wo, tm):
    u = _swiglu_up(x2d, g, wi, tm=tm)
    return _mm_res(u, wo, x2d, 0.5, tm=tm, tn=FFN_DOWN_TN)


def _diag_blocks(acc, nblk):
    rblk = _iota((acc.shape[0], HEAD_DIM), 0) % nblk
    out = jnp.zeros((acc.shape[0], HEAD_DIM), F32)
    for j in range(nblk):
        out = out + jnp.where(rblk == j, acc[:, j * HEAD_DIM:(j + 1) * HEAD_DIM], 0.0)
    return out


def _row_query(rows, mode):
    r = _iota((rows, LANES), 0)
    return (r % N_HEADS) // NSA_KV_HEADS if mode == "nsa" else r // N_HEADS


def _paged_attn_kernel(pt_ref, q_ref, hm_ref, hmn_ref, *rest, mode, nh, npp, n_steps):
    del pt_ref
    joint = mode == "nsa"
    if joint:
        xk_refs = xv_refs = rest[:npp]
        kn_ref = vn_ref = rest[npp]
        rest = rest[npp + 1:]
    else:
        xk_refs, xv_refs = rest[:npp], rest[npp:2 * npp]
        kn_ref, vn_ref = rest[2 * npp:2 * npp + 2]
        rest = rest[2 * npp + 2:]
    if mode == "fox":
        cq_ref, ck_ref, ckn_ref, o_ref, m_ref, l_ref, acc_ref = rest
    else:
        sel_ref, o_ref, m_ref, l_ref, acc_ref = rest
    p = pl.program_id(1)
    R = q_ref.shape[0]
    q = q_ref[...]
    roff = cq_ref[:, 0:1] * LOG2E if mode == "fox" else None
    pv_roll = nh if joint else 0

    def rows_of(refs):
        xs = [r[...].reshape(-1, HEAD_DIM).astype(BF16) for r in refs]
        return xs[0] if len(xs) == 1 else jnp.concatenate(xs, axis=0)

    def tiles(k_refs, v_refs):
        k = rows_of(k_refs)
        return k, (k if joint else rows_of(v_refs))

    def sel_bias(blk):
        if isinstance(blk, int):
            tile = sel_ref[:, (blk // LANES) * LANES:(blk // LANES + 1) * LANES]
        else:
            tile = sel_ref[:, pl.ds(pl.multiple_of((blk // LANES) * LANES, LANES), LANES)]
        return _mask_bias(_lane_col(tile, blk % LANES) > 0.5), _mask_bias(_lane_col(tile, blk % LANES + 1) > 0.5)

    def update(t, v, row_off):
        m, l, acc = _online_update(t, m_ref[...], l_ref[...], acc_ref[...], v, roff=row_off, pv_roll=pv_roll)
        m_ref[...] = m
        l_ref[...] = l
        acc_ref[...] = acc

    @pl.when(p == 0)
    def _():
        m_ref[...] = jnp.full(m_ref.shape, NEG, F32)
        l_ref[...] = jnp.zeros(l_ref.shape, F32)
        acc_ref[...] = jnp.zeros(acc_ref.shape, F32)
        kn, vn = tiles((kn_ref,), (vn_ref,))
        tn = _nt(q, kn) * QK_LOG2 + hmn_ref[...]
        if mode == "fox":
            tn = tn - ckn_ref[...] * LOG2E
        elif mode == "nsa":
            tn = tn + sel_bias(n_steps * npp * (PAGE // SLC_LEN))[0]
        update(tn, vn, roff)

    k, v = tiles(xk_refs, xv_refs)
    t = _nt(q, k) * QK_LOG2 + hm_ref[...]
    if mode == "fox":
        update(t - ck_ref[...] * LOG2E, v, roff)
    else:
        blk_len = MOBA_BLOCK if mode == "moba" else SLC_LEN
        width = blk_len * (2 * nh if joint else nh)
        parts = []
        for j in range(0, npp * PAGE // blk_len, 2):
            b0, b1 = sel_bias(p * (npp * PAGE // blk_len) + j)
            parts += [jnp.broadcast_to(b0, (R, width)), jnp.broadcast_to(b1, (R, width))]
        update(t + jnp.concatenate(parts, axis=1), v, None)

    @pl.when(p == n_steps - 1)
    def _():
        o_ref[...] = _flash_finish(l_ref[...], acc_ref[...]).astype(o_ref.dtype)


def _head_mask(rows, cols, vecs, nh, causal_q=None):
    r = np.arange(rows)[:, None]
    c = np.arange(cols)[None, :]
    ok = (c % vecs) == (r % nh)
    if causal_q is not None:
        ok &= (c // vecs) <= causal_q(r)
    return jnp.asarray(np.where(ok, 0.0, NEG), F32)


def _paged_attn(q_rows, pools, page_table, new_rows, extras, *, mode, out_dtype):
    B, R, _ = q_rows.shape
    P = page_table.shape[1]
    joint = mode == "nsa"
    nh = NSA_KV_HEADS if joint else N_HEADS
    vecs = 2 * nh if joint else nh
    n_new = new_rows[0].shape[1]
    q_of_row = (lambda r: (r % N_HEADS) // NSA_KV_HEADS) if joint else (lambda r: r // N_HEADS)
    npp = min(4, P)
    assert P % npp == 0 and (npp * PAGE) % (2 * MOBA_BLOCK) == 0
    hm = _head_mask(R, npp * PAGE * vecs, vecs, nh)
    hmn = _head_mask(R, n_new, vecs, nh, causal_q=q_of_row)
    const = lambda shape: pl.BlockSpec(shape, lambda b, p, pt: (0,) * len(shape))
    per_b = lambda shape: pl.BlockSpec((None,) + shape, lambda b, p, pt: (b,) + (0,) * len(shape))
    in_specs = [per_b((R, HEAD_DIM)), const(hm.shape), const(hmn.shape)]
    if joint:
        in_specs += [pl.BlockSpec((None, PAGE * vecs, HEAD_DIM), lambda b, p, pt, r=r: (pt[b, p * npp + r], 0, 0))
                     for r in range(npp)]
        in_specs += [per_b((n_new, HEAD_DIM))]
        args = (pools[0],) * npp + (new_rows[0],)
    else:
        for c in range(2):
            in_specs += [pl.BlockSpec((None, PAGE, None, nh, HEAD_DIM),
                                      lambda b, p, pt, r=r, c=c: (pt[b, p * npp + r], 0, c, 0, 0)) for r in range(npp)]
        in_specs += [per_b((n_new, HEAD_DIM)), per_b((n_new, HEAD_DIM))]
        args = (pools[0],) * (2 * npp) + (new_rows[0], new_rows[1])
    if mode == "fox":
        cq, ck = extras
        in_specs += [per_b((R, LANES)),
                     pl.BlockSpec((None, 1, npp * PAGE * nh), lambda b, p, pt: (b, 0, p)),
                     pl.BlockSpec((None, 1, n_new), lambda b, p, pt: (b, 0, P * PAGE * nh // n_new))]
        args += (cq, ck, ck)
    else:
        (sel,) = extras
        in_specs += [per_b((R, sel.shape[2]))]
        args += (sel,)
    gs = pltpu.PrefetchScalarGridSpec(
        num_scalar_prefetch=1, grid=(B, P // npp), in_specs=in_specs,
        out_specs=pl.BlockSpec((None, R, HEAD_DIM), lambda b, p, pt: (b, 0, 0)),
        scratch_shapes=[pltpu.VMEM((R, 1), F32), pltpu.VMEM((R, 1), F32), pltpu.VMEM((R, HEAD_DIM), F32)])
    return pl.pallas_call(
        functools.partial(_paged_attn_kernel, mode=mode, nh=nh, npp=npp, n_steps=P // npp), grid_spec=gs,
        out_shape=jax.ShapeDtypeStruct((B, R, HEAD_DIM), out_dtype), name="paged_attn_" + mode,
        compiler_params=_cparams("parallel", "arbitrary"))(page_table, q_rows, hm, hmn, *args)


PAGES_PER_STEP = 8


def _cumsum_paged_kernel(pt_ref, *refs, n_groups):
    del pt_ref
    x_refs = refs[:PAGES_PER_STEP]
    xn_ref, o_ref, carry_ref = refs[PAGES_PER_STEP:]
    g = pl.program_id(1)

    @pl.when(g == 0)
    def _():
        carry_ref[...] = jnp.zeros(carry_ref.shape, F32)

    carry = carry_ref[...]
    for r in range(PAGES_PER_STEP):
        tail = xn_ref[...] if r == 0 else jnp.zeros(xn_ref.shape, F32)
        cs = _tile_cumsum(jnp.where(g == n_groups, tail, x_refs[r][...])) + carry
        o_ref[:, r * PAGE:(r + 1) * PAGE] = cs
        carry = jnp.broadcast_to(cs[:, LANES - 1:LANES], carry.shape)
    carry_ref[...] = carry


def _cumsum_paged(pool_t, page_table, new_t):
    B, P = page_table.shape
    H = pool_t.shape[1]
    n = PAGES_PER_STEP
    assert P % n == 0
    ng = P // n
    page = [pl.BlockSpec((None, H, PAGE), lambda b, g, pt, r=r: (pt[b, jnp.minimum(g, ng - 1) * n + r], 0, 0))
            for r in range(n)]
    gs = pltpu.PrefetchScalarGridSpec(
        num_scalar_prefetch=1, grid=(B, ng + 1),
        in_specs=page + [pl.BlockSpec((None, H, PAGE), lambda b, g, pt: (b, 0, 0))],
        out_specs=pl.BlockSpec((None, H, n * PAGE), lambda b, g, pt: (b, 0, g)),
        scratch_shapes=[pltpu.VMEM((H, LANES), F32)])
    return pl.pallas_call(
        functools.partial(_cumsum_paged_kernel, n_groups=ng), grid_spec=gs,
        out_shape=jax.ShapeDtypeStruct((B, H, (ng + 1) * n * PAGE), F32), name="cumsum_paged",
        compiler_params=_cparams("parallel", "arbitrary"))(page_table, *([pool_t] * n), new_t)


def _kmeans_kernel(pt_ref, *refs, ppb):
    del pt_ref
    x_refs, o_ref = refs[:PAGES_PER_STEP], refs[PAGES_PER_STEP]
    for blk in range(PAGES_PER_STEP // ppb):
        tot = jnp.sum(x_refs[blk * ppb][...], axis=0)
        for r in range(1, ppb):
            tot = tot + jnp.sum(x_refs[blk * ppb + r][...], axis=0)
        o_ref[blk] = tot * (1.0 / MOBA_BLOCK)


def _moba_kmeans(pool5, page_table):
    B, P = page_table.shape
    ppb = MOBA_BLOCK // PAGE
    n = PAGES_PER_STEP
    assert P % n == 0 and n % ppb == 0
    page = [pl.BlockSpec((None, PAGE, None, N_HEADS, HEAD_DIM), lambda b, g, pt, r=r: (pt[b, g * n + r], 0, 0, 0, 0))
            for r in range(n)]
    gs = pltpu.PrefetchScalarGridSpec(
        num_scalar_prefetch=1, grid=(B, P // n), in_specs=page,
        out_specs=pl.BlockSpec((None, n // ppb, N_HEADS, HEAD_DIM), lambda b, g, pt: (b, g, 0, 0)))
    return pl.pallas_call(
        functools.partial(_kmeans_kernel, ppb=ppb), grid_spec=gs,
        out_shape=jax.ShapeDtypeStruct((B, P // ppb, N_HEADS, HEAD_DIM), F32), name="moba_kmeans",
        compiler_params=_cparams("parallel", "parallel"))(page_table, *([pool5] * n))


def _moba_select_kernel(q_ref, km_ref, o_ref, *, n_past_blocks):
    gate = _nt(q_ref[...], km_ref[...].astype(BF16))
    lane = _iota(gate.shape, 1)
    gate = jnp.where(lane < n_past_blocks, gate, NEG_INF)
    o_ref[...] = _top_k_mask(gate, MOBA_TOP, finite_only=True)


def _moba_select(qbd, kmeans, n_past_blocks):
    B, R, C = qbd.shape
    return pl.pallas_call(
        functools.partial(_moba_select_kernel, n_past_blocks=n_past_blocks), grid=(B,),
        in_specs=[pl.BlockSpec((None, R, C), lambda b: (b, 0, 0)), pl.BlockSpec((None, LANES, C), lambda b: (b, 0, 0))],
        out_specs=pl.BlockSpec((None, R, LANES), lambda b: (b, 0, 0)),
        out_shape=jax.ShapeDtypeStruct((B, R, LANES), F32), name="moba_select",
        compiler_params=_cparams("parallel"))(qbd, kmeans)


def _nsa_cmp_sample_kernel(q_ref, kc_ref, vc_ref, ovl_ref, o_ref, sel_ref, *, past):
    R = q_ref.shape[0]
    nm = kc_ref.shape[0]
    G = NSA_GROUP
    s = _nt(q_ref[...], kc_ref[...].astype(BF16)) * SCALE
    m_i = _iota((R, nm), 1)
    qpos = past + (_iota((R, nm), 0) % N_HEADS) // NSA_KV_HEADS
    ok = (m_i >= 1) & ((m_i - 1) * CMP_STRIDE + (CMP_LEN - 1) <= qpos)
    p_cmp = _masked_softmax(s, ok).astype(BF16)
    o_ref[...] = _diag_blocks(_dot(p_cmp, vc_ref[...].astype(BF16)), NSA_KV_HEADS)
    pov = _dot(p_cmp, ovl_ref[...])
    rg = R // G
    imp = pov[0:rg]
    for g in range(1, G):
        imp = imp + pov[g * rg:(g + 1) * rg]
    lane = _iota(imp.shape, 1)
    qblk = (past + _iota(imp.shape, 0) // NSA_KV_HEADS) // SLC_LEN
    sel = _top_k_mask(_slc_scores(imp, qblk, lane), SLC_TOP)
    sel_ref[...] = jnp.concatenate([sel] * G, axis=0)


def _nsa_cmp_sample(qbd, comp, past):
    B, R, C = qbd.shape
    nm = comp.shape[1]
    nb = -(-(past + R // N_HEADS) // SLC_LEN)
    width = -(-nb // LANES) * LANES
    ovl = _overlap_shifted(nm, nb, width)
    return pl.pallas_call(
        functools.partial(_nsa_cmp_sample_kernel, past=past), grid=(B,),
        in_specs=[pl.BlockSpec((None, R, C), lambda b: (b, 0, 0)),
                  pl.BlockSpec((None, nm, C), lambda b: (b, 0, 0)),
                  pl.BlockSpec((None, nm, C), lambda b: (b, 0, 1)),
                  pl.BlockSpec((nm, width), lambda b: (0, 0))],
        out_specs=[pl.BlockSpec((None, R, HEAD_DIM), lambda b: (b, 0, 0)),
                   pl.BlockSpec((None, R, width), lambda b: (b, 0, 0))],
        out_shape=[jax.ShapeDtypeStruct((B, R, HEAD_DIM), F32), jax.ShapeDtypeStruct((B, R, width), F32)],
        name="nsa_cmp_sample", compiler_params=_cparams("parallel"))(qbd, comp, comp, ovl)


def _nsa_win_sample_kernel(q_ref, kw_ref, vw_ref, oc_ref, os_ref, g_ref, o_ref, *, past, wb):
    R = q_ref.shape[0]
    nk = kw_ref.shape[0]
    s = _nt(q_ref[...], kw_ref[...].astype(BF16)) * SCALE
    j = _iota((R, nk), 1)
    qpos = past + (_iota((R, nk), 0) % N_HEADS) // NSA_KV_HEADS
    wpos = past - wb + j
    d = qpos - wpos
    ok = (d >= 0) & (d < WINDOW) & (wpos >= 0)
    p = _masked_softmax(s, ok).astype(BF16)
    o_win = _diag_blocks(_dot(p, vw_ref[...].astype(BF16)), NSA_KV_HEADS)
    g = g_ref[...]
    o = oc_ref[...] * g[:, 0:1] + os_ref[...] * g[:, 1:2] + o_win * g[:, 2:3]
    o_ref[...] = o.astype(o_ref.dtype)


def _nsa_win_sample(qbd, kw, o_cmp, o_slc, gates, past, wb):
    B, R, C = qbd.shape
    nk = kw.shape[1]
    row = pl.BlockSpec((None, R, HEAD_DIM), lambda b: (b, 0, 0))
    return pl.pallas_call(
        functools.partial(_nsa_win_sample_kernel, past=past, wb=wb), grid=(B,),
        in_specs=[pl.BlockSpec((None, R, C), lambda b: (b, 0, 0)),
                  pl.BlockSpec((None, nk, C), lambda b: (b, 0, 0)),
                  pl.BlockSpec((None, nk, C), lambda b: (b, 0, 1)),
                  row, row, row],
        out_specs=row, out_shape=jax.ShapeDtypeStruct((B, R, HEAD_DIM), BF16), name="nsa_win_sample",
        compiler_params=_cparams("parallel"))(qbd, kw, kw, o_cmp, o_slc, gates)


def _qbd_heads(q):
    B, Q, H, dh = q.shape
    x = q[:, :, :, None, :] * jnp.eye(H, dtype=q.dtype)[None, None, :, :, None]
    return x.reshape(B, Q * H, H * dh).astype(BF16)


def _qbd_groups(q):
    B, Q, H, dh = q.shape
    x = q.reshape(B, Q, NSA_KV_HEADS, NSA_GROUP, dh).transpose(0, 3, 1, 2, 4)
    x = x[:, :, :, :, None, :] * jnp.eye(NSA_KV_HEADS, dtype=q.dtype)[None, None, None, :, :, None]
    return x.reshape(B, NSA_GROUP * Q * NSA_KV_HEADS, NSA_KV_HEADS * dh).astype(BF16)


def _pad_rows(x, rows):
    return jnp.pad(x, ((0, 0), (0, rows - x.shape[1]), (0, 0)))


def _new_kv_rows(new):
    B, Q, _, H, dh = new.shape
    return (_pad_rows(new[:, :, 0].reshape(B, Q * H, dh), LANES), _pad_rows(new[:, :, 1].reshape(B, Q * H, dh), LANES))


def _sample_tables(B, Q, past):
    return _rope_tables(jnp.tile(past + jnp.arange(Q, dtype=I32), B))


def _nsa_sample_mixer(x2d, g, w_in, pe, w1, w2, pool_cmp, pool_slc, win_buf, page_table, layer, B, Q):
    P = page_table.shape[1]
    past = P * PAGE
    assert past % SLC_LEN == 0 and Q * N_HEADS % 8 == 0
    n_pool = pool_cmp.shape[1]
    pt = page_table + layer * n_pool
    z, gates = _nsa_project(x2d, g, w_in, _sample_tables(B, Q, past), B * Q)
    kv = [z[:, NSA_Q + 2 * NSA_KV * br:NSA_Q + 2 * NSA_KV * (br + 1)].reshape(B, Q, 2 * NSA_KV) for br in range(N_BRANCH)]
    assert (past + Q - CMP_LEN) // CMP_STRIDE + 1 == past // CMP_STRIDE - 1
    comp = _compress(pool_cmp.reshape(-1, PAGE * ROWS_PER_KV, HEAD_DIM), pt, w1, w2, pe)
    comp = comp.reshape(B, -1, ROWS_PER_KV * HEAD_DIM)
    q4 = z[:, :NSA_Q].reshape(B, Q, N_HEADS, HEAD_DIM)
    qbd = _qbd_groups(q4)
    o_cmp, sel = _nsa_cmp_sample(qbd, comp, past)
    q_rows = q4.reshape(B, Q, NSA_KV_HEADS, NSA_GROUP, HEAD_DIM).transpose(0, 3, 1, 2, 4)
    q_rows = q_rows.reshape(B, Q * N_HEADS, HEAD_DIM).astype(BF16)
    x_new = _pad_rows(kv[1].reshape(B, Q * ROWS_PER_KV, HEAD_DIM), LANES)
    o_slc = _paged_attn(q_rows, (pool_slc.reshape(-1, PAGE * ROWS_PER_KV, HEAD_DIM),), pt, (x_new,), (sel,),
                        mode="nsa", out_dtype=F32)
    wb = win_buf.shape[2]
    kw = jnp.concatenate([win_buf[layer].reshape(B, wb, 2 * NSA_KV), kv[2]], axis=1)
    g_rows = gates[:, :N_HEADS * N_BRANCH].reshape(B, Q, NSA_KV_HEADS, NSA_GROUP, N_BRANCH)
    g_rows = jnp.pad(g_rows.transpose(0, 3, 1, 2, 4).reshape(B, Q * N_HEADS, N_BRANCH), ((0, 0), (0, 0), (0, LANES - N_BRANCH)))
    o = _nsa_win_sample(qbd, _pad_rows(kw, -(-(wb + Q) // LANES) * LANES), o_cmp, o_slc, g_rows, past, wb)
    o = o.reshape(B, NSA_GROUP, Q, NSA_KV_HEADS, HEAD_DIM).transpose(0, 2, 3, 1, 4).reshape(B * Q, HD)
    shp = (B, -1, 2, NSA_KV_HEADS, HEAD_DIM)
    return o, (kv[0].reshape(shp), kv[1].reshape(shp), kw[:, Q:].reshape(shp))


def _moba_sample_mixer(x2d, g, w_in, pool, page_table, layer, B, Q):
    P = page_table.shape[1]
    past = P * PAGE
    assert past % MOBA_BLOCK == 0 and Q <= MOBA_BLOCK and past // MOBA_BLOCK <= LANES
    n_pool = pool.shape[1]
    pt = page_table + layer * n_pool
    z = _norm_mm(x2d, g, w_in, 3 * HD, tm=B * Q, epi="rope", rope=(_moba_rope_flags(),) + _sample_tables(B, Q, past))
    q4 = z[:, :HD].reshape(B, Q, N_HEADS, HEAD_DIM)
    nbp = past // MOBA_BLOCK
    pool5 = pool.reshape(-1, PAGE, 2, N_HEADS, HEAD_DIM)
    km = _moba_kmeans(pool5, pt).reshape(B, nbp, HD)
    sel = _moba_select(_qbd_heads(q4), _pad_rows(km, LANES), nbp)
    new = z[:, HD:].reshape(B, Q, 2, N_HEADS, HEAD_DIM)
    o = _paged_attn(q4.reshape(B, Q * N_HEADS, HEAD_DIM).astype(BF16), (pool5,), pt, _new_kv_rows(new), (sel,),
                    mode="moba", out_dtype=BF16)
    return o.reshape(B * Q, HD), (new,)


def _fox_sample_mixer(x2d, g, w_in, b_f, pool, pool_logf, page_table, layer, B, Q):
    P = page_table.shape[1]
    n_pool = pool.shape[1]
    pt = page_table + layer * n_pool
    z, logf = _fox_project(x2d, g, w_in, b_f, B * Q)
    log_f = logf[:, :N_HEADS].reshape(B, Q, N_HEADS)
    pool_t = pool_logf.astype(F32).transpose(0, 1, 3, 2).reshape(-1, N_HEADS, PAGE)
    new_t = jnp.pad(log_f.transpose(0, 2, 1), ((0, 0), (0, 0), (0, PAGE - Q)))
    ct = _cumsum_paged(pool_t, pt, new_t)
    cq = ct[:, :, P * PAGE:P * PAGE + Q].transpose(0, 2, 1).reshape(B, Q * N_HEADS, 1)
    ck = ct.transpose(0, 2, 1).reshape(B, 1, -1)
    q_rows = z[:, :HD].reshape(B, Q * N_HEADS, HEAD_DIM).astype(BF16)
    new = z[:, HD:].reshape(B, Q, 2, N_HEADS, HEAD_DIM)
    o = _paged_attn(q_rows, (pool.reshape(-1, PAGE, 2, N_HEADS, HEAD_DIM),), pt, _new_kv_rows(new),
                    (jnp.broadcast_to(cq, (B, Q * N_HEADS, LANES)), ck), mode="fox", out_dtype=BF16)
    return o.reshape(B * Q, HD), (new, log_f)


PROMPT_TM = 1024
FFN_DOWN_TN = 256


def kernel(x_prompt, x_sample, cache_nsa_cmp, cache_nsa_slc, cache_nsa_win, cache_moba_kv, cache_fox_kv,
           cache_fox_logf, page_table, norms, ffn_wi, ffn_wo, final_norm, nsa_w_in, nsa_cmp_pe, nsa_cmp_w1,
           nsa_cmp_w2, nsa_w_out, moba_w_in, moba_w_out, fox_w_in, fox_b_f, fox_w_out):
    B, T, D = x_prompt.shape
    Bs, Q, _ = x_sample.shape
    depth = norms.shape[0]
    xp = x_prompt.reshape(B * T, D)
    xs = x_sample.reshape(Bs * Q, D)
    tp = min(PROMPT_TM, B * T)
    ts = Bs * Q
    ffn_wi, ffn_wo = ffn_wi.astype(BF16), ffn_wo.astype(BF16)
    nsa_w_in, moba_w_in, fox_w_in = nsa_w_in.astype(BF16), moba_w_in.astype(BF16), fox_w_in.astype(BF16)
    nsa_w_out, moba_w_out, fox_w_out = nsa_w_out.astype(BF16), moba_w_out.astype(BF16), fox_w_out.astype(BF16)
    st_p = {0: [], 1: [], 2: []}
    st_s = {0: [], 1: [], 2: []}
    for i in range(depth):
        kind, j = i % 3, i // 3
        xp = _ffn(xp, norms[i, 0], _W(ffn_wi, (i, 0)), _W(ffn_wo, (i, 0)), tp)
        xs = _ffn(xs, norms[i, 0], _W(ffn_wi, (i, 0)), _W(ffn_wo, (i, 0)), ts)
        g = norms[i, 1]
        if kind == 0:
            w_in, w_out = _W(nsa_w_in, (j,)), _W(nsa_w_out, (j,))
            op, sp = _nsa_prompt_mixer(xp, g, w_in, nsa_cmp_pe[j], nsa_cmp_w1[j], nsa_cmp_w2[j], B, T, tp)
            os_, ss = _nsa_sample_mixer(xs, g, w_in, nsa_cmp_pe[j], nsa_cmp_w1[j], nsa_cmp_w2[j],
                                        cache_nsa_cmp, cache_nsa_slc, cache_nsa_win, page_table, j, Bs, Q)
        elif kind == 1:
            w_in, w_out = _W(moba_w_in, (j,)), _W(moba_w_out, (j,))
            op, sp = _moba_prompt_mixer(xp, g, w_in, B, T, tp)
            os_, ss = _moba_sample_mixer(xs, g, w_in, cache_moba_kv, page_table, j, Bs, Q)
        else:
            w_in, w_out = _W(fox_w_in, (j,)), _W(fox_w_out, (j,))
            op, sp = _fox_prompt_mixer(xp, g, w_in, fox_b_f[j], B, T, tp)
            os_, ss = _fox_sample_mixer(xs, g, w_in, fox_b_f[j], cache_fox_kv, cache_fox_logf, page_table, j, Bs, Q)
        st_p[kind].append(sp)
        st_s[kind].append(ss)
        xp = _mm_res(op, w_out, xp, 1.0, tm=tp, tn=512)
        xs = _mm_res(os_, w_out, xs, 1.0, tm=ts, tn=512)
        xp = _ffn(xp, norms[i, 2], _W(ffn_wi, (i, 1)), _W(ffn_wo, (i, 1)), tp)
        xs = _ffn(xs, norms[i, 2], _W(ffn_wi, (i, 1)), _W(ffn_wo, (i, 1)), ts)
    y_prompt = _rmsnorm(xp, final_norm, tm=tp).reshape(B, T, D)
    y_sample = _rmsnorm(xs, final_norm, tm=ts).reshape(Bs, Q, D)

    def stack(states, k):
        return jnp.stack([s[k] for s in states])

    return (y_prompt, y_sample,
            stack(st_p[0], 0), stack(st_p[0], 1), stack(st_p[0], 2),
            stack(st_s[0], 0), stack(st_s[0], 1), stack(st_s[0], 2),
            stack(st_p[1], 0), stack(st_s[1], 0),
            stack(st_p[2], 0), stack(st_p[2], 1), stack(st_s[2], 0), stack(st_s[2], 1))
```

```python
import functools
from typing import NamedTuple

import numpy as np
import jax
import jax.numpy as jnp
from jax import lax
from jax.experimental import pallas as pl
from jax.experimental.pallas import tpu as pltpu

F32 = jnp.float32
BF16 = jnp.bfloat16
I32 = jnp.int32

LANES = 128
VMEM_LIMIT_BYTES = 56 << 20

N_HEADS = 16
HEAD_DIM = 128
ROT_DIM = HEAD_DIM // 4
ROPE_THETA = 500000.0
NORM_EPS = 1e-6
PAGE = 128
NSA_KV_HEADS = 4
NSA_GROUP = N_HEADS // NSA_KV_HEADS
CMP_LEN = 32
CMP_STRIDE = 16
SLC_LEN = 64
SLC_TOP = 16
SLC_LOCAL = 2
WINDOW = 512
N_BRANCH = 3
MOBA_BLOCK = 256
MOBA_TOP = 3
SCALE = HEAD_DIM ** -0.5
LOG2E = 1.4426950408889634
QK_LOG2 = SCALE * LOG2E
ROW_CHUNK = 64
NEG = -1e30
NEG_INF = float("-inf")
POS_INF = float("inf")


def _cparams(*sem):
    return pltpu.CompilerParams(dimension_semantics=sem, vmem_limit_bytes=VMEM_LIMIT_BYTES)


def _nt(a, b):
    return lax.dot_general(a, b, (((1,), (1,)), ((), ())), preferred_element_type=F32)


def _dot(a, b):
    return jnp.dot(a, b, preferred_element_type=F32)


def _iota(shape, dim):
    return lax.broadcasted_iota(I32, shape, dim)


def _lane_col(x, idx):
    return jnp.sum(jnp.where(_iota(x.shape, 1) == idx, x, 0.0), axis=-1, keepdims=True)


class _W(NamedTuple):
    a: jax.Array
    lead: tuple = ()


def _w_parts(w):
    return (w.a, tuple(w.lead)) if isinstance(w, _W) else (w, ())


def _w_cols(w, start):
    a, lead = _w_parts(w)
    return a[lead + (slice(None), slice(start, None))]


def _w_spec(lead, rows, tn, col_of=lambda j: j):
    return pl.BlockSpec((None,) * len(lead) + (rows, tn), lambda i, j, *_: tuple(lead) + (0, col_of(j)))


def _rms_to_bf16(x_ref, g_ref):
    x = x_ref[...]
    var = jnp.mean(x * x, axis=-1, keepdims=True)
    return ((x * lax.rsqrt(var + NORM_EPS)) * g_ref[...]).astype(BF16)


def _log_sigmoid(x):
    return jnp.minimum(x, 0.0) - jnp.log1p(jnp.exp(-jnp.abs(x)))


def _norm_mm_kernel(*refs, epi, tn, kv_first=None):
    kv_ref = None
    if epi == "rope" and kv_first is not None:
        flags_ref, x_ref, g_ref, w_ref, c_ref, s1_ref, s2_ref, o_ref, kv_ref, xn_ref = refs
    elif epi == "rope":
        flags_ref, x_ref, g_ref, w_ref, c_ref, s1_ref, s2_ref, o_ref, xn_ref = refs
    elif epi == "logsig":
        x_ref, g_ref, w_ref, b_ref, o_ref, xn_ref = refs
    else:
        x_ref, g_ref, w_ref, o_ref, xn_ref = refs
    j = pl.program_id(1)

    @pl.when(j == 0)
    def _():
        xn_ref[...] = _rms_to_bf16(x_ref, g_ref)

    z = _dot(xn_ref[...], w_ref[...].astype(BF16))
    if epi == "rope":
        nvec = tn // HEAD_DIM
        tm = z.shape[0]

        def put(hh, val, half):
            o_ref[:, hh * HEAD_DIM:(hh + 1) * HEAD_DIM] = val
            if kv_ref is not None:
                @pl.when((j >= kv_first) & ((j - kv_first) % 2 == half))
                def _():
                    kv_ref[pl.ds(half * nvec + hh, tm, stride=2 * nvec), :] = val

        @pl.when(flags_ref[j] == 1)
        def _():
            c, s1, s2 = c_ref[...], s1_ref[...], s2_ref[...]
            for hh in range(nvec):
                zs = z[:, hh * HEAD_DIM:(hh + 1) * HEAD_DIM]
                put(hh, zs * c + pltpu.roll(zs, HEAD_DIM - ROT_DIM // 2, 1) * s1 + pltpu.roll(zs, ROT_DIM // 2, 1) * s2, 0)

        @pl.when(flags_ref[j] == 0)
        def _():
            for hh in range(nvec):
                put(hh, z[:, hh * HEAD_DIM:(hh + 1) * HEAD_DIM], 1)
    elif epi == "sigmoid":
        o_ref[...] = jax.nn.sigmoid(z)
    elif epi == "logsig":
        o_ref[...] = _log_sigmoid(z + b_ref[...])
    else:
        o_ref[...] = z


def _norm_mm(x, g, w, n_out, *, tm, tn=512, epi="none", rope=None, bias=None, kv_rows=None):
    M, D = x.shape
    w, lead = _w_parts(w)
    tn = min(tn, n_out)
    assert M % tm == 0 and n_out % tn == 0
    grid = (M // tm, n_out // tn)
    g2 = g.reshape(1, D)
    kern = functools.partial(_norm_mm_kernel, epi=epi, tn=tn)
    scratch = [pltpu.VMEM((tm, D), BF16)]
    out_shape = jax.ShapeDtypeStruct((M, n_out), F32)
    if epi == "rope":
        flags, c, s1, s2 = rope
        nt = c.shape[0] // tm
        tab = pl.BlockSpec((tm, HEAD_DIM), lambda i, j, f: (i % nt, 0))
        out_specs = pl.BlockSpec((tm, tn), lambda i, j, f: (i, j))
        if kv_rows is not None:
            first, nbr = kv_rows
            vecs = 2 * tn // HEAD_DIM
            kern = functools.partial(_norm_mm_kernel, epi=epi, tn=tn, kv_first=first)
            out_shape = [out_shape, jax.ShapeDtypeStruct((nbr, M * vecs, HEAD_DIM), F32)]
            out_specs = [out_specs, pl.BlockSpec((None, tm * vecs, HEAD_DIM),
                                                 lambda i, j, f: (jnp.clip((j - first) // 2, 0, nbr - 1), i, 0))]
        gs = pltpu.PrefetchScalarGridSpec(
            num_scalar_prefetch=1, grid=grid,
            in_specs=[pl.BlockSpec((tm, D), lambda i, j, f: (i, 0)),
                      pl.BlockSpec((1, D), lambda i, j, f: (0, 0)),
                      _w_spec(lead, D, tn),
                      tab, tab, tab],
            out_specs=out_specs,
            scratch_shapes=scratch)
        return pl.pallas_call(kern, grid_spec=gs, out_shape=out_shape, name="norm_mm_rope",
                              compiler_params=_cparams("parallel", "arbitrary"))(flags, x, g2, w, c, s1, s2)
    in_specs = [pl.BlockSpec((tm, D), lambda i, j: (i, 0)),
                pl.BlockSpec((1, D), lambda i, j: (0, 0)),
                _w_spec(lead, D, tn)]
    args = [x, g2, w]
    if epi == "logsig":
        in_specs.append(pl.BlockSpec((1, tn), lambda i, j: (0, j)))
        args.append(bias.reshape(1, n_out))
    return pl.pallas_call(kern, grid=grid, in_specs=in_specs,
                          out_specs=pl.BlockSpec((tm, tn), lambda i, j: (i, j)),
                          out_shape=out_shape, scratch_shapes=scratch, name="norm_mm_" + epi,
                          compiler_params=_cparams("parallel", "arbitrary"))(*args)


def _swiglu_up_kernel(x_ref, g_ref, wa_ref, *rest, nsub):
    wb_refs = rest[:nsub]
    o_ref, xn_ref, wb_scr = rest[nsub:]

    @pl.when(pl.program_id(1) == 0)
    def _():
        xn_ref[...] = _rms_to_bf16(x_ref, g_ref)

    for r in range(nsub):
        wb_scr[:, r * LANES:(r + 1) * LANES] = wb_refs[r][...]
    xn = xn_ref[...]
    a = _dot(xn, wa_ref[...])
    b = _dot(xn, wb_scr[...])
    o_ref[...] = (a * jax.nn.sigmoid(a) * b).astype(o_ref.dtype)


def _swiglu_up(x, g, wi, *, tm, tn=512):
    M, D = x.shape
    wi, lead = _w_parts(wi)
    F = wi.shape[-1] // 2
    assert F % LANES == 0 and M % tm == 0
    nsub = tn // LANES
    off = F // LANES
    last = wi.shape[-1] // LANES - 1
    in_specs = [pl.BlockSpec((tm, D), lambda i, j: (i, 0)), pl.BlockSpec((1, D), lambda i, j: (0, 0)),
                _w_spec(lead, D, tn)]
    in_specs += [_w_spec(lead, D, LANES, lambda j, r=r: jnp.minimum(off + nsub * j + r, last)) for r in range(nsub)]
    return pl.pallas_call(
        functools.partial(_swiglu_up_kernel, nsub=nsub),
        grid=(M // tm, pl.cdiv(F, tn)), in_specs=in_specs,
        out_specs=pl.BlockSpec((tm, tn), lambda i, j: (i, j)),
        out_shape=jax.ShapeDtypeStruct((M, F), BF16),
        scratch_shapes=[pltpu.VMEM((tm, D), BF16), pltpu.VMEM((D, tn), BF16)],
        name="swiglu_up",
        compiler_params=_cparams("parallel", "arbitrary"))(x, g.reshape(1, D), wi, *([wi] * nsub))


def _mm_res_kernel(u_ref, w_ref, r_ref, o_ref, *, scale):
    y = _dot(u_ref[...].astype(BF16), w_ref[...].astype(BF16))
    o_ref[...] = r_ref[...] + (y if scale == 1.0 else scale * y)


def _mm_res(u, w, res, scale, *, tm, tn):
    M, K = u.shape
    w, lead = _w_parts(w)
    N = w.shape[-1]
    assert M % tm == 0 and N % tn == 0
    return pl.pallas_call(
        functools.partial(_mm_res_kernel, scale=scale),
        grid=(M // tm, N // tn),
        in_specs=[pl.BlockSpec((tm, K), lambda i, j: (i, 0)),
                  _w_spec(lead, K, tn),
                  pl.BlockSpec((tm, tn), lambda i, j: (i, j))],
        out_specs=pl.BlockSpec((tm, tn), lambda i, j: (i, j)),
        out_shape=jax.ShapeDtypeStruct((M, N), F32), name="mm_res",
        compiler_params=_cparams("parallel", "arbitrary"))(u, w, res)


def _rmsnorm_kernel(x_ref, g_ref, o_ref):
    x = x_ref[...]
    var = jnp.mean(x * x, axis=-1, keepdims=True)
    o_ref[...] = (x * lax.rsqrt(var + NORM_EPS)) * g_ref[...]


def _rmsnorm(x, g, *, tm):
    M, D = x.shape
    return pl.pallas_call(
        _rmsnorm_kernel, grid=(M // tm,),
        in_specs=[pl.BlockSpec((tm, D), lambda i: (i, 0)), pl.BlockSpec((1, D), lambda i: (0, 0))],
        out_specs=pl.BlockSpec((tm, D), lambda i: (i, 0)),
        out_shape=jax.ShapeDtypeStruct((M, D), F32), name="rmsnorm",
        compiler_params=_cparams("parallel"))(x, g.reshape(1, D))


def _rope_tables(pos):
    half = ROT_DIM // 2
    inv = ROPE_THETA ** (-jnp.arange(half, dtype=F32) / half)
    ang = pos.astype(F32)[:, None] * inv[None, :]
    cos, sin = jnp.cos(ang), jnp.sin(ang)
    n = pos.shape[0]
    zeros = jnp.zeros((n, HEAD_DIM - ROT_DIM), F32)
    zh = jnp.zeros((n, half), F32)
    c = jnp.concatenate([cos, cos, jnp.ones((n, HEAD_DIM - ROT_DIM), F32)], axis=1)
    s1 = jnp.concatenate([-sin, zh, zeros], axis=1)
    s2 = jnp.concatenate([zh, sin, zeros], axis=1)
    return c, s1, s2


def _online_update(t, m, l, acc, v, roff=None, pv_roll=0):
    ms, ls, als, ps = [], [], [], []
    for r0 in range(0, t.shape[0], ROW_CHUNK):
        r = slice(r0, r0 + ROW_CHUNK)
        tmax = jnp.max(t[r], axis=-1, keepdims=True)
        if roff is not None:
            tmax = tmax + roff[r]
        m_new = jnp.maximum(m[r], tmax)
        alpha = jnp.exp2(m[r] - m_new)
        p = jnp.exp2(t[r] + ((roff[r] - m_new) if roff is not None else -m_new))
        ms.append(m_new)
        als.append(alpha)
        ls.append(alpha * l[r] + jnp.sum(p, axis=-1, keepdims=True))
        ps.append((pltpu.roll(p, pv_roll, 1) if pv_roll else p).astype(BF16))
    cat = lambda xs: xs[0] if len(xs) == 1 else jnp.concatenate(xs, axis=0)
    return cat(ms), cat(ls), cat(als) * acc + _dot(cat(ps), v)


def _online_update_t(t, m, l, acc, vt, coff=None):
    tmax = jnp.max(t, axis=0, keepdims=True)
    if coff is not None:
        tmax = tmax + coff
    m_new = jnp.maximum(m, tmax)
    alpha = jnp.exp2(m - m_new)
    p = jnp.exp2(t + ((coff - m_new) if coff is not None else -m_new))
    l_new = alpha * l + jnp.sum(p, axis=0, keepdims=True)
    return m_new, l_new, alpha * acc + _dot(vt, p.astype(BF16))


def _flash_init_t(n_q, width=HEAD_DIM):
    return (jnp.full((1, n_q), NEG, F32), jnp.zeros((1, n_q), F32), jnp.zeros((width, n_q), F32))


def _transpose_into(dst_ref, src_ref, n_rows, dtype, chunk=512):
    for r0 in range(0, n_rows, chunk):
        r1 = min(r0 + chunk, n_rows)
        dst_ref[:, r0:r1] = src_ref[r0:r1, :].T.astype(dtype)


def _mask_bias(ok):
    return jnp.where(ok, 0.0, NEG)


def _flash_init(rows, width=HEAD_DIM):
    return (jnp.full((rows, 1), NEG, F32), jnp.zeros((rows, 1), F32), jnp.zeros((rows, width), F32))


def _flash_finish(l, acc):
    return acc / jnp.where(l > 0.0, l, 1.0)


def _masked_softmax(s, ok, axis=-1):
    sm = jnp.where(ok, s, NEG)
    mx = jnp.max(sm, axis=axis, keepdims=True)
    e = jnp.where(ok, jnp.exp(sm - mx), 0.0)
    d = jnp.sum(e, axis=axis, keepdims=True)
    return e / jnp.where(d > 0.0, d, 1.0)


def _top_k_mask(score, k, axis=-1, finite_only=False):
    idx = _iota(score.shape, axis % score.ndim)
    n = score.shape[axis]
    sel = jnp.zeros(score.shape, F32)
    for _ in range(k):
        mx = jnp.max(score, axis=axis, keepdims=True)
        first = jnp.min(jnp.where(score == mx, idx, n), axis=axis, keepdims=True)
        pick = idx == first
        sel = jnp.where((pick & (mx > NEG_INF)) if finite_only else pick, 1.0, sel)
        score = jnp.where(pick, NEG_INF, score)
    return sel


def _tile_cumsum(x):
    lane = _iota(x.shape, 1)
    for s in (1, 2, 4, 8, 16, 32, 64):
        x = x + jnp.where(lane >= s, pltpu.roll(x, s, 1), 0.0)
    return x


def _cumsum_rows_kernel(x_ref, o_ref, *, n_tiles):
    carry = jnp.zeros((x_ref.shape[0], 1), F32)
    for j in range(n_tiles):
        cs = _tile_cumsum(x_ref[:, j * LANES:(j + 1) * LANES]) + carry
        o_ref[:, j * LANES:(j + 1) * LANES] = cs
        carry = cs[:, LANES - 1:LANES]


def _cumsum_rows(xt):
    B, H, T = xt.shape
    return pl.pallas_call(
        functools.partial(_cumsum_rows_kernel, n_tiles=T // LANES), grid=(B,),
        in_specs=[pl.BlockSpec((None, H, T), lambda b: (b, 0, 0))],
        out_specs=pl.BlockSpec((None, H, T), lambda b: (b, 0, 0)),
        out_shape=jax.ShapeDtypeStruct((B, H, T), F32), name="cumsum_rows",
        compiler_params=_cparams("parallel"))(xt)


def _fox_prompt_kernel(q_ref, k_ref, v_ref, c_ref, ct_ref, o_ref, vt_ref, ck_ref, *, tq, T):
    h = pl.program_id(1)
    qi = pl.program_id(2)

    @pl.when(qi == 0)
    def _():
        _transpose_into(vt_ref, v_ref, T, BF16)
        for r0 in range(0, T, 512):
            col = _lane_col(c_ref[r0:r0 + 512, :], h) * LOG2E
            ck_ref[r0:r0 + 512, :] = jnp.broadcast_to(col, (512, LANES))

    q = q_ref[...].astype(BF16)
    q0 = pl.multiple_of(qi * tq, tq)
    cq2 = ct_ref[pl.ds(h, 1), pl.ds(q0, tq)] * LOG2E

    def step(j, carry, diagonal):
        k0 = pl.multiple_of(j * tq, tq)
        k = k_ref[pl.ds(k0, tq), :].astype(BF16)
        t = _nt(k, q) * QK_LOG2 - jnp.concatenate([ck_ref[pl.ds(k0, tq), :]] * (tq // LANES), axis=1)
        if diagonal:
            t = t + _mask_bias(_iota((tq, tq), 0) <= _iota((tq, tq), 1))
        return _online_update_t(t, *carry, vt_ref[:, pl.ds(k0, tq)], coff=cq2)

    carry = lax.fori_loop(0, qi, lambda j, c: step(j, c, False), _flash_init_t(tq))
    _, l, acc = step(qi, carry, True)
    o_ref[...] = _flash_finish(l, acc).T.astype(o_ref.dtype)


def _fox_prompt_attn(z, c, ct, B, T, *, tq=1024):
    tq = min(tq, T)
    assert T % tq == 0 and T % 512 == 0
    nq = T // tq
    H = N_HEADS
    return pl.pallas_call(
        functools.partial(_fox_prompt_kernel, tq=tq, T=T), grid=(B, H, nq),
        in_specs=[pl.BlockSpec((tq, HEAD_DIM), lambda b, h, i: (b * nq + i, h)),
                  pl.BlockSpec((T, HEAD_DIM), lambda b, h, i: (b, H + h)),
                  pl.BlockSpec((T, HEAD_DIM), lambda b, h, i: (b, 2 * H + h)),
                  pl.BlockSpec((T, LANES), lambda b, h, i: (b, 0)),
                  pl.BlockSpec((None, H, T), lambda b, h, i: (b, 0, 0))],
        out_specs=pl.BlockSpec((tq, HEAD_DIM), lambda b, h, i: (b * nq + i, h)),
        out_shape=jax.ShapeDtypeStruct((B * T, H * HEAD_DIM), BF16),
        scratch_shapes=[pltpu.VMEM((HEAD_DIM, T), BF16), pltpu.VMEM((T, LANES), F32)], name="fox_prompt",
        compiler_params=_cparams("parallel", "parallel", "arbitrary"))(z, z, z, c, ct)


def _moba_prompt_kernel(q_ref, k_ref, v_ref, o_ref, km_ref, vt_ref, *, tq, T):
    qi = pl.program_id(2)
    q0 = qi * tq
    nb = T // MOBA_BLOCK
    nsub = tq // MOBA_BLOCK
    nbr = km_ref.shape[0]

    @pl.when(qi == 0)
    def _():
        km_ref[...] = jnp.zeros(km_ref.shape, F32)
        for n in range(nb):
            km_ref[n:n + 1, :] = jnp.mean(k_ref[n * MOBA_BLOCK:(n + 1) * MOBA_BLOCK, :], axis=0, keepdims=True)
        _transpose_into(vt_ref, v_ref, T, BF16)

    q = q_ref[...].astype(BF16)
    blk = _iota((nbr, tq), 0)
    own = (q0 + _iota((nbr, tq), 1)) // MOBA_BLOCK
    gate_t = jnp.where(blk < own, _nt(km_ref[...].astype(BF16), q), NEG_INF)
    sel_bias_t = _mask_bias(_top_k_mask(gate_t, MOBA_TOP, axis=0, finite_only=True) > 0.5)

    def tile(j, diagonal):
        k0 = pl.multiple_of(j * tq, tq)
        t = _nt(k_ref[pl.ds(k0, tq), :].astype(BF16), q) * QK_LOG2
        parts = []
        for r in range(nsub):
            b = jnp.sum(jnp.where(blk == j * nsub + r, sel_bias_t, 0.0), axis=0, keepdims=True)
            b = jnp.broadcast_to(b, (MOBA_BLOCK, tq))
            if diagonal:
                kk = r * MOBA_BLOCK + _iota((MOBA_BLOCK, tq), 0)
                qq = _iota((MOBA_BLOCK, tq), 1)
                b = jnp.where(qq // MOBA_BLOCK == r, _mask_bias(kk <= qq), b)
            parts.append(b)
        return t + jnp.concatenate(parts, axis=0), vt_ref[:, pl.ds(k0, tq)]

    t, vt = tile(qi, True)
    carry = _online_update_t(t, *_flash_init_t(tq), vt)

    def body(j, carry):
        t, vt = tile(j, False)
        return _online_update_t(t, *carry, vt)

    _, l, acc = lax.fori_loop(0, qi, body, carry)
    o_ref[...] = _flash_finish(l, acc).T.astype(o_ref.dtype)


def _moba_prompt_attn(z, B, T, *, tq=1024):
    tq = min(tq, T)
    assert T % tq == 0 and tq % MOBA_BLOCK == 0
    nq = T // tq
    H = N_HEADS
    return pl.pallas_call(
        functools.partial(_moba_prompt_kernel, tq=tq, T=T), grid=(B, H, nq),
        in_specs=[pl.BlockSpec((tq, HEAD_DIM), lambda b, h, i: (b * nq + i, h)),
                  pl.BlockSpec((T, HEAD_DIM), lambda b, h, i: (b, H + h)),
                  pl.BlockSpec((T, HEAD_DIM), lambda b, h, i: (b, 2 * H + h))],
        out_specs=pl.BlockSpec((tq, HEAD_DIM), lambda b, h, i: (b * nq + i, h)),
        out_shape=jax.ShapeDtypeStruct((B * T, H * HEAD_DIM), BF16),
        scratch_shapes=[pltpu.VMEM((-(-(T // MOBA_BLOCK) // 8) * 8, HEAD_DIM), F32),
                        pltpu.VMEM((HEAD_DIM, T), BF16)], name="moba_prompt",
        compiler_params=_cparams("parallel", "parallel", "arbitrary"))(z, z, z)


ROWS_PER_KV = 2 * NSA_KV_HEADS
CHUNKS_PER_PAGE = PAGE // CMP_STRIDE


def _compress_kernel(pt_ref, *refs, ppg, npp):
    del pt_ref
    x_refs = refs[:npp]
    w1_ref, w2_ref, pe_ref, o_ref, xcat_ref, carry_ref = refs[npp:]
    s = pl.program_id(1)
    slot = s % (ppg // npp)
    V = ROWS_PER_KV
    for r in range(npp):
        base = pl.multiple_of((slot * npp + r) * (CHUNKS_PER_PAGE * V), CHUNKS_PER_PAGE * V)
        for n in range(CHUNKS_PER_PAGE):
            for i in range(CMP_STRIDE):
                row = (n * CMP_STRIDE + i) * V
                xcat_ref[pl.ds(base + n * V, V), i * HEAD_DIM:(i + 1) * HEAD_DIM] = x_refs[r][row:row + V, :]

    @pl.when(s == 0)
    def _():
        carry_ref[...] = jnp.zeros(carry_ref.shape, F32)

    @pl.when(slot == ppg // npp - 1)
    def _():
        rows = ppg * CHUNKS_PER_PAGE * V
        H = HEAD_DIM
        is_v = (_iota((rows, H), 0) % V) >= NSA_KV_HEADS
        pick = lambda y: jnp.where(is_v, y[:, y.shape[1] // 2:y.shape[1] // 2 + H], y[:, :H])
        w1 = w1_ref[...]
        pb = _dot(pe_ref[...].astype(BF16), w1)
        bias = jnp.where(is_v, pb[2:3, 2 * H:3 * H] + pb[3:4, 3 * H:], pb[0:1, :H] + pb[1:2, H:2 * H])
        part = _dot(xcat_ref[...].astype(BF16), w1)
        p0 = pick(part)
        p1 = jnp.where(is_v, part[:, 3 * H:], part[:, H:2 * H])
        p0_prev = jnp.concatenate([carry_ref[...], p0[:rows - V]], axis=0)
        pre = bias + p0_prev + p1
        o_ref[...] = pick(_dot((pre * jax.nn.sigmoid(pre)).astype(BF16), w2_ref[...]))
        carry_ref[...] = p0[rows - V:]


def _compress(rows, page_table, w1, w2, pe):
    B, P = page_table.shape
    ppg = min(32, P)
    npp = min(4, ppg)
    assert P % ppg == 0 and ppg % npp == 0
    half = CMP_LEN * HEAD_DIM // 2
    w1cat = jnp.concatenate([w1[0, :half], w1[0, half:], w1[1, :half], w1[1, half:]], axis=1).astype(BF16)
    w2cat = jnp.concatenate([w2[0], w2[1]], axis=1).astype(BF16)
    pe4 = jnp.concatenate([pe.reshape(4, half), jnp.zeros((4, half), F32)], axis=0)
    grp = ppg * CHUNKS_PER_PAGE * ROWS_PER_KV
    page = [pl.BlockSpec((None, PAGE * ROWS_PER_KV, HEAD_DIM), lambda b, s, pt, r=r: (pt[b, s * npp + r], 0, 0))
            for r in range(npp)]
    const = lambda shape: pl.BlockSpec(shape, lambda b, s, pt: (0,) * len(shape))
    gs = pltpu.PrefetchScalarGridSpec(
        num_scalar_prefetch=1, grid=(B, P // npp),
        in_specs=page + [const(w1cat.shape), const(w2cat.shape), const(pe4.shape)],
        out_specs=pl.BlockSpec((None, grp, HEAD_DIM), lambda b, s, pt: (b, s // (ppg // npp), 0)),
        scratch_shapes=[pltpu.VMEM((grp, CMP_STRIDE * HEAD_DIM), F32), pltpu.VMEM((ROWS_PER_KV, HEAD_DIM), F32)])
    out = pl.pallas_call(
        functools.partial(_compress_kernel, ppg=ppg, npp=npp), grid_spec=gs,
        out_shape=jax.ShapeDtypeStruct((B, P * CHUNKS_PER_PAGE * ROWS_PER_KV, HEAD_DIM), F32), name="nsa_compress",
        compiler_params=_cparams("parallel", "arbitrary"))(page_table, *([rows] * npp), w1cat, w2cat, pe4)
    return out.reshape(B, P * CHUNKS_PER_PAGE, ROWS_PER_KV, HEAD_DIM)


def _overlap_shifted(nm, nb, width):
    nc = nm - 1
    m = np.zeros((nm, width), np.float32)
    j = np.arange(nb)
    for a in range(SLC_LEN // CMP_STRIDE):
        for b in range(CMP_LEN // CMP_STRIDE):
            i = (SLC_LEN // CMP_STRIDE) * j + a - b
            ok = (i >= 0) & (i < nc)
            np.add.at(m, (i[ok] + 1, j[ok]), 1.0)
    return jnp.asarray(m, BF16)


def _slc_scores(imp, qpos_blk, lane):
    lag = qpos_blk - lane
    valid = lag >= 0
    forced = (lane == 0) | (valid & (lag < SLC_LOCAL))
    return jnp.where(valid, jnp.where(forced, POS_INF, imp), NEG_INF)


def _nsa_prompt_kernel(q_ref, kc_ref, vc_ref, ks_ref, vs_ref, kw_ref, vw_ref, gt_ref, ovl_ref, et_ref, o_ref,
                       vst_ref, vwt_ref, *, tq, tk, nm, T):
    G = NSA_GROUP
    hkv = pl.program_id(1)
    qi = pl.program_id(2)
    q0 = pl.multiple_of(qi * tq, tq)

    @pl.when(qi == 0)
    def _():
        _transpose_into(vst_ref, vs_ref, T, BF16)
        _transpose_into(vwt_ref, vw_ref, T, BF16)

    qs = jnp.concatenate([q_ref[:, g * HEAD_DIM:(g + 1) * HEAD_DIM] for g in range(G)], axis=0).astype(BF16)

    def rep(x):
        return jnp.concatenate([x] * G, axis=1)

    s = _nt(kc_ref[...].astype(BF16), qs) * SCALE
    m_i = _iota((nm, tq), 0)
    c_ok = jnp.where((m_i >= 1) & ((m_i - 1) * CMP_STRIDE + (CMP_LEN - 1) <= q0 + _iota((nm, tq), 1)), 1.0, 0.0)
    p_cmp = _masked_softmax(s, rep(c_ok) > 0.5, axis=0).astype(BF16)
    o_cmp = _dot(vc_ref[...].T.astype(BF16), p_cmp)
    pov = _dot(ovl_ref[...], p_cmp)
    imp = pov[:, 0:tq]
    for g in range(1, G):
        imp = imp + pov[:, g * tq:(g + 1) * tq]
    blk = _iota(imp.shape, 0)
    qblk = (q0 + _iota(imp.shape, 1)) // SLC_LEN
    sel_b = _top_k_mask(_slc_scores(imp, qblk, blk), SLC_TOP, axis=0).astype(BF16)

    krow = _iota((tk, tq), 0)
    qpos = q0 + _iota((tk, tq), 1)

    def slc_body(j, carry):
        k0 = pl.multiple_of(j * tk, tk)
        k = ks_ref[pl.ds(k0, tk), :].astype(BF16)
        sel_e = _dot(et_ref[pl.ds(k0, tk), :], sel_b)
        bias = jnp.where((k0 + krow) <= qpos, (sel_e - 1.0) * -NEG, NEG)
        return _online_update_t(_nt(k, qs) * QK_LOG2 + rep(bias), *carry, vst_ref[:, pl.ds(k0, tk)])

    _, l, acc = lax.fori_loop(0, (q0 + tq + tk - 1) // tk, slc_body, _flash_init_t(G * tq))
    o_slc = _flash_finish(l, acc)

    nw = WINDOW + tq
    w0 = pl.multiple_of(jnp.maximum(q0 - WINDOW, 0), tq)
    d = (q0 + _iota((nw, tq), 1)) - (w0 + _iota((nw, tq), 0))
    t = _nt(kw_ref[pl.ds(w0, nw), :].astype(BF16), qs) * QK_LOG2 + rep(_mask_bias((d >= 0) & (d < WINDOW)))
    _, l, acc = _online_update_t(t, *_flash_init_t(G * tq), vwt_ref[:, pl.ds(w0, nw)])
    o_win = _flash_finish(l, acc)

    for g in range(G):
        col = (hkv * G + g) * N_BRANCH
        c = slice(g * tq, (g + 1) * tq)
        o = (o_cmp[:, c] * gt_ref[pl.ds(col, 1), :] + o_slc[:, c] * gt_ref[pl.ds(col + 1, 1), :]
             + o_win[:, c] * gt_ref[pl.ds(col + 2, 1), :])
        o_ref[:, g * HEAD_DIM:(g + 1) * HEAD_DIM] = o.T.astype(o_ref.dtype)


def _nsa_prompt_attn(z, comp, gates_t, B, T, *, tq=512, tk=512):
    tk = min(tk, T)
    assert T % tk == 0 and WINDOW % tq == 0 and T >= WINDOW + tq and T % 512 == 0
    nq = T // tq
    nm = comp.shape[2]
    G = NSA_GROUP
    qw = G * HEAD_DIM
    kvb = N_HEADS
    nb = T // SLC_LEN
    nbr = -(-nb // 8) * 8
    ovl = _overlap_shifted(nm, nb, nbr).T
    et = jnp.asarray((np.arange(T)[:, None] // SLC_LEN) == np.arange(nbr)[None, :], BF16)

    def kv_spec(off):
        return pl.BlockSpec((T, HEAD_DIM), lambda b, h, i: (b, kvb + off + h))

    return pl.pallas_call(
        functools.partial(_nsa_prompt_kernel, tq=tq, tk=tk, nm=nm, T=T), grid=(B, NSA_KV_HEADS, nq),
        in_specs=[pl.BlockSpec((tq, qw), lambda b, h, i: (b * nq + i, h)),
                  pl.BlockSpec((None, None, nm, HEAD_DIM), lambda b, h, i: (b, h, 0, 0)),
                  pl.BlockSpec((None, None, nm, HEAD_DIM), lambda b, h, i: (b, NSA_KV_HEADS + h, 0, 0)),
                  kv_spec(8), kv_spec(12), kv_spec(16), kv_spec(20),
                  pl.BlockSpec((LANES, tq), lambda b, h, i: (0, b * nq + i)),
                  pl.BlockSpec((nbr, nm), lambda b, h, i: (0, 0)),
                  pl.BlockSpec((T, nbr), lambda b, h, i: (0, 0))],
        out_specs=pl.BlockSpec((tq, qw), lambda b, h, i: (b * nq + i, h)),
        out_shape=jax.ShapeDtypeStruct((B * T, N_HEADS * HEAD_DIM), BF16),
        scratch_shapes=[pltpu.VMEM((HEAD_DIM, T), BF16), pltpu.VMEM((HEAD_DIM, T), BF16)], name="nsa_prompt",
        compiler_params=_cparams("parallel", "parallel", "arbitrary"))(z, comp, comp, z, z, z, z, gates_t, ovl, et)


NSA_Q = N_HEADS * HEAD_DIM
NSA_KV = NSA_KV_HEADS * HEAD_DIM
NSA_MAIN = NSA_Q + 2 * N_BRANCH * NSA_KV
HD = N_HEADS * HEAD_DIM


def _pad_cols(w, width=LANES):
    return jnp.pad(w, ((0, 0), (0, width - w.shape[1])))


def _nsa_rope_flags(tn=512):
    per = NSA_KV // tn
    flags = [1] * (NSA_Q // tn)
    for _ in range(N_BRANCH):
        flags += [1] * per + [0] * per
    return jnp.asarray(flags, I32)


def _nsa_project(x2d, g, w_in, pos_tables, tm, kv_rows=False):
    assert NSA_KV == 512
    z = _norm_mm(x2d, g, w_in, NSA_MAIN, tm=tm, epi="rope", rope=(_nsa_rope_flags(),) + pos_tables,
                 kv_rows=(NSA_Q // NSA_KV, N_BRANCH) if kv_rows else None)
    gates = _norm_mm(x2d, g, _pad_cols(_w_cols(w_in, NSA_MAIN)), LANES, tm=tm, epi="sigmoid")
    return z, gates


def _nsa_prompt_mixer(x2d, g, w_in, pe, w1, w2, B, T, tm):
    (z, kv), gates = _nsa_project(x2d, g, w_in, _rope_tables(jnp.arange(T, dtype=I32)), tm, kv_rows=True)
    rows = kv[0].reshape(B * T // PAGE, PAGE * ROWS_PER_KV, HEAD_DIM)
    pt = jnp.arange(B * T // PAGE, dtype=I32).reshape(B, T // PAGE)
    comp = _compress(rows, pt, w1, w2, pe).transpose(0, 2, 1, 3)
    o = _nsa_prompt_attn(z, comp, gates.T, B, T)
    st = kv.reshape(N_BRANCH, B, T, 2, NSA_KV_HEADS, HEAD_DIM)
    return o, (st[0], st[1], st[2][:, T - min(WINDOW, T):])


def _moba_rope_flags(tn=512):
    return jnp.asarray([1] * (2 * HD // tn) + [0] * (HD // tn), I32)


def _moba_prompt_mixer(x2d, g, w_in, B, T, tm):
    z = _norm_mm(x2d, g, w_in, 3 * HD, tm=tm, epi="rope",
                 rope=(_moba_rope_flags(),) + _rope_tables(jnp.arange(T, dtype=I32)))
    o = _moba_prompt_attn(z, B, T)
    return o, (z[:, HD:].reshape(B, T, 2, N_HEADS, HEAD_DIM),)


def _fox_project(x2d, g, w_in, b_f, tm):
    z = _norm_mm(x2d, g, w_in, 3 * HD, tm=tm)
    logf = _norm_mm(x2d, g, _pad_cols(_w_cols(w_in, 3 * HD)), LANES, tm=tm, epi="logsig",
                    bias=jnp.pad(b_f, (0, LANES - N_HEADS)))
    return z, logf


def _fox_prompt_mixer(x2d, g, w_in, b_f, B, T, tm):
    z, logf = _fox_project(x2d, g, w_in, b_f, tm)
    log_f = logf[:, :N_HEADS].reshape(B, T, N_HEADS)
    ct = _cumsum_rows(log_f.transpose(0, 2, 1))
    c = _pad_cols(ct.transpose(0, 2, 1).reshape(B * T, N_HEADS))
    o = _fox_prompt_attn(z, c, ct, B, T)
    return o, (z[:, HD:].reshape(B, T, 2, N_HEADS, HEAD_DIM), log_f)


def _ffn(x2d, g, wi, wo, tm):
    u = _swiglu_up(x2d, g, wi, tm=tm)
    return _mm_res(u, wo, x2d, 0.5, tm=tm, tn=FFN_DOWN_TN)


def _diag_blocks(acc, nblk):
    rblk = _iota((acc.shape[0], HEAD_DIM), 0) % nblk
    out = jnp.zeros((acc.shape[0], HEAD_DIM), F32)
    for j in range(nblk):
        out = out + jnp.where(rblk == j, acc[:, j * HEAD_DIM:(j + 1) * HEAD_DIM], 0.0)
    return out


def _row_query(rows, mode):
    r = _iota((rows, LANES), 0)
    return (r % N_HEADS) // NSA_KV_HEADS if mode == "nsa" else r // N_HEADS


def _paged_attn_kernel(pt_ref, q_ref, hm_ref, hmn_ref, *rest, mode, nh, npp, n_steps):
    del pt_ref
    joint = mode == "nsa"
    if joint:
        xk_refs = xv_refs = rest[:npp]
        kn_ref = vn_ref = rest[npp]
        rest = rest[npp + 1:]
    else:
        xk_refs, xv_refs = rest[:npp], rest[npp:2 * npp]
        kn_ref, vn_ref = rest[2 * npp:2 * npp + 2]
        rest = rest[2 * npp + 2:]
    if mode == "fox":
        cq_ref, ck_ref, ckn_ref, o_ref, m_ref, l_ref, acc_ref = rest
    else:
        sel_ref, o_ref, m_ref, l_ref, acc_ref = rest
    p = pl.program_id(1)
    R = q_ref.shape[0]
    q = q_ref[...]
    roff = cq_ref[:, 0:1] * LOG2E if mode == "fox" else None
    pv_roll = nh if joint else 0

    def rows_of(refs):
        xs = [r[...].reshape(-1, HEAD_DIM).astype(BF16) for r in refs]
        return xs[0] if len(xs) == 1 else jnp.concatenate(xs, axis=0)

    def tiles(k_refs, v_refs):
        k = rows_of(k_refs)
        return k, (k if joint else rows_of(v_refs))

    def sel_bias(blk):
        if isinstance(blk, int):
            tile = sel_ref[:, (blk // LANES) * LANES:(blk // LANES + 1) * LANES]
        else:
            tile = sel_ref[:, pl.ds(pl.multiple_of((blk // LANES) * LANES, LANES), LANES)]
        return _mask_bias(_lane_col(tile, blk % LANES) > 0.5), _mask_bias(_lane_col(tile, blk % LANES + 1) > 0.5)

    def update(t, v, row_off):
        m, l, acc = _online_update(t, m_ref[...], l_ref[...], acc_ref[...], v, roff=row_off, pv_roll=pv_roll)
        m_ref[...] = m
        l_ref[...] = l
        acc_ref[...] = acc

    @pl.when(p == 0)
    def _():
        m_ref[...] = jnp.full(m_ref.shape, NEG, F32)
        l_ref[...] = jnp.zeros(l_ref.shape, F32)
        acc_ref[...] = jnp.zeros(acc_ref.shape, F32)
        kn, vn = tiles((kn_ref,), (vn_ref,))
        tn = _nt(q, kn) * QK_LOG2 + hmn_ref[...]
        if mode == "fox":
            tn = tn - ckn_ref[...] * LOG2E
        elif mode == "nsa":
            tn = tn + sel_bias(n_steps * npp * (PAGE // SLC_LEN))[0]
        update(tn, vn, roff)

    k, v = tiles(xk_refs, xv_refs)
    t = _nt(q, k) * QK_LOG2 + hm_ref[...]
    if mode == "fox":
        update(t - ck_ref[...] * LOG2E, v, roff)
    else:
        blk_len = MOBA_BLOCK if mode == "moba" else SLC_LEN
        width = blk_len * (2 * nh if joint else nh)
        parts = []
        for j in range(0, npp * PAGE // blk_len, 2):
            b0, b1 = sel_bias(p * (npp * PAGE // blk_len) + j)
            parts += [jnp.broadcast_to(b0, (R, width)), jnp.broadcast_to(b1, (R, width))]
        update(t + jnp.concatenate(parts, axis=1), v, None)

    @pl.when(p == n_steps - 1)
    def _():
        o_ref[...] = _flash_finish(l_ref[...], acc_ref[...]).astype(o_ref.dtype)


def _head_mask(rows, cols, vecs, nh, causal_q=None):
    r = np.arange(rows)[:, None]
    c = np.arange(cols)[None, :]
    ok = (c % vecs) == (r % nh)
    if causal_q is not None:
        ok &= (c // vecs) <= causal_q(r)
    return jnp.asarray(np.where(ok, 0.0, NEG), F32)


def _paged_attn(q_rows, pools, page_table, new_rows, extras, *, mode, out_dtype):
    B, R, _ = q_rows.shape
    P = page_table.shape[1]
    joint = mode == "nsa"
    nh = NSA_KV_HEADS if joint else N_HEADS
    vecs = 2 * nh if joint else nh
    n_new = new_rows[0].shape[1]
    q_of_row = (lambda r: (r % N_HEADS) // NSA_KV_HEADS) if joint else (lambda r: r // N_HEADS)
    npp = min(4, P)
    assert P % npp == 0 and (npp * PAGE) % (2 * MOBA_BLOCK) == 0
    hm = _head_mask(R, npp * PAGE * vecs, vecs, nh)
    hmn = _head_mask(R, n_new, vecs, nh, causal_q=q_of_row)
    const = lambda shape: pl.BlockSpec(shape, lambda b, p, pt: (0,) * len(shape))
    per_b = lambda shape: pl.BlockSpec((None,) + shape, lambda b, p, pt: (b,) + (0,) * len(shape))
    in_specs = [per_b((R, HEAD_DIM)), const(hm.shape), const(hmn.shape)]
    if joint:
        in_specs += [pl.BlockSpec((None, PAGE * vecs, HEAD_DIM), lambda b, p, pt, r=r: (pt[b, p * npp + r], 0, 0))
                     for r in range(npp)]
        in_specs += [per_b((n_new, HEAD_DIM))]
        args = (pools[0],) * npp + (new_rows[0],)
    else:
        for c in range(2):
            in_specs += [pl.BlockSpec((None, PAGE, None, nh, HEAD_DIM),
                                      lambda b, p, pt, r=r, c=c: (pt[b, p * npp + r], 0, c, 0, 0)) for r in range(npp)]
        in_specs += [per_b((n_new, HEAD_DIM)), per_b((n_new, HEAD_DIM))]
        args = (pools[0],) * (2 * npp) + (new_rows[0], new_rows[1])
    if mode == "fox":
        cq, ck = extras
        in_specs += [per_b((R, LANES)),
                     pl.BlockSpec((None, 1, npp * PAGE * nh), lambda b, p, pt: (b, 0, p)),
                     pl.BlockSpec((None, 1, n_new), lambda b, p, pt: (b, 0, P * PAGE * nh // n_new))]
        args += (cq, ck, ck)
    else:
        (sel,) = extras
        in_specs += [per_b((R, sel.shape[2]))]
        args += (sel,)
    gs = pltpu.PrefetchScalarGridSpec(
        num_scalar_prefetch=1, grid=(B, P // npp), in_specs=in_specs,
        out_specs=pl.BlockSpec((None, R, HEAD_DIM), lambda b, p, pt: (b, 0, 0)),
        scratch_shapes=[pltpu.VMEM((R, 1), F32), pltpu.VMEM((R, 1), F32), pltpu.VMEM((R, HEAD_DIM), F32)])
    return pl.pallas_call(
        functools.partial(_paged_attn_kernel, mode=mode, nh=nh, npp=npp, n_steps=P // npp), grid_spec=gs,
        out_shape=jax.ShapeDtypeStruct((B, R, HEAD_DIM), out_dtype), name="paged_attn_" + mode,
        compiler_params=_cparams("parallel", "arbitrary"))(page_table, q_rows, hm, hmn, *args)


PAGES_PER_STEP = 8


def _cumsum_paged_kernel(pt_ref, *refs, n_groups):
    del pt_ref
    x_refs = refs[:PAGES_PER_STEP]
    xn_ref, o_ref, carry_ref = refs[PAGES_PER_STEP:]
    g = pl.program_id(1)

    @pl.when(g == 0)
    def _():
        carry_ref[...] = jnp.zeros(carry_ref.shape, F32)

    carry = carry_ref[...]
    for r in range(PAGES_PER_STEP):
        tail = xn_ref[...] if r == 0 else jnp.zeros(xn_ref.shape, F32)
        cs = _tile_cumsum(jnp.where(g == n_groups, tail, x_refs[r][...])) + carry
        o_ref[:, r * PAGE:(r + 1) * PAGE] = cs
        carry = jnp.broadcast_to(cs[:, LANES - 1:LANES], carry.shape)
    carry_ref[...] = carry


def _cumsum_paged(pool_t, page_table, new_t):
    B, P = page_table.shape
    H = pool_t.shape[1]
    n = PAGES_PER_STEP
    assert P % n == 0
    ng = P // n
    page = [pl.BlockSpec((None, H, PAGE), lambda b, g, pt, r=r: (pt[b, jnp.minimum(g, ng - 1) * n + r], 0, 0))
            for r in range(n)]
    gs = pltpu.PrefetchScalarGridSpec(
        num_scalar_prefetch=1, grid=(B, ng + 1),
        in_specs=page + [pl.BlockSpec((None, H, PAGE), lambda b, g, pt: (b, 0, 0))],
        out_specs=pl.BlockSpec((None, H, n * PAGE), lambda b, g, pt: (b, 0, g)),
        scratch_shapes=[pltpu.VMEM((H, LANES), F32)])
    return pl.pallas_call(
        functools.partial(_cumsum_paged_kernel, n_groups=ng), grid_spec=gs,
        out_shape=jax.ShapeDtypeStruct((B, H, (ng + 1) * n * PAGE), F32), name="cumsum_paged",
        compiler_params=_cparams("parallel", "arbitrary"))(page_table, *([pool_t] * n), new_t)


def _kmeans_kernel(pt_ref, *refs, ppb):
    del pt_ref
    x_refs, o_ref = refs[:PAGES_PER_STEP], refs[PAGES_PER_STEP]
    for blk in range(PAGES_PER_STEP // ppb):
        tot = jnp.sum(x_refs[blk * ppb][...], axis=0)
        for r in range(1, ppb):
            tot = tot + jnp.sum(x_refs[blk * ppb + r][...], axis=0)
        o_ref[blk] = tot * (1.0 / MOBA_BLOCK)


def _moba_kmeans(pool5, page_table):
    B, P = page_table.shape
    ppb = MOBA_BLOCK // PAGE
    n = PAGES_PER_STEP
    assert P % n == 0 and n % ppb == 0
    page = [pl.BlockSpec((None, PAGE, None, N_HEADS, HEAD_DIM), lambda b, g, pt, r=r: (pt[b, g * n + r], 0, 0, 0, 0))
            for r in range(n)]
    gs = pltpu.PrefetchScalarGridSpec(
        num_scalar_prefetch=1, grid=(B, P // n), in_specs=page,
        out_specs=pl.BlockSpec((None, n // ppb, N_HEADS, HEAD_DIM), lambda b, g, pt: (b, g, 0, 0)))
    return pl.pallas_call(
        functools.partial(_kmeans_kernel, ppb=ppb), grid_spec=gs,
        out_shape=jax.ShapeDtypeStruct((B, P // ppb, N_HEADS, HEAD_DIM), F32), name="moba_kmeans",
        compiler_params=_cparams("parallel", "parallel"))(page_table, *([pool5] * n))


def _moba_select_kernel(q_ref, km_ref, o_ref, *, n_past_blocks):
    gate = _nt(q_ref[...], km_ref[...].astype(BF16))
    lane = _iota(gate.shape, 1)
    gate = jnp.where(lane < n_past_blocks, gate, NEG_INF)
    o_ref[...] = _top_k_mask(gate, MOBA_TOP, finite_only=True)


def _moba_select(qbd, kmeans, n_past_blocks):
    B, R, C = qbd.shape
    return pl.pallas_call(
        functools.partial(_moba_select_kernel, n_past_blocks=n_past_blocks), grid=(B,),
        in_specs=[pl.BlockSpec((None, R, C), lambda b: (b, 0, 0)), pl.BlockSpec((None, LANES, C), lambda b: (b, 0, 0))],
        out_specs=pl.BlockSpec((None, R, LANES), lambda b: (b, 0, 0)),
        out_shape=jax.ShapeDtypeStruct((B, R, LANES), F32), name="moba_select",
        compiler_params=_cparams("parallel"))(qbd, kmeans)


def _nsa_cmp_sample_kernel(q_ref, kc_ref, vc_ref, ovl_ref, o_ref, sel_ref, *, past):
    R = q_ref.shape[0]
    nm = kc_ref.shape[0]
    G = NSA_GROUP
    s = _nt(q_ref[...], kc_ref[...].astype(BF16)) * SCALE
    m_i = _iota((R, nm), 1)
    qpos = past + (_iota((R, nm), 0) % N_HEADS) // NSA_KV_HEADS
    ok = (m_i >= 1) & ((m_i - 1) * CMP_STRIDE + (CMP_LEN - 1) <= qpos)
    p_cmp = _masked_softmax(s, ok).astype(BF16)
    o_ref[...] = _diag_blocks(_dot(p_cmp, vc_ref[...].astype(BF16)), NSA_KV_HEADS)
    pov = _dot(p_cmp, ovl_ref[...])
    rg = R // G
    imp = pov[0:rg]
    for g in range(1, G):
        imp = imp + pov[g * rg:(g + 1) * rg]
    lane = _iota(imp.shape, 1)
    qblk = (past + _iota(imp.shape, 0) // NSA_KV_HEADS) // SLC_LEN
    sel = _top_k_mask(_slc_scores(imp, qblk, lane), SLC_TOP)
    sel_ref[...] = jnp.concatenate([sel] * G, axis=0)


def _nsa_cmp_sample(qbd, comp, past):
    B, R, C = qbd.shape
    nm = comp.shape[1]
    nb = -(-(past + R // N_HEADS) // SLC_LEN)
    width = -(-nb // LANES) * LANES
    ovl = _overlap_shifted(nm, nb, width)
    return pl.pallas_call(
        functools.partial(_nsa_cmp_sample_kernel, past=past), grid=(B,),
        in_specs=[pl.BlockSpec((None, R, C), lambda b: (b, 0, 0)),
                  pl.BlockSpec((None, nm, C), lambda b: (b, 0, 0)),
                  pl.BlockSpec((None, nm, C), lambda b: (b, 0, 1)),
                  pl.BlockSpec((nm, width), lambda b: (0, 0))],
        out_specs=[pl.BlockSpec((None, R, HEAD_DIM), lambda b: (b, 0, 0)),
                   pl.BlockSpec((None, R, width), lambda b: (b, 0, 0))],
        out_shape=[jax.ShapeDtypeStruct((B, R, HEAD_DIM), F32), jax.ShapeDtypeStruct((B, R, width), F32)],
        name="nsa_cmp_sample", compiler_params=_cparams("parallel"))(qbd, comp, comp, ovl)


def _nsa_win_sample_kernel(q_ref, kw_ref, vw_ref, oc_ref, os_ref, g_ref, o_ref, *, past, wb):
    R = q_ref.shape[0]
    nk = kw_ref.shape[0]
    s = _nt(q_ref[...], kw_ref[...].astype(BF16)) * SCALE
    j = _iota((R, nk), 1)
    qpos = past + (_iota((R, nk), 0) % N_HEADS) // NSA_KV_HEADS
    wpos = past - wb + j
    d = qpos - wpos
    ok = (d >= 0) & (d < WINDOW) & (wpos >= 0)
    p = _masked_softmax(s, ok).astype(BF16)
    o_win = _diag_blocks(_dot(p, vw_ref[...].astype(BF16)), NSA_KV_HEADS)
    g = g_ref[...]
    o = oc_ref[...] * g[:, 0:1] + os_ref[...] * g[:, 1:2] + o_win * g[:, 2:3]
    o_ref[...] = o.astype(o_ref.dtype)


def _nsa_win_sample(qbd, kw, o_cmp, o_slc, gates, past, wb):
    B, R, C = qbd.shape
    nk = kw.shape[1]
    row = pl.BlockSpec((None, R, HEAD_DIM), lambda b: (b, 0, 0))
    return pl.pallas_call(
        functools.partial(_nsa_win_sample_kernel, past=past, wb=wb), grid=(B,),
        in_specs=[pl.BlockSpec((None, R, C), lambda b: (b, 0, 0)),
                  pl.BlockSpec((None, nk, C), lambda b: (b, 0, 0)),
                  pl.BlockSpec((None, nk, C), lambda b: (b, 0, 1)),
                  row, row, row],
        out_specs=row, out_shape=jax.ShapeDtypeStruct((B, R, HEAD_DIM), BF16), name="nsa_win_sample",
        compiler_params=_cparams("parallel"))(qbd, kw, kw, o_cmp, o_slc, gates)


def _qbd_heads(q):
    B, Q, H, dh = q.shape
    x = q[:, :, :, None, :] * jnp.eye(H, dtype=q.dtype)[None, None, :, :, None]
    return x.reshape(B, Q * H, H * dh).astype(BF16)


def _qbd_groups(q):
    B, Q, H, dh = q.shape
    x = q.reshape(B, Q, NSA_KV_HEADS, NSA_GROUP, dh).transpose(0, 3, 1, 2, 4)
    x = x[:, :, :, :, None, :] * jnp.eye(NSA_KV_HEADS, dtype=q.dtype)[None, None, None, :, :, None]
    return x.reshape(B, NSA_GROUP * Q * NSA_KV_HEADS, NSA_KV_HEADS * dh).astype(BF16)


def _pad_rows(x, rows):
    return jnp.pad(x, ((0, 0), (0, rows - x.shape[1]), (0, 0)))


def _new_kv_rows(new):
    B, Q, _, H, dh = new.shape
    return (_pad_rows(new[:, :, 0].reshape(B, Q * H, dh), LANES), _pad_rows(new[:, :, 1].reshape(B, Q * H, dh), LANES))


def _sample_tables(B, Q, past):
    return _rope_tables(jnp.tile(past + jnp.arange(Q, dtype=I32), B))


def _nsa_sample_mixer(x2d, g, w_in, pe, w1, w2, pool_cmp, pool_slc, win_buf, page_table, layer, B, Q):
    P = page_table.shape[1]
    past = P * PAGE
    assert past % SLC_LEN == 0 and Q * N_HEADS % 8 == 0
    n_pool = pool_cmp.shape[1]
    pt = page_table + layer * n_pool
    z, gates = _nsa_project(x2d, g, w_in, _sample_tables(B, Q, past), B * Q)
    kv = [z[:, NSA_Q + 2 * NSA_KV * br:NSA_Q + 2 * NSA_KV * (br + 1)].reshape(B, Q, 2 * NSA_KV) for br in range(N_BRANCH)]
    assert (past + Q - CMP_LEN) // CMP_STRIDE + 1 == past // CMP_STRIDE - 1
    comp = _compress(pool_cmp.reshape(-1, PAGE * ROWS_PER_KV, HEAD_DIM), pt, w1, w2, pe)
    comp = comp.reshape(B, -1, ROWS_PER_KV * HEAD_DIM)
    q4 = z[:, :NSA_Q].reshape(B, Q, N_HEADS, HEAD_DIM)
    qbd = _qbd_groups(q4)
    o_cmp, sel = _nsa_cmp_sample(qbd, comp, past)
    q_rows = q4.reshape(B, Q, NSA_KV_HEADS, NSA_GROUP, HEAD_DIM).transpose(0, 3, 1, 2, 4)
    q_rows = q_rows.reshape(B, Q * N_HEADS, HEAD_DIM).astype(BF16)
    x_new = _pad_rows(kv[1].reshape(B, Q * ROWS_PER_KV, HEAD_DIM), LANES)
    o_slc = _paged_attn(q_rows, (pool_slc.reshape(-1, PAGE * ROWS_PER_KV, HEAD_DIM),), pt, (x_new,), (sel,),
                        mode="nsa", out_dtype=F32)
    wb = win_buf.shape[2]
    kw = jnp.concatenate([win_buf[layer].reshape(B, wb, 2 * NSA_KV), kv[2]], axis=1)
    g_rows = gates[:, :N_HEADS * N_BRANCH].reshape(B, Q, NSA_KV_HEADS, NSA_GROUP, N_BRANCH)
    g_rows = jnp.pad(g_rows.transpose(0, 3, 1, 2, 4).reshape(B, Q * N_HEADS, N_BRANCH), ((0, 0), (0, 0), (0, LANES - N_BRANCH)))
    o = _nsa_win_sample(qbd, _pad_rows(kw, -(-(wb + Q) // LANES) * LANES), o_cmp, o_slc, g_rows, past, wb)
    o = o.reshape(B, NSA_GROUP, Q, NSA_KV_HEADS, HEAD_DIM).transpose(0, 2, 3, 1, 4).reshape(B * Q, HD)
    shp = (B, -1, 2, NSA_KV_HEADS, HEAD_DIM)
    return o, (kv[0].reshape(shp), kv[1].reshape(shp), kw[:, Q:].reshape(shp))


def _moba_sample_mixer(x2d, g, w_in, pool, page_table, layer, B, Q):
    P = page_table.shape[1]
    past = P * PAGE
    assert past % MOBA_BLOCK == 0 and Q <= MOBA_BLOCK and past // MOBA_BLOCK <= LANES
    n_pool = pool.shape[1]
    pt = page_table + layer * n_pool
    z = _norm_mm(x2d, g, w_in, 3 * HD, tm=B * Q, epi="rope", rope=(_moba_rope_flags(),) + _sample_tables(B, Q, past))
    q4 = z[:, :HD].reshape(B, Q, N_HEADS, HEAD_DIM)
    nbp = past // MOBA_BLOCK
    pool5 = pool.reshape(-1, PAGE, 2, N_HEADS, HEAD_DIM)
    km = _moba_kmeans(pool5, pt).reshape(B, nbp, HD)
    sel = _moba_select(_qbd_heads(q4), _pad_rows(km, LANES), nbp)
    new = z[:, HD:].reshape(B, Q, 2, N_HEADS, HEAD_DIM)
    o = _paged_attn(q4.reshape(B, Q * N_HEADS, HEAD_DIM).astype(BF16), (pool5,), pt, _new_kv_rows(new), (sel,),
                    mode="moba", out_dtype=BF16)
    return o.reshape(B * Q, HD), (new,)


def _fox_sample_mixer(x2d, g, w_in, b_f, pool, pool_logf, page_table, layer, B, Q):
    P = page_table.shape[1]
    n_pool = pool.shape[1]
    pt = page_table + layer * n_pool
    z, logf = _fox_project(x2d, g, w_in, b_f, B * Q)
    log_f = logf[:, :N_HEADS].reshape(B, Q, N_HEADS)
    pool_t = pool_logf.astype(F32).transpose(0, 1, 3, 2).reshape(-1, N_HEADS, PAGE)
    new_t = jnp.pad(log_f.transpose(0, 2, 1), ((0, 0), (0, 0), (0, PAGE - Q)))
    ct = _cumsum_paged(pool_t, pt, new_t)
    cq = ct[:, :, P * PAGE:P * PAGE + Q].transpose(0, 2, 1).reshape(B, Q * N_HEADS, 1)
    ck = ct.transpose(0, 2, 1).reshape(B, 1, -1)
    q_rows = z[:, :HD].reshape(B, Q * N_HEADS, HEAD_DIM).astype(BF16)
    new = z[:, HD:].reshape(B, Q, 2, N_HEADS, HEAD_DIM)
    o = _paged_attn(q_rows, (pool.reshape(-1, PAGE, 2, N_HEADS, HEAD_DIM),), pt, _new_kv_rows(new),
                    (jnp.broadcast_to(cq, (B, Q * N_HEADS, LANES)), ck), mode="fox", out_dtype=BF16)
    return o.reshape(B * Q, HD), (new, log_f)


PROMPT_TM = 1024
FFN_DOWN_TN = 256


def kernel(x_prompt, x_sample, cache_nsa_cmp, cache_nsa_slc, cache_nsa_win, cache_moba_kv, cache_fox_kv,
           cache_fox_logf, page_table, norms, ffn_wi, ffn_wo, final_norm, nsa_w_in, nsa_cmp_pe, nsa_cmp_w1,
           nsa_cmp_w2, nsa_w_out, moba_w_in, moba_w_out, fox_w_in, fox_b_f, fox_w_out):
    B, T, D = x_prompt.shape
    Bs, Q, _ = x_sample.shape
    depth = norms.shape[0]
    xp = x_prompt.reshape(B * T, D)
    xs = x_sample.reshape(Bs * Q, D)
    tp = min(PROMPT_TM, B * T)
    ts = Bs * Q
    ffn_wi, ffn_wo = ffn_wi.astype(BF16), ffn_wo.astype(BF16)
    nsa_w_in, moba_w_in, fox_w_in = nsa_w_in.astype(BF16), moba_w_in.astype(BF16), fox_w_in.astype(BF16)
    nsa_w_out, moba_w_out, fox_w_out = nsa_w_out.astype(BF16), moba_w_out.astype(BF16), fox_w_out.astype(BF16)
    st_p = {0: [], 1: [], 2: []}
    st_s = {0: [], 1: [], 2: []}
    for i in range(depth):
        kind, j = i % 3, i // 3
        xp = _ffn(xp, norms[i, 0], _W(ffn_wi, (i, 0)), _W(ffn_wo, (i, 0)), tp)
        xs = _ffn(xs, norms[i, 0], _W(ffn_wi, (i, 0)), _W(ffn_wo, (i, 0)), ts)
        g = norms[i, 1]
        if kind == 0:
            w_in, w_out = _W(nsa_w_in, (j,)), _W(nsa_w_out, (j,))
            op, sp = _nsa_prompt_mixer(xp, g, w_in, nsa_cmp_pe[j], nsa_cmp_w1[j], nsa_cmp_w2[j], B, T, tp)
            os_, ss = _nsa_sample_mixer(xs, g, w_in, nsa_cmp_pe[j], nsa_cmp_w1[j], nsa_cmp_w2[j],
                                        cache_nsa_cmp, cache_nsa_slc, cache_nsa_win, page_table, j, Bs, Q)
        elif kind == 1:
            w_in, w_out = _W(moba_w_in, (j,)), _W(moba_w_out, (j,))
            op, sp = _moba_prompt_mixer(xp, g, w_in, B, T, tp)
            os_, ss = _moba_sample_mixer(xs, g, w_in, cache_moba_kv, page_table, j, Bs, Q)
        else:
            w_in, w_out = _W(fox_w_in, (j,)), _W(fox_w_out, (j,))
            op, sp = _fox_prompt_mixer(xp, g, w_in, fox_b_f[j], B, T, tp)
            os_, ss = _fox_sample_mixer(xs, g, w_in, fox_b_f[j], cache_fox_kv, cache_fox_logf, page_table, j, Bs, Q)
        st_p[kind].append(sp)
        st_s[kind].append(ss)
        xp = _mm_res(op, w_out, xp, 1.0, tm=tp, tn=512)
        xs = _mm_res(os_, w_out, xs, 1.0, tm=ts, tn=512)
        xp = _ffn(xp, norms[i, 2], _W(ffn_wi, (i, 1)), _W(ffn_wo, (i, 1)), tp)
        xs = _ffn(xs, norms[i, 2], _W(ffn_wi, (i, 1)), _W(ffn_wo, (i, 1)), ts)
    y_prompt = _rmsnorm(xp, final_norm, tm=tp).reshape(B, T, D)
    y_sample = _rmsnorm(xs, final_norm, tm=ts).reshape(Bs, Q, D)

    def stack(states, k):
        return jnp.stack([s[k] for s in states])

    return (y_prompt, y_sample,
            stack(st_p[0], 0), stack(st_p[0], 1), stack(st_p[0], 2),
            stack(st_s[0], 0), stack(st_s[0], 1), stack(st_s[0], 2),
            stack(st_p[1], 0), stack(st_s[1], 0),
            stack(st_p[2], 0), stack(st_p[2], 1), stack(st_s[2], 0), stack(st_s[2], 1))
```

```python
import functools
from typing import NamedTuple

import numpy as np
import jax
import jax.numpy as jnp
from jax import lax
from jax.experimental import pallas as pl
from jax.experimental.pallas import tpu as pltpu

F32 = jnp.float32
BF16 = jnp.bfloat16
I32 = jnp.int32

LANES = 128
VMEM_LIMIT_BYTES = 56 << 20

N_HEADS = 16
HEAD_DIM = 128
ROT_DIM = HEAD_DIM // 4
ROPE_THETA = 500000.0
NORM_EPS = 1e-6
PAGE = 128
NSA_KV_HEADS = 4
NSA_GROUP = N_HEADS // NSA_KV_HEADS
CMP_LEN = 32
CMP_STRIDE = 16
SLC_LEN = 64
SLC_TOP = 16
SLC_LOCAL = 2
WINDOW = 512
N_BRANCH = 3
MOBA_BLOCK = 256
MOBA_TOP = 3
SCALE = HEAD_DIM ** -0.5
LOG2E = 1.4426950408889634
QK_LOG2 = SCALE * LOG2E
ROW_CHUNK = 64
ONES_ROWS = 16
NEG = -1e30
NEG_INF = float("-inf")
POS_INF = float("inf")


def _cparams(*sem):
    return pltpu.CompilerParams(dimension_semantics=sem, vmem_limit_bytes=VMEM_LIMIT_BYTES)


def _nt(a, b):
    return lax.dot_general(a, b, (((1,), (1,)), ((), ())), preferred_element_type=F32)


def _dot(a, b):
    return jnp.dot(a, b, preferred_element_type=F32)


def _iota(shape, dim):
    return lax.broadcasted_iota(I32, shape, dim)


def _lane_col(x, idx):
    return jnp.sum(jnp.where(_iota(x.shape, 1) == idx, x, 0.0), axis=-1, keepdims=True)


class _W(NamedTuple):
    a: jax.Array
    lead: tuple = ()


def _w_parts(w):
    return (w.a, tuple(w.lead)) if isinstance(w, _W) else (w, ())


def _w_cols(w, start):
    a, lead = _w_parts(w)
    return a[lead + (slice(None), slice(start, None))]


def _w_spec(lead, rows, tn, col_of=lambda j: j):
    return pl.BlockSpec((None,) * len(lead) + (rows, tn), lambda i, j, *_: tuple(lead) + (0, col_of(j)))


def _rms_to_bf16(x_ref, g_ref):
    x = x_ref[...]
    var = jnp.mean(x * x, axis=-1, keepdims=True)
    return ((x * lax.rsqrt(var + NORM_EPS)) * g_ref[...]).astype(BF16)


def _log_sigmoid(x):
    return jnp.minimum(x, 0.0) - jnp.log1p(jnp.exp(-jnp.abs(x)))


def _norm_mm_kernel(*refs, epi, tn, kv_first=None):
    kv_ref = None
    if epi == "rope" and kv_first is not None:
        flags_ref, x_ref, g_ref, w_ref, c_ref, s1_ref, s2_ref, o_ref, kv_ref, xn_ref = refs
    elif epi == "rope":
        flags_ref, x_ref, g_ref, w_ref, c_ref, s1_ref, s2_ref, o_ref, xn_ref = refs
    elif epi == "logsig":
        x_ref, g_ref, w_ref, b_ref, o_ref, xn_ref = refs
    else:
        x_ref, g_ref, w_ref, o_ref, xn_ref = refs
    j = pl.program_id(1)

    @pl.when(j == 0)
    def _():
        xn_ref[...] = _rms_to_bf16(x_ref, g_ref)

    z = _dot(xn_ref[...], w_ref[...].astype(BF16))
    if epi == "rope":
        nvec = tn // HEAD_DIM
        tm = z.shape[0]

        def put(hh, val, half):
            o_ref[:, hh * HEAD_DIM:(hh + 1) * HEAD_DIM] = val
            if kv_ref is not None:
                @pl.when((j >= kv_first) & ((j - kv_first) % 2 == half))
                def _():
                    kv_ref[pl.ds(half * nvec + hh, tm, stride=2 * nvec), :] = val

        @pl.when(flags_ref[j] == 1)
        def _():
            c, s1, s2 = c_ref[...], s1_ref[...], s2_ref[...]
            for hh in range(nvec):
                zs = z[:, hh * HEAD_DIM:(hh + 1) * HEAD_DIM]
                put(hh, zs * c + pltpu.roll(zs, HEAD_DIM - ROT_DIM // 2, 1) * s1 + pltpu.roll(zs, ROT_DIM // 2, 1) * s2, 0)

        @pl.when(flags_ref[j] == 0)
        def _():
            for hh in range(nvec):
                put(hh, z[:, hh * HEAD_DIM:(hh + 1) * HEAD_DIM], 1)
    elif epi == "sigmoid":
        o_ref[...] = jax.nn.sigmoid(z)
    elif epi == "logsig":
        o_ref[...] = _log_sigmoid(z + b_ref[...])
    else:
        o_ref[...] = z


def _norm_mm(x, g, w, n_out, *, tm, tn=512, epi="none", rope=None, bias=None, kv_rows=None):
    M, D = x.shape
    w, lead = _w_parts(w)
    tn = min(tn, n_out)
    assert M % tm == 0 and n_out % tn == 0
    grid = (M // tm, n_out // tn)
    g2 = g.reshape(1, D)
    kern = functools.partial(_norm_mm_kernel, epi=epi, tn=tn)
    scratch = [pltpu.VMEM((tm, D), BF16)]
    out_shape = jax.ShapeDtypeStruct((M, n_out), F32)
    if epi == "rope":
        flags, c, s1, s2 = rope
        nt = c.shape[0] // tm
        tab = pl.BlockSpec((tm, HEAD_DIM), lambda i, j, f: (i % nt, 0))
        out_specs = pl.BlockSpec((tm, tn), lambda i, j, f: (i, j))
        if kv_rows is not None:
            first, nbr = kv_rows
            vecs = 2 * tn // HEAD_DIM
            kern = functools.partial(_norm_mm_kernel, epi=epi, tn=tn, kv_first=first)
            out_shape = [out_shape, jax.ShapeDtypeStruct((nbr, M * vecs, HEAD_DIM), F32)]
            out_specs = [out_specs, pl.BlockSpec((None, tm * vecs, HEAD_DIM),
                                                 lambda i, j, f: (jnp.clip((j - first) // 2, 0, nbr - 1), i, 0))]
        gs = pltpu.PrefetchScalarGridSpec(
            num_scalar_prefetch=1, grid=grid,
            in_specs=[pl.BlockSpec((tm, D), lambda i, j, f: (i, 0)),
                      pl.BlockSpec((1, D), lambda i, j, f: (0, 0)),
                      _w_spec(lead, D, tn),
                      tab, tab, tab],
            out_specs=out_specs,
            scratch_shapes=scratch)
        return pl.pallas_call(kern, grid_spec=gs, out_shape=out_shape, name="norm_mm_rope",
                              compiler_params=_cparams("parallel", "arbitrary"))(flags, x, g2, w, c, s1, s2)
    in_specs = [pl.BlockSpec((tm, D), lambda i, j: (i, 0)),
                pl.BlockSpec((1, D), lambda i, j: (0, 0)),
                _w_spec(lead, D, tn)]
    args = [x, g2, w]
    if epi == "logsig":
        in_specs.append(pl.BlockSpec((1, tn), lambda i, j: (0, j)))
        args.append(bias.reshape(1, n_out))
    return pl.pallas_call(kern, grid=grid, in_specs=in_specs,
                          out_specs=pl.BlockSpec((tm, tn), lambda i, j: (i, j)),
                          out_shape=out_shape, scratch_shapes=scratch, name="norm_mm_" + epi,
                          compiler_params=_cparams("parallel", "arbitrary"))(*args)


def _swiglu_up_kernel(x_ref, g_ref, wa_ref, *rest, nsub):
    wb_refs = rest[:nsub]
    o_ref, xn_ref, wb_scr = rest[nsub:]

    @pl.when(pl.program_id(1) == 0)
    def _():
        xn_ref[...] = _rms_to_bf16(x_ref, g_ref)

    for r in range(nsub):
        wb_scr[:, r * LANES:(r + 1) * LANES] = wb_refs[r][...]
    xn = xn_ref[...]
    a = _dot(xn, wa_ref[...])
    b = _dot(xn, wb_scr[...])
    o_ref[...] = (a * jax.nn.sigmoid(a) * b).astype(o_ref.dtype)


def _swiglu_up(x, g, wi, *, tm, tn=512):
    M, D = x.shape
    wi, lead = _w_parts(wi)
    F = wi.shape[-1] // 2
    assert F % LANES == 0 and M % tm == 0
    nsub = tn // LANES
    off = F // LANES
    last = wi.shape[-1] // LANES - 1
    in_specs = [pl.BlockSpec((tm, D), lambda i, j: (i, 0)), pl.BlockSpec((1, D), lambda i, j: (0, 0)),
                _w_spec(lead, D, tn)]
    in_specs += [_w_spec(lead, D, LANES, lambda j, r=r: jnp.minimum(off + nsub * j + r, last)) for r in range(nsub)]
    return pl.pallas_call(
        functools.partial(_swiglu_up_kernel, nsub=nsub),
        grid=(M // tm, pl.cdiv(F, tn)), in_specs=in_specs,
        out_specs=pl.BlockSpec((tm, tn), lambda i, j: (i, j)),
        out_shape=jax.ShapeDtypeStruct((M, F), BF16),
        scratch_shapes=[pltpu.VMEM((tm, D), BF16), pltpu.VMEM((D, tn), BF16)],
        name="swiglu_up",
        compiler_params=_cparams("parallel", "arbitrary"))(x, g.reshape(1, D), wi, *([wi] * nsub))


def _mm_res_kernel(u_ref, w_ref, r_ref, o_ref, *, scale):
    y = _dot(u_ref[...].astype(BF16), w_ref[...].astype(BF16))
    o_ref[...] = r_ref[...] + (y if scale == 1.0 else scale * y)


def _mm_res(u, w, res, scale, *, tm, tn):
    M, K = u.shape
    w, lead = _w_parts(w)
    N = w.shape[-1]
    assert M % tm == 0 and N % tn == 0
    return pl.pallas_call(
        functools.partial(_mm_res_kernel, scale=scale),
        grid=(M // tm, N // tn),
        in_specs=[pl.BlockSpec((tm, K), lambda i, j: (i, 0)),
                  _w_spec(lead, K, tn),
                  pl.BlockSpec((tm, tn), lambda i, j: (i, j))],
        out_specs=pl.BlockSpec((tm, tn), lambda i, j: (i, j)),
        out_shape=jax.ShapeDtypeStruct((M, N), F32), name="mm_res",
        compiler_params=_cparams("parallel", "arbitrary"))(u, w, res)


def _rmsnorm_kernel(x_ref, g_ref, o_ref):
    x = x_ref[...]
    var = jnp.mean(x * x, axis=-1, keepdims=True)
    o_ref[...] = (x * lax.rsqrt(var + NORM_EPS)) * g_ref[...]


def _rmsnorm(x, g, *, tm):
    M, D = x.shape
    return pl.pallas_call(
        _rmsnorm_kernel, grid=(M // tm,),
        in_specs=[pl.BlockSpec((tm, D), lambda i: (i, 0)), pl.BlockSpec((1, D), lambda i: (0, 0))],
        out_specs=pl.BlockSpec((tm, D), lambda i: (i, 0)),
        out_shape=jax.ShapeDtypeStruct((M, D), F32), name="rmsnorm",
        compiler_params=_cparams("parallel"))(x, g.reshape(1, D))


def _rope_tables(pos):
    half = ROT_DIM // 2
    inv = ROPE_THETA ** (-jnp.arange(half, dtype=F32) / half)
    ang = pos.astype(F32)[:, None] * inv[None, :]
    cos, sin = jnp.cos(ang), jnp.sin(ang)
    n = pos.shape[0]
    zeros = jnp.zeros((n, HEAD_DIM - ROT_DIM), F32)
    zh = jnp.zeros((n, half), F32)
    c = jnp.concatenate([cos, cos, jnp.ones((n, HEAD_DIM - ROT_DIM), F32)], axis=1)
    s1 = jnp.concatenate([-sin, zh, zeros], axis=1)
    s2 = jnp.concatenate([zh, sin, zeros], axis=1)
    return c, s1, s2


def _online_update(t, m, l, acc, v, roff=None, pv_roll=0):
    ms, ls, als, ps = [], [], [], []
    for r0 in range(0, t.shape[0], ROW_CHUNK):
        r = slice(r0, r0 + ROW_CHUNK)
        tmax = jnp.max(t[r], axis=-1, keepdims=True)
        if roff is not None:
            tmax = tmax + roff[r]
        m_new = jnp.maximum(m[r], tmax)
        alpha = jnp.exp2(m[r] - m_new)
        p = jnp.exp2(t[r] + ((roff[r] - m_new) if roff is not None else -m_new))
        ms.append(m_new)
        als.append(alpha)
        ls.append(alpha * l[r] + jnp.sum(p, axis=-1, keepdims=True))
        ps.append((pltpu.roll(p, pv_roll, 1) if pv_roll else p).astype(BF16))
    cat = lambda xs: xs[0] if len(xs) == 1 else jnp.concatenate(xs, axis=0)
    return cat(ms), cat(ls), cat(als) * acc + _dot(cat(ps), v)


def _online_update_t(t, m, l, acc, vt, coff=None):
    tmax = jnp.max(t, axis=0, keepdims=True)
    if coff is not None:
        tmax = tmax + coff
    m_new = jnp.maximum(m, tmax)
    alpha = jnp.exp2(m - m_new)
    p = jnp.exp2(t + ((coff - m_new) if coff is not None else -m_new)).astype(BF16)
    ones = jnp.ones((ONES_ROWS, vt.shape[1]), BF16)
    pv = _dot(jnp.concatenate([vt, ones], axis=0), p)
    dh = vt.shape[0]
    return m_new, alpha * l + pv[dh:dh + 1], alpha * acc + pv[:dh]


def _flash_init_t(n_q, width=HEAD_DIM):
    return (jnp.full((1, n_q), NEG, F32), jnp.zeros((1, n_q), F32), jnp.zeros((width, n_q), F32))


def _transpose_into(dst_ref, src_ref, n_rows, dtype, chunk=512):
    for r0 in range(0, n_rows, chunk):
        r1 = min(r0 + chunk, n_rows)
        dst_ref[:, r0:r1] = src_ref[r0:r1, :].T.astype(dtype)


def _mask_bias(ok):
    return jnp.where(ok, 0.0, NEG)


def _flash_init(rows, width=HEAD_DIM):
    return (jnp.full((rows, 1), NEG, F32), jnp.zeros((rows, 1), F32), jnp.zeros((rows, width), F32))


def _flash_finish(l, acc):
    return acc / jnp.where(l > 0.0, l, 1.0)


def _masked_softmax(s, ok, axis=-1):
    sm = jnp.where(ok, s, NEG)
    mx = jnp.max(sm, axis=axis, keepdims=True)
    e = jnp.where(ok, jnp.exp(sm - mx), 0.0)
    d = jnp.sum(e, axis=axis, keepdims=True)
    return e / jnp.where(d > 0.0, d, 1.0)


def _top_k_mask(score, k, axis=-1, finite_only=False):
    idx = _iota(score.shape, axis % score.ndim)
    n = score.shape[axis]
    sel = jnp.zeros(score.shape, F32)
    for _ in range(k):
        mx = jnp.max(score, axis=axis, keepdims=True)
        first = jnp.min(jnp.where(score == mx, idx, n), axis=axis, keepdims=True)
        pick = idx == first
        sel = jnp.where((pick & (mx > NEG_INF)) if finite_only else pick, 1.0, sel)
        score = jnp.where(pick, NEG_INF, score)
    return sel


def _tile_cumsum(x):
    lane = _iota(x.shape, 1)
    for s in (1, 2, 4, 8, 16, 32, 64):
        x = x + jnp.where(lane >= s, pltpu.roll(x, s, 1), 0.0)
    return x


def _cumsum_rows_kernel(x_ref, o_ref, *, n_tiles):
    carry = jnp.zeros((x_ref.shape[0], 1), F32)
    for j in range(n_tiles):
        cs = _tile_cumsum(x_ref[:, j * LANES:(j + 1) * LANES]) + carry
        o_ref[:, j * LANES:(j + 1) * LANES] = cs
        carry = cs[:, LANES - 1:LANES]


def _cumsum_rows(xt):
    B, H, T = xt.shape
    return pl.pallas_call(
        functools.partial(_cumsum_rows_kernel, n_tiles=T // LANES), grid=(B,),
        in_specs=[pl.BlockSpec((None, H, T), lambda b: (b, 0, 0))],
        out_specs=pl.BlockSpec((None, H, T), lambda b: (b, 0, 0)),
        out_shape=jax.ShapeDtypeStruct((B, H, T), F32), name="cumsum_rows",
        compiler_params=_cparams("parallel"))(xt)


def _fox_prompt_kernel(q_ref, k_ref, v_ref, c_ref, ct_ref, o_ref, vt_ref, ck_ref, *, tq, T):
    h = pl.program_id(1)
    qi = pl.program_id(2)

    @pl.when(qi == 0)
    def _():
        _transpose_into(vt_ref, v_ref, T, BF16)
        for r0 in range(0, T, 512):
            col = _lane_col(c_ref[r0:r0 + 512, :], h) * LOG2E
            ck_ref[r0:r0 + 512, :] = jnp.broadcast_to(col, (512, LANES))

    q = q_ref[...].astype(BF16)
    q0 = pl.multiple_of(qi * tq, tq)
    cq2 = ct_ref[pl.ds(h, 1), pl.ds(q0, tq)] * LOG2E

    def step(j, carry, diagonal):
        k0 = pl.multiple_of(j * tq, tq)
        k = k_ref[pl.ds(k0, tq), :].astype(BF16)
        t = _nt(k, q) * QK_LOG2 - jnp.concatenate([ck_ref[pl.ds(k0, tq), :]] * (tq // LANES), axis=1)
        if diagonal:
            t = t + _mask_bias(_iota((tq, tq), 0) <= _iota((tq, tq), 1))
        return _online_update_t(t, *carry, vt_ref[:, pl.ds(k0, tq)], coff=cq2)

    carry = lax.fori_loop(0, qi, lambda j, c: step(j, c, False), _flash_init_t(tq))
    _, l, acc = step(qi, carry, True)
    o_ref[...] = _flash_finish(l, acc).T.astype(o_ref.dtype)


def _fox_prompt_attn(z, c, ct, B, T, *, tq=1024):
    tq = min(tq, T)
    assert T % tq == 0 and T % 512 == 0
    nq = T // tq
    H = N_HEADS
    return pl.pallas_call(
        functools.partial(_fox_prompt_kernel, tq=tq, T=T), grid=(B, H, nq),
        in_specs=[pl.BlockSpec((tq, HEAD_DIM), lambda b, h, i: (b * nq + i, h)),
                  pl.BlockSpec((T, HEAD_DIM), lambda b, h, i: (b, H + h)),
                  pl.BlockSpec((T, HEAD_DIM), lambda b, h, i: (b, 2 * H + h)),
                  pl.BlockSpec((T, LANES), lambda b, h, i: (b, 0)),
                  pl.BlockSpec((None, H, T), lambda b, h, i: (b, 0, 0))],
        out_specs=pl.BlockSpec((tq, HEAD_DIM), lambda b, h, i: (b * nq + i, h)),
        out_shape=jax.ShapeDtypeStruct((B * T, H * HEAD_DIM), BF16),
        scratch_shapes=[pltpu.VMEM((HEAD_DIM, T), BF16), pltpu.VMEM((T, LANES), F32)], name="fox_prompt",
        compiler_params=_cparams("parallel", "parallel", "arbitrary"))(z, z, z, c, ct)


def _moba_prompt_kernel(q_ref, k_ref, v_ref, o_ref, km_ref, vt_ref, *, tq, T):
    qi = pl.program_id(2)
    q0 = qi * tq
    nb = T // MOBA_BLOCK
    nsub = tq // MOBA_BLOCK
    nbr = km_ref.shape[0]

    @pl.when(qi == 0)
    def _():
        km_ref[...] = jnp.zeros(km_ref.shape, F32)
        for n in range(nb):
            km_ref[n:n + 1, :] = jnp.mean(k_ref[n * MOBA_BLOCK:(n + 1) * MOBA_BLOCK, :], axis=0, keepdims=True)
        _transpose_into(vt_ref, v_ref, T, BF16)

    q = q_ref[...].astype(BF16)
    blk = _iota((nbr, tq), 0)
    own = (q0 + _iota((nbr, tq), 1)) // MOBA_BLOCK
    gate_t = jnp.where(blk < own, _nt(km_ref[...].astype(BF16), q), NEG_INF)
    sel_bias_t = _mask_bias(_top_k_mask(gate_t, MOBA_TOP, axis=0, finite_only=True) > 0.5)

    def tile(j, diagonal):
        k0 = pl.multiple_of(j * tq, tq)
        t = _nt(k_ref[pl.ds(k0, tq), :].astype(BF16), q) * QK_LOG2
        parts = []
        for r in range(nsub):
            b = jnp.sum(jnp.where(blk == j * nsub + r, sel_bias_t, 0.0), axis=0, keepdims=True)
            b = jnp.broadcast_to(b, (MOBA_BLOCK, tq))
            if diagonal:
                kk = r * MOBA_BLOCK + _iota((MOBA_BLOCK, tq), 0)
                qq = _iota((MOBA_BLOCK, tq), 1)
                b = jnp.where(qq // MOBA_BLOCK == r, _mask_bias(kk <= qq), b)
            parts.append(b)
        return t + jnp.concatenate(parts, axis=0), vt_ref[:, pl.ds(k0, tq)]

    t, vt = tile(qi, True)
    carry = _online_update_t(t, *_flash_init_t(tq), vt)

    def body(j, carry):
        t, vt = tile(j, False)
        return _online_update_t(t, *carry, vt)

    _, l, acc = lax.fori_loop(0, qi, body, carry)
    o_ref[...] = _flash_finish(l, acc).T.astype(o_ref.dtype)


def _moba_prompt_attn(z, B, T, *, tq=1024):
    tq = min(tq, T)
    assert T % tq == 0 and tq % MOBA_BLOCK == 0
    nq = T // tq
    H = N_HEADS
    return pl.pallas_call(
        functools.partial(_moba_prompt_kernel, tq=tq, T=T), grid=(B, H, nq),
        in_specs=[pl.BlockSpec((tq, HEAD_DIM), lambda b, h, i: (b * nq + i, h)),
                  pl.BlockSpec((T, HEAD_DIM), lambda b, h, i: (b, H + h)),
                  pl.BlockSpec((T, HEAD_DIM), lambda b, h, i: (b, 2 * H + h))],
        out_specs=pl.BlockSpec((tq, HEAD_DIM), lambda b, h, i: (b * nq + i, h)),
        out_shape=jax.ShapeDtypeStruct((B * T, H * HEAD_DIM), BF16),
        scratch_shapes=[pltpu.VMEM((-(-(T // MOBA_BLOCK) // 8) * 8, HEAD_DIM), F32),
                        pltpu.VMEM((HEAD_DIM, T), BF16)], name="moba_prompt",
        compiler_params=_cparams("parallel", "parallel", "arbitrary"))(z, z, z)


ROWS_PER_KV = 2 * NSA_KV_HEADS
CHUNKS_PER_PAGE = PAGE // CMP_STRIDE


def _compress_kernel(pt_ref, *refs, ppg, npp):
    del pt_ref
    x_refs = refs[:npp]
    w1_ref, w2_ref, pe_ref, o_ref, xcat_ref, carry_ref = refs[npp:]
    s = pl.program_id(1)
    slot = s % (ppg // npp)
    V = ROWS_PER_KV
    for r in range(npp):
        base = pl.multiple_of((slot * npp + r) * (CHUNKS_PER_PAGE * V), CHUNKS_PER_PAGE * V)
        for n in range(CHUNKS_PER_PAGE):
            for i in range(CMP_STRIDE):
                row = (n * CMP_STRIDE + i) * V
                xcat_ref[pl.ds(base + n * V, V), i * HEAD_DIM:(i + 1) * HEAD_DIM] = x_refs[r][row:row + V, :]

    @pl.when(s == 0)
    def _():
        carry_ref[...] = jnp.zeros(carry_ref.shape, F32)

    @pl.when(slot == ppg // npp - 1)
    def _():
        rows = ppg * CHUNKS_PER_PAGE * V
        H = HEAD_DIM
        is_v = (_iota((rows, H), 0) % V) >= NSA_KV_HEADS
        pick = lambda y: jnp.where(is_v, y[:, y.shape[1] // 2:y.shape[1] // 2 + H], y[:, :H])
        w1 = w1_ref[...]
        pb = _dot(pe_ref[...].astype(BF16), w1)
        bias = jnp.where(is_v, pb[2:3, 2 * H:3 * H] + pb[3:4, 3 * H:], pb[0:1, :H] + pb[1:2, H:2 * H])
        part = _dot(xcat_ref[...].astype(BF16), w1)
        p0 = pick(part)
        p1 = jnp.where(is_v, part[:, 3 * H:], part[:, H:2 * H])
        p0_prev = jnp.concatenate([carry_ref[...], p0[:rows - V]], axis=0)
        pre = bias + p0_prev + p1
        o_ref[...] = pick(_dot((pre * jax.nn.sigmoid(pre)).astype(BF16), w2_ref[...]))
        carry_ref[...] = p0[rows - V:]


def _compress(rows, page_table, w1, w2, pe):
    B, P = page_table.shape
    ppg = min(32, P)
    npp = min(4, ppg)
    assert P % ppg == 0 and ppg % npp == 0
    half = CMP_LEN * HEAD_DIM // 2
    w1cat = jnp.concatenate([w1[0, :half], w1[0, half:], w1[1, :half], w1[1, half:]], axis=1).astype(BF16)
    w2cat = jnp.concatenate([w2[0], w2[1]], axis=1).astype(BF16)
    pe4 = jnp.concatenate([pe.reshape(4, half), jnp.zeros((4, half), F32)], axis=0)
    grp = ppg * CHUNKS_PER_PAGE * ROWS_PER_KV
    page = [pl.BlockSpec((None, PAGE * ROWS_PER_KV, HEAD_DIM), lambda b, s, pt, r=r: (pt[b, s * npp + r], 0, 0))
            for r in range(npp)]
    const = lambda shape: pl.BlockSpec(shape, lambda b, s, pt: (0,) * len(shape))
    gs = pltpu.PrefetchScalarGridSpec(
        num_scalar_prefetch=1, grid=(B, P // npp),
        in_specs=page + [const(w1cat.shape), const(w2cat.shape), const(pe4.shape)],
        out_specs=pl.BlockSpec((None, grp, HEAD_DIM), lambda b, s, pt: (b, s // (ppg // npp), 0)),
        scratch_shapes=[pltpu.VMEM((grp, CMP_STRIDE * HEAD_DIM), F32), pltpu.VMEM((ROWS_PER_KV, HEAD_DIM), F32)])
    out = pl.pallas_call(
        functools.partial(_compress_kernel, ppg=ppg, npp=npp), grid_spec=gs,
        out_shape=jax.ShapeDtypeStruct((B, P * CHUNKS_PER_PAGE * ROWS_PER_KV, HEAD_DIM), F32), name="nsa_compress",
        compiler_params=_cparams("parallel", "arbitrary"))(page_table, *([rows] * npp), w1cat, w2cat, pe4)
    return out.reshape(B, P * CHUNKS_PER_PAGE, ROWS_PER_KV, HEAD_DIM)


def _overlap_shifted(nm, nb, width):
    nc = nm - 1
    m = np.zeros((nm, width), np.float32)
    j = np.arange(nb)
    for a in range(SLC_LEN // CMP_STRIDE):
        for b in range(CMP_LEN // CMP_STRIDE):
            i = (SLC_LEN // CMP_STRIDE) * j + a - b
            ok = (i >= 0) & (i < nc)
            np.add.at(m, (i[ok] + 1, j[ok]), 1.0)
    return jnp.asarray(m, BF16)


def _slc_scores(imp, qpos_blk, lane):
    lag = qpos_blk - lane
    valid = lag >= 0
    forced = (lane == 0) | (valid & (lag < SLC_LOCAL))
    return jnp.where(valid, jnp.where(forced, POS_INF, imp), NEG_INF)


def _nsa_prompt_kernel(q_ref, kc_ref, vc_ref, ks_ref, vs_ref, kw_ref, vw_ref, gt_ref, ovl_ref, et_ref, o_ref,
                       vst_ref, vwt_ref, *, tq, tk, nm, T):
    G = NSA_GROUP
    hkv = pl.program_id(1)
    qi = pl.program_id(2)
    q0 = pl.multiple_of(qi * tq, tq)

    @pl.when(qi == 0)
    def _():
        _transpose_into(vst_ref, vs_ref, T, BF16)
        _transpose_into(vwt_ref, vw_ref, T, BF16)

    qs = jnp.concatenate([q_ref[:, g * HEAD_DIM:(g + 1) * HEAD_DIM] for g in range(G)], axis=0).astype(BF16)

    def rep(x):
        return jnp.concatenate([x] * G, axis=1)

    s = _nt(kc_ref[...].astype(BF16), qs) * SCALE
    m_i = _iota((nm, tq), 0)
    c_ok = jnp.where((m_i >= 1) & ((m_i - 1) * CMP_STRIDE + (CMP_LEN - 1) <= q0 + _iota((nm, tq), 1)), 1.0, 0.0)
    p_cmp = _masked_softmax(s, rep(c_ok) > 0.5, axis=0).astype(BF16)
    o_cmp = _dot(vc_ref[...].T.astype(BF16), p_cmp)
    pov = _dot(ovl_ref[...], p_cmp)
    imp = pov[:, 0:tq]
    for g in range(1, G):
        imp = imp + pov[:, g * tq:(g + 1) * tq]
    blk = _iota(imp.shape, 0)
    qblk = (q0 + _iota(imp.shape, 1)) // SLC_LEN
    sel_b = _top_k_mask(_slc_scores(imp, qblk, blk), SLC_TOP, axis=0).astype(BF16)

    krow = _iota((tk, tq), 0)
    qpos = q0 + _iota((tk, tq), 1)

    def slc_body(j, carry):
        k0 = pl.multiple_of(j * tk, tk)
        k = ks_ref[pl.ds(k0, tk), :].astype(BF16)
        sel_e = _dot(et_ref[pl.ds(k0, tk), :], sel_b)
        bias = jnp.where((k0 + krow) <= qpos, (sel_e - 1.0) * -NEG, NEG)
        return _online_update_t(_nt(k, qs) * QK_LOG2 + rep(bias), *carry, vst_ref[:, pl.ds(k0, tk)])

    _, l, acc = lax.fori_loop(0, (q0 + tq + tk - 1) // tk, slc_body, _flash_init_t(G * tq))
    o_slc = _flash_finish(l, acc)

    nw = WINDOW + tq
    w0 = pl.multiple_of(jnp.maximum(q0 - WINDOW, 0), tq)
    d = (q0 + _iota((nw, tq), 1)) - (w0 + _iota((nw, tq), 0))
    t = _nt(kw_ref[pl.ds(w0, nw), :].astype(BF16), qs) * QK_LOG2 + rep(_mask_bias((d >= 0) & (d < WINDOW)))
    _, l, acc = _online_update_t(t, *_flash_init_t(G * tq), vwt_ref[:, pl.ds(w0, nw)])
    o_win = _flash_finish(l, acc)

    for g in range(G):
        col = (hkv * G + g) * N_BRANCH
        c = slice(g * tq, (g + 1) * tq)
        o = (o_cmp[:, c] * gt_ref[pl.ds(col, 1), :] + o_slc[:, c] * gt_ref[pl.ds(col + 1, 1), :]
             + o_win[:, c] * gt_ref[pl.ds(col + 2, 1), :])
        o_ref[:, g * HEAD_DIM:(g + 1) * HEAD_DIM] = o.T.astype(o_ref.dtype)


def _nsa_prompt_attn(z, comp, gates_t, B, T, *, tq=512, tk=512):
    tk = min(tk, T)
    assert T % tk == 0 and WINDOW % tq == 0 and T >= WINDOW + tq and T % 512 == 0
    nq = T // tq
    nm = comp.shape[2]
    G = NSA_GROUP
    qw = G * HEAD_DIM
    kvb = N_HEADS
    nb = T // SLC_LEN
    nbr = -(-nb // 8) * 8
    ovl = _overlap_shifted(nm, nb, nbr).T
    et = jnp.asarray((np.arange(T)[:, None] // SLC_LEN) == np.arange(nbr)[None, :], BF16)

    def kv_spec(off):
        return pl.BlockSpec((T, HEAD_DIM), lambda b, h, i: (b, kvb + off + h))

    return pl.pallas_call(
        functools.partial(_nsa_prompt_kernel, tq=tq, tk=tk, nm=nm, T=T), grid=(B, NSA_KV_HEADS, nq),
        in_specs=[pl.BlockSpec((tq, qw), lambda b, h, i: (b * nq + i, h)),
                  pl.BlockSpec((None, None, nm, HEAD_DIM), lambda b, h, i: (b, h, 0, 0)),
                  pl.BlockSpec((None, None, nm, HEAD_DIM), lambda b, h, i: (b, NSA_KV_HEADS + h, 0, 0)),
                  kv_spec(8), kv_spec(12), kv_spec(16), kv_spec(20),
                  pl.BlockSpec((LANES, tq), lambda b, h, i: (0, b * nq + i)),
                  pl.BlockSpec((nbr, nm), lambda b, h, i: (0, 0)),
                  pl.BlockSpec((T, nbr), lambda b, h, i: (0, 0))],
        out_specs=pl.BlockSpec((tq, qw), lambda b, h, i: (b * nq + i, h)),
        out_shape=jax.ShapeDtypeStruct((B * T, N_HEADS * HEAD_DIM), BF16),
        scratch_shapes=[pltpu.VMEM((HEAD_DIM, T), BF16), pltpu.VMEM((HEAD_DIM, T), BF16)], name="nsa_prompt",
        compiler_params=_cparams("parallel", "parallel", "arbitrary"))(z, comp, comp, z, z, z, z, gates_t, ovl, et)


NSA_Q = N_HEADS * HEAD_DIM
NSA_KV = NSA_KV_HEADS * HEAD_DIM
NSA_MAIN = NSA_Q + 2 * N_BRANCH * NSA_KV
HD = N_HEADS * HEAD_DIM


def _pad_cols(w, width=LANES):
    return jnp.pad(w, ((0, 0), (0, width - w.shape[1])))


def _nsa_rope_flags(tn=512):
    per = NSA_KV // tn
    flags = [1] * (NSA_Q // tn)
    for _ in range(N_BRANCH):
        flags += [1] * per + [0] * per
    return jnp.asarray(flags, I32)


def _nsa_project(x2d, g, w_in, pos_tables, tm, kv_rows=False):
    assert NSA_KV == 512
    z = _norm_mm(x2d, g, w_in, NSA_MAIN, tm=tm, epi="rope", rope=(_nsa_rope_flags(),) + pos_tables,
                 kv_rows=(NSA_Q // NSA_KV, N_BRANCH) if kv_rows else None)
    gates = _norm_mm(x2d, g, _pad_cols(_w_cols(w_in, NSA_MAIN)), LANES, tm=tm, epi="sigmoid")
    return z, gates


def _nsa_prompt_mixer(x2d, g, w_in, pe, w1, w2, B, T, tm):
    (z, kv), gates = _nsa_project(x2d, g, w_in, _rope_tables(jnp.arange(T, dtype=I32)), tm, kv_rows=True)
    rows = kv[0].reshape(B * T // PAGE, PAGE * ROWS_PER_KV, HEAD_DIM)
    pt = jnp.arange(B * T // PAGE, dtype=I32).reshape(B, T // PAGE)
    comp = _compress(rows, pt, w1, w2, pe).transpose(0, 2, 1, 3)
    o = _nsa_prompt_attn(z, comp, gates.T, B, T)
    st = kv.reshape(N_BRANCH, B, T, 2, NSA_KV_HEADS, HEAD_DIM)
    return o, (st[0], st[1], st[2][:, T - min(WINDOW, T):])


def _moba_rope_flags(tn=512):
    return jnp.asarray([1] * (2 * HD // tn) + [0] * (HD // tn), I32)


def _moba_prompt_mixer(x2d, g, w_in, B, T, tm):
    z = _norm_mm(x2d, g, w_in, 3 * HD, tm=tm, epi="rope",
                 rope=(_moba_rope_flags(),) + _rope_tables(jnp.arange(T, dtype=I32)))
    o = _moba_prompt_attn(z, B, T)
    return o, (z[:, HD:].reshape(B, T, 2, N_HEADS, HEAD_DIM),)


def _fox_project(x2d, g, w_in, b_f, tm):
    z = _norm_mm(x2d, g, w_in, 3 * HD, tm=tm)
    logf = _norm_mm(x2d, g, _pad_cols(_w_cols(w_in, 3 * HD)), LANES, tm=tm, epi="logsig",
                    bias=jnp.pad(b_f, (0, LANES - N_HEADS)))
    return z, logf


def _fox_prompt_mixer(x2d, g, w_in, b_f, B, T, tm):
    z, logf = _fox_project(x2d, g, w_in, b_f, tm)
    log_f = logf[:, :N_HEADS].reshape(B, T, N_HEADS)
    ct = _cumsum_rows(log_f.transpose(0, 2, 1))
    c = _pad_cols(ct.transpose(0, 2, 1).reshape(B * T, N_HEADS))
    o = _fox_prompt_attn(z, c, ct, B, T)
    return o, (z[:, HD:].reshape(B, T, 2, N_HEADS, HEAD_DIM), log_f)


def _ffn(x2d, g, wi, wo, tm):
    u = _swiglu_up(x2d, g, wi, tm=tm)
    return _mm_res(u, wo, x2d, 0.5, tm=tm, tn=FFN_DOWN_TN)


def _diag_blocks(acc, nblk):
    rblk = _iota((acc.shape[0], HEAD_DIM), 0) % nblk
    out = jnp.zeros((acc.shape[0], HEAD_DIM), F32)
    for j in range(nblk):
        out = out + jnp.where(rblk == j, acc[:, j * HEAD_DIM:(j + 1) * HEAD_DIM], 0.0)
    return out


def _row_query(rows, mode):
    r = _iota((rows, LANES), 0)
    return (r % N_HEADS) // NSA_KV_HEADS if mode == "nsa" else r // N_HEADS


def _paged_attn_kernel(pt_ref, q_ref, hm_ref, hmn_ref, *rest, mode, nh, npp, n_steps):
    del pt_ref
    joint = mode == "nsa"
    if joint:
        xk_refs = xv_refs = rest[:npp]
        kn_ref = vn_ref = rest[npp]
        rest = rest[npp + 1:]
    else:
        xk_refs, xv_refs = rest[:npp], rest[npp:2 * npp]
        kn_ref, vn_ref = rest[2 * npp:2 * npp + 2]
        rest = rest[2 * npp + 2:]
    if mode == "fox":
        cq_ref, ck_ref, ckn_ref, o_ref, m_ref, l_ref, acc_ref = rest
    else:
        sel_ref, o_ref, m_ref, l_ref, acc_ref = rest
    p = pl.program_id(1)
    R = q_ref.shape[0]
    q = q_ref[...]
    roff = cq_ref[:, 0:1] * LOG2E if mode == "fox" else None
    pv_roll = nh if joint else 0

    def rows_of(refs):
        xs = [r[...].reshape(-1, HEAD_DIM).astype(BF16) for r in refs]
        return xs[0] if len(xs) == 1 else jnp.concatenate(xs, axis=0)

    def tiles(k_refs, v_refs):
        k = rows_of(k_refs)
        return k, (k if joint else rows_of(v_refs))

    def sel_bias(blk):
        if isinstance(blk, int):
            tile = sel_ref[:, (blk // LANES) * LANES:(blk // LANES + 1) * LANES]
        else:
            tile = sel_ref[:, pl.ds(pl.multiple_of((blk // LANES) * LANES, LANES), LANES)]
        return _mask_bias(_lane_col(tile, blk % LANES) > 0.5), _mask_bias(_lane_col(tile, blk % LANES + 1) > 0.5)

    def update(t, v, row_off):
        m, l, acc = _online_update(t, m_ref[...], l_ref[...], acc_ref[...], v, roff=row_off, pv_roll=pv_roll)
        m_ref[...] = m
        l_ref[...] = l
        acc_ref[...] = acc

    @pl.when(p == 0)
    def _():
        m_ref[...] = jnp.full(m_ref.shape, NEG, F32)
        l_ref[...] = jnp.zeros(l_ref.shape, F32)
        acc_ref[...] = jnp.zeros(acc_ref.shape, F32)
        kn, vn = tiles((kn_ref,), (vn_ref,))
        tn = _nt(q, kn) * QK_LOG2 + hmn_ref[...]
        if mode == "fox":
            tn = tn - ckn_ref[...] * LOG2E
        elif mode == "nsa":
            tn = tn + sel_bias(n_steps * npp * (PAGE // SLC_LEN))[0]
        update(tn, vn, roff)

    k, v = tiles(xk_refs, xv_refs)
    t = _nt(q, k) * QK_LOG2 + hm_ref[...]
    if mode == "fox":
        update(t - ck_ref[...] * LOG2E, v, roff)
    else:
        blk_len = MOBA_BLOCK if mode == "moba" else SLC_LEN
        width = blk_len * (2 * nh if joint else nh)
        parts = []
        for j in range(0, npp * PAGE // blk_len, 2):
            b0, b1 = sel_bias(p * (npp * PAGE // blk_len) + j)
            parts += [jnp.broadcast_to(b0, (R, width)), jnp.broadcast_to(b1, (R, width))]
        update(t + jnp.concatenate(parts, axis=1), v, None)

    @pl.when(p == n_steps - 1)
    def _():
        o_ref[...] = _flash_finish(l_ref[...], acc_ref[...]).astype(o_ref.dtype)


def _head_mask(rows, cols, vecs, nh, causal_q=None):
    r = np.arange(rows)[:, None]
    c = np.arange(cols)[None, :]
    ok = (c % vecs) == (r % nh)
    if causal_q is not None:
        ok &= (c // vecs) <= causal_q(r)
    return jnp.asarray(np.where(ok, 0.0, NEG), F32)


def _paged_attn(q_rows, pools, page_table, new_rows, extras, *, mode, out_dtype):
    B, R, _ = q_rows.shape
    P = page_table.shape[1]
    joint = mode == "nsa"
    nh = NSA_KV_HEADS if joint else N_HEADS
    vecs = 2 * nh if joint else nh
    n_new = new_rows[0].shape[1]
    q_of_row = (lambda r: (r % N_HEADS) // NSA_KV_HEADS) if joint else (lambda r: r // N_HEADS)
    npp = min(4, P)
    assert P % npp == 0 and (npp * PAGE) % (2 * MOBA_BLOCK) == 0
    hm = _head_mask(R, npp * PAGE * vecs, vecs, nh)
    hmn = _head_mask(R, n_new, vecs, nh, causal_q=q_of_row)
    const = lambda shape: pl.BlockSpec(shape, lambda b, p, pt: (0,) * len(shape))
    per_b = lambda shape: pl.BlockSpec((None,) + shape, lambda b, p, pt: (b,) + (0,) * len(shape))
    in_specs = [per_b((R, HEAD_DIM)), const(hm.shape), const(hmn.shape)]
    if joint:
        in_specs += [pl.BlockSpec((None, PAGE * vecs, HEAD_DIM), lambda b, p, pt, r=r: (pt[b, p * npp + r], 0, 0))
                     for r in range(npp)]
        in_specs += [per_b((n_new, HEAD_DIM))]
        args = (pools[0],) * npp + (new_rows[0],)
    else:
        for c in range(2):
            in_specs += [pl.BlockSpec((None, PAGE, None, nh, HEAD_DIM),
                                      lambda b, p, pt, r=r, c=c: (pt[b, p * npp + r], 0, c, 0, 0)) for r in range(npp)]
        in_specs += [per_b((n_new, HEAD_DIM)), per_b((n_new, HEAD_DIM))]
        args = (pools[0],) * (2 * npp) + (new_rows[0], new_rows[1])
    if mode == "fox":
        cq, ck = extras
        in_specs += [per_b((R, LANES)),
                     pl.BlockSpec((None, 1, npp * PAGE * nh), lambda b, p, pt: (b, 0, p)),
                     pl.BlockSpec((None, 1, n_new), lambda b, p, pt: (b, 0, P * PAGE * nh // n_new))]
        args += (cq, ck, ck)
    else:
        (sel,) = extras
        in_specs += [per_b((R, sel.shape[2]))]
        args += (sel,)
    gs = pltpu.PrefetchScalarGridSpec(
        num_scalar_prefetch=1, grid=(B, P // npp), in_specs=in_specs,
        out_specs=pl.BlockSpec((None, R, HEAD_DIM), lambda b, p, pt: (b, 0, 0)),
        scratch_shapes=[pltpu.VMEM((R, 1), F32), pltpu.VMEM((R, 1), F32), pltpu.VMEM((R, HEAD_DIM), F32)])
    return pl.pallas_call(
        functools.partial(_paged_attn_kernel, mode=mode, nh=nh, npp=npp, n_steps=P // npp), grid_spec=gs,
        out_shape=jax.ShapeDtypeStruct((B, R, HEAD_DIM), out_dtype), name="paged_attn_" + mode,
        compiler_params=_cparams("parallel", "arbitrary"))(page_table, q_rows, hm, hmn, *args)


PAGES_PER_STEP = 8


def _cumsum_paged_kernel(pt_ref, *refs, n_groups):
    del pt_ref
    x_refs = refs[:PAGES_PER_STEP]
    xn_ref, o_ref, carry_ref = refs[PAGES_PER_STEP:]
    g = pl.program_id(1)

    @pl.when(g == 0)
    def _():
        carry_ref[...] = jnp.zeros(carry_ref.shape, F32)

    carry = carry_ref[...]
    for r in range(PAGES_PER_STEP):
        tail = xn_ref[...] if r == 0 else jnp.zeros(xn_ref.shape, F32)
        cs = _tile_cumsum(jnp.where(g == n_groups, tail, x_refs[r][...])) + carry
        o_ref[:, r * PAGE:(r + 1) * PAGE] = cs
        carry = jnp.broadcast_to(cs[:, LANES - 1:LANES], carry.shape)
    carry_ref[...] = carry


def _cumsum_paged(pool_t, page_table, new_t):
    B, P = page_table.shape
    H = pool_t.shape[1]
    n = PAGES_PER_STEP
    assert P % n == 0
    ng = P // n
    page = [pl.BlockSpec((None, H, PAGE), lambda b, g, pt, r=r: (pt[b, jnp.minimum(g, ng - 1) * n + r], 0, 0))
            for r in range(n)]
    gs = pltpu.PrefetchScalarGridSpec(
        num_scalar_prefetch=1, grid=(B, ng + 1),
        in_specs=page + [pl.BlockSpec((None, H, PAGE), lambda b, g, pt: (b, 0, 0))],
        out_specs=pl.BlockSpec((None, H, n * PAGE), lambda b, g, pt: (b, 0, g)),
        scratch_shapes=[pltpu.VMEM((H, LANES), F32)])
    return pl.pallas_call(
        functools.partial(_cumsum_paged_kernel, n_groups=ng), grid_spec=gs,
        out_shape=jax.ShapeDtypeStruct((B, H, (ng + 1) * n * PAGE), F32), name="cumsum_paged",
        compiler_params=_cparams("parallel", "arbitrary"))(page_table, *([pool_t] * n), new_t)


def _kmeans_kernel(pt_ref, *refs, ppb):
    del pt_ref
    x_refs, o_ref = refs[:PAGES_PER_STEP], refs[PAGES_PER_STEP]
    for blk in range(PAGES_PER_STEP // ppb):
        tot = jnp.sum(x_refs[blk * ppb][...], axis=0)
        for r in range(1, ppb):
            tot = tot + jnp.sum(x_refs[blk * ppb + r][...], axis=0)
        o_ref[blk] = tot * (1.0 / MOBA_BLOCK)


def _moba_kmeans(pool5, page_table):
    B, P = page_table.shape
    ppb = MOBA_BLOCK // PAGE
    n = PAGES_PER_STEP
    assert P % n == 0 and n % ppb == 0
    page = [pl.BlockSpec((None, PAGE, None, N_HEADS, HEAD_DIM), lambda b, g, pt, r=r: (pt[b, g * n + r], 0, 0, 0, 0))
            for r in range(n)]
    gs = pltpu.PrefetchScalarGridSpec(
        num_scalar_prefetch=1, grid=(B, P // n), in_specs=page,
        out_specs=pl.BlockSpec((None, n // ppb, N_HEADS, HEAD_DIM), lambda b, g, pt: (b, g, 0, 0)))
    return pl.pallas_call(
        functools.partial(_kmeans_kernel, ppb=ppb), grid_spec=gs,
        out_shape=jax.ShapeDtypeStruct((B, P // ppb, N_HEADS, HEAD_DIM), F32), name="moba_kmeans",
        compiler_params=_cparams("parallel", "parallel"))(page_table, *([pool5] * n))


def _moba_select_kernel(q_ref, km_ref, o_ref, *, n_past_blocks):
    gate = _nt(q_ref[...], km_ref[...].astype(BF16))
    lane = _iota(gate.shape, 1)
    gate = jnp.where(lane < n_past_blocks, gate, NEG_INF)
    o_ref[...] = _top_k_mask(gate, MOBA_TOP, finite_only=True)


def _moba_select(qbd, kmeans, n_past_blocks):
    B, R, C = qbd.shape
    return pl.pallas_call(
        functools.partial(_moba_select_kernel, n_past_blocks=n_past_blocks), grid=(B,),
        in_specs=[pl.BlockSpec((None, R, C), lambda b: (b, 0, 0)), pl.BlockSpec((None, LANES, C), lambda b: (b, 0, 0))],
        out_specs=pl.BlockSpec((None, R, LANES), lambda b: (b, 0, 0)),
        out_shape=jax.ShapeDtypeStruct((B, R, LANES), F32), name="moba_select",
        compiler_params=_cparams("parallel"))(qbd, kmeans)


def _nsa_cmp_sample_kernel(q_ref, kc_ref, vc_ref, ovl_ref, o_ref, sel_ref, *, past):
    R = q_ref.shape[0]
    nm = kc_ref.shape[0]
    G = NSA_GROUP
    s = _nt(q_ref[...], kc_ref[...].astype(BF16)) * SCALE
    m_i = _iota((R, nm), 1)
    qpos = past + (_iota((R, nm), 0) % N_HEADS) // NSA_KV_HEADS
    ok = (m_i >= 1) & ((m_i - 1) * CMP_STRIDE + (CMP_LEN - 1) <= qpos)
    p_cmp = _masked_softmax(s, ok).astype(BF16)
    o_ref[...] = _diag_blocks(_dot(p_cmp, vc_ref[...].astype(BF16)), NSA_KV_HEADS)
    pov = _dot(p_cmp, ovl_ref[...])
    rg = R // G
    imp = pov[0:rg]
    for g in range(1, G):
        imp = imp + pov[g * rg:(g + 1) * rg]
    lane = _iota(imp.shape, 1)
    qblk = (past + _iota(imp.shape, 0) // NSA_KV_HEADS) // SLC_LEN
    sel = _top_k_mask(_slc_scores(imp, qblk, lane), SLC_TOP)
    sel_ref[...] = jnp.concatenate([sel] * G, axis=0)


def _nsa_cmp_sample(qbd, comp, past):
    B, R, C = qbd.shape
    nm = comp.shape[1]
    nb = -(-(past + R // N_HEADS) // SLC_LEN)
    width = -(-nb // LANES) * LANES
    ovl = _overlap_shifted(nm, nb, width)
    return pl.pallas_call(
        functools.partial(_nsa_cmp_sample_kernel, past=past), grid=(B,),
        in_specs=[pl.BlockSpec((None, R, C), lambda b: (b, 0, 0)),
                  pl.BlockSpec((None, nm, C), lambda b: (b, 0, 0)),
                  pl.BlockSpec((None, nm, C), lambda b: (b, 0, 1)),
                  pl.BlockSpec((nm, width), lambda b: (0, 0))],
        out_specs=[pl.BlockSpec((None, R, HEAD_DIM), lambda b: (b, 0, 0)),
                   pl.BlockSpec((None, R, width), lambda b: (b, 0, 0))],
        out_shape=[jax.ShapeDtypeStruct((B, R, HEAD_DIM), F32), jax.ShapeDtypeStruct((B, R, width), F32)],
        name="nsa_cmp_sample", compiler_params=_cparams("parallel"))(qbd, comp, comp, ovl)


def _nsa_win_sample_kernel(q_ref, kw_ref, vw_ref, oc_ref, os_ref, g_ref, o_ref, *, past, wb):
    R = q_ref.shape[0]
    nk = kw_ref.shape[0]
    s = _nt(q_ref[...], kw_ref[...].astype(BF16)) * SCALE
    j = _iota((R, nk), 1)
    qpos = past + (_iota((R, nk), 0) % N_HEADS) // NSA_KV_HEADS
    wpos = past - wb + j
    d = qpos - wpos
    ok = (d >= 0) & (d < WINDOW) & (wpos >= 0)
    p = _masked_softmax(s, ok).astype(BF16)
    o_win = _diag_blocks(_dot(p, vw_ref[...].astype(BF16)), NSA_KV_HEADS)
    g = g_ref[...]
    o = oc_ref[...] * g[:, 0:1] + os_ref[...] * g[:, 1:2] + o_win * g[:, 2:3]
    o_ref[...] = o.astype(o_ref.dtype)


def _nsa_win_sample(qbd, kw, o_cmp, o_slc, gates, past, wb):
    B, R, C = qbd.shape
    nk = kw.shape[1]
    row = pl.BlockSpec((None, R, HEAD_DIM), lambda b: (b, 0, 0))
    return pl.pallas_call(
        functools.partial(_nsa_win_sample_kernel, past=past, wb=wb), grid=(B,),
        in_specs=[pl.BlockSpec((None, R, C), lambda b: (b, 0, 0)),
                  pl.BlockSpec((None, nk, C), lambda b: (b, 0, 0)),
                  pl.BlockSpec((None, nk, C), lambda b: (b, 0, 1)),
                  row, row, row],
        out_specs=row, out_shape=jax.ShapeDtypeStruct((B, R, HEAD_DIM), BF16), name="nsa_win_sample",
        compiler_params=_cparams("parallel"))(qbd, kw, kw, o_cmp, o_slc, gates)


def _qbd_heads(q):
    B, Q, H, dh = q.shape
    x = q[:, :, :, None, :] * jnp.eye(H, dtype=q.dtype)[None, None, :, :, None]
    return x.reshape(B, Q * H, H * dh).astype(BF16)


def _qbd_groups(q):
    B, Q, H, dh = q.shape
    x = q.reshape(B, Q, NSA_KV_HEADS, NSA_GROUP, dh).transpose(0, 3, 1, 2, 4)
    x = x[:, :, :, :, None, :] * jnp.eye(NSA_KV_HEADS, dtype=q.dtype)[None, None, None, :, :, None]
    return x.reshape(B, NSA_GROUP * Q * NSA_KV_HEADS, NSA_KV_HEADS * dh).astype(BF16)


def _pad_rows(x, rows):
    return jnp.pad(x, ((0, 0), (0, rows - x.shape[1]), (0, 0)))


def _new_kv_rows(new):
    B, Q, _, H, dh = new.shape
    return (_pad_rows(new[:, :, 0].reshape(B, Q * H, dh), LANES), _pad_rows(new[:, :, 1].reshape(B, Q * H, dh), LANES))


def _sample_tables(B, Q, past):
    return _rope_tables(jnp.tile(past + jnp.arange(Q, dtype=I32), B))


def _nsa_sample_mixer(x2d, g, w_in, pe, w1, w2, pool_cmp, pool_slc, win_buf, page_table, layer, B, Q):
    P = page_table.shape[1]
    past = P * PAGE
    assert past % SLC_LEN == 0 and Q * N_HEADS % 8 == 0
    n_pool = pool_cmp.shape[1]
    pt = page_table + layer * n_pool
    z, gates = _nsa_project(x2d, g, w_in, _sample_tables(B, Q, past), B * Q)
    kv = [z[:, NSA_Q + 2 * NSA_KV * br:NSA_Q + 2 * NSA_KV * (br + 1)].reshape(B, Q, 2 * NSA_KV) for br in range(N_BRANCH)]
    assert (past + Q - CMP_LEN) // CMP_STRIDE + 1 == past // CMP_STRIDE - 1
    comp = _compress(pool_cmp.reshape(-1, PAGE * ROWS_PER_KV, HEAD_DIM), pt, w1, w2, pe)
    comp = comp.reshape(B, -1, ROWS_PER_KV * HEAD_DIM)
    q4 = z[:, :NSA_Q].reshape(B, Q, N_HEADS, HEAD_DIM)
    qbd = _qbd_groups(q4)
    o_cmp, sel = _nsa_cmp_sample(qbd, comp, past)
    q_rows = q4.reshape(B, Q, NSA_KV_HEADS, NSA_GROUP, HEAD_DIM).transpose(0, 3, 1, 2, 4)
    q_rows = q_rows.reshape(B, Q * N_HEADS, HEAD_DIM).astype(BF16)
    x_new = _pad_rows(kv[1].reshape(B, Q * ROWS_PER_KV, HEAD_DIM), LANES)
    o_slc = _paged_attn(q_rows, (pool_slc.reshape(-1, PAGE * ROWS_PER_KV, HEAD_DIM),), pt, (x_new,), (sel,),
                        mode="nsa", out_dtype=F32)
    wb = win_buf.shape[2]
    kw = jnp.concatenate([win_buf[layer].reshape(B, wb, 2 * NSA_KV), kv[2]], axis=1)
    g_rows = gates[:, :N_HEADS * N_BRANCH].reshape(B, Q, NSA_KV_HEADS, NSA_GROUP, N_BRANCH)
    g_rows = jnp.pad(g_rows.transpose(0, 3, 1, 2, 4).reshape(B, Q * N_HEADS, N_BRANCH), ((0, 0), (0, 0), (0, LANES - N_BRANCH)))
    o = _nsa_win_sample(qbd, _pad_rows(kw, -(-(wb + Q) // LANES) * LANES), o_cmp, o_slc, g_rows, past, wb)
    o = o.reshape(B, NSA_GROUP, Q, NSA_KV_HEADS, HEAD_DIM).transpose(0, 2, 3, 1, 4).reshape(B * Q, HD)
    shp = (B, -1, 2, NSA_KV_HEADS, HEAD_DIM)
    return o, (kv[0].reshape(shp), kv[1].reshape(shp), kw[:, Q:].reshape(shp))


def _moba_sample_mixer(x2d, g, w_in, pool, page_table, layer, B, Q):
    P = page_table.shape[1]
    past = P * PAGE
    assert past % MOBA_BLOCK == 0 and Q <= MOBA_BLOCK and past // MOBA_BLOCK <= LANES
    n_pool = pool.shape[1]
    pt = page_table + layer * n_pool
    z = _norm_mm(x2d, g, w_in, 3 * HD, tm=B * Q, epi="rope", rope=(_moba_rope_flags(),) + _sample_tables(B, Q, past))
    q4 = z[:, :HD].reshape(B, Q, N_HEADS, HEAD_DIM)
    nbp = past // MOBA_BLOCK
    pool5 = pool.reshape(-1, PAGE, 2, N_HEADS, HEAD_DIM)
    km = _moba_kmeans(pool5, pt).reshape(B, nbp, HD)
    sel = _moba_select(_qbd_heads(q4), _pad_rows(km, LANES), nbp)
    new = z[:, HD:].reshape(B, Q, 2, N_HEADS, HEAD_DIM)
    o = _paged_attn(q4.reshape(B, Q * N_HEADS, HEAD_DIM).astype(BF16), (pool5,), pt, _new_kv_rows(new), (sel,),
                    mode="moba", out_dtype=BF16)
    return o.reshape(B * Q, HD), (new,)


def _fox_sample_mixer(x2d, g, w_in, b_f, pool, pool_logf, page_table, layer, B, Q):
    P = page_table.shape[1]
    n_pool = pool.shape[1]
    pt = page_table + layer * n_pool
    z, logf = _fox_project(x2d, g, w_in, b_f, B * Q)
    log_f = logf[:, :N_HEADS].reshape(B, Q, N_HEADS)
    pool_t = pool_logf.astype(F32).transpose(0, 1, 3, 2).reshape(-1, N_HEADS, PAGE)
    new_t = jnp.pad(log_f.transpose(0, 2, 1), ((0, 0), (0, 0), (0, PAGE - Q)))
    ct = _cumsum_paged(pool_t, pt, new_t)
    cq = ct[:, :, P * PAGE:P * PAGE + Q].transpose(0, 2, 1).reshape(B, Q * N_HEADS, 1)
    ck = ct.transpose(0, 2, 1).reshape(B, 1, -1)
    q_rows = z[:, :HD].reshape(B, Q * N_HEADS, HEAD_DIM).astype(BF16)
    new = z[:, HD:].reshape(B, Q, 2, N_HEADS, HEAD_DIM)
    o = _paged_attn(q_rows, (pool.reshape(-1, PAGE, 2, N_HEADS, HEAD_DIM),), pt, _new_kv_rows(new),
                    (jnp.broadcast_to(cq, (B, Q * N_HEADS, LANES)), ck), mode="fox", out_dtype=BF16)
    return o.reshape(B * Q, HD), (new, log_f)


PROMPT_TM = 1024
FFN_DOWN_TN = 256


def kernel(x_prompt, x_sample, cache_nsa_cmp, cache_nsa_slc, cache_nsa_win, cache_moba_kv, cache_fox_kv,
           cache_fox_logf, page_table, norms, ffn_wi, ffn_wo, final_norm, nsa_w_in, nsa_cmp_pe, nsa_cmp_w1,
           nsa_cmp_w2, nsa_w_out, moba_w_in, moba_w_out, fox_w_in, fox_b_f, fox_w_out):
    B, T, D = x_prompt.shape
    Bs, Q, _ = x_sample.shape
    depth = norms.shape[0]
    xp = x_prompt.reshape(B * T, D)
    xs = x_sample.reshape(Bs * Q, D)
    tp = min(PROMPT_TM, B * T)
    ts = Bs * Q
    ffn_wi, ffn_wo = ffn_wi.astype(BF16), ffn_wo.astype(BF16)
    nsa_w_in, moba_w_in, fox_w_in = nsa_w_in.astype(BF16), moba_w_in.astype(BF16), fox_w_in.astype(BF16)
    nsa_w_out, moba_w_out, fox_w_out = nsa_w_out.astype(BF16), moba_w_out.astype(BF16), fox_w_out.astype(BF16)
    st_p = {0: [], 1: [], 2: []}
    st_s = {0: [], 1: [], 2: []}
    for i in range(depth):
        kind, j = i % 3, i // 3
        xp = _ffn(xp, norms[i, 0], _W(ffn_wi, (i, 0)), _W(ffn_wo, (i, 0)), tp)
        xs = _ffn(xs, norms[i, 0], _W(ffn_wi, (i, 0)), _W(ffn_wo, (i, 0)), ts)
        g = norms[i, 1]
        if kind == 0:
            w_in, w_out = _W(nsa_w_in, (j,)), _W(nsa_w_out, (j,))
            op, sp = _nsa_prompt_mixer(xp, g, w_in, nsa_cmp_pe[j], nsa_cmp_w1[j], nsa_cmp_w2[j], B, T, tp)
            os_, ss = _nsa_sample_mixer(xs, g, w_in, nsa_cmp_pe[j], nsa_cmp_w1[j], nsa_cmp_w2[j],
                                        cache_nsa_cmp, cache_nsa_slc, cache_nsa_win, page_table, j, Bs, Q)
        elif kind == 1:
            w_in, w_out = _W(moba_w_in, (j,)), _W(moba_w_out, (j,))
            op, sp = _moba_prompt_mixer(xp, g, w_in, B, T, tp)
            os_, ss = _moba_sample_mixer(xs, g, w_in, cache_moba_kv, page_table, j, Bs, Q)
        else:
            w_in, w_out = _W(fox_w_in, (j,)), _W(fox_w_out, (j,))
            op, sp = _fox_prompt_mixer(xp, g, w_in, fox_b_f[j], B, T, tp)
            os_, ss = _fox_sample_mixer(xs, g, w_in, fox_b_f[j], cache_fox_kv, cache_fox_logf, page_table, j, Bs, Q)
        st_p[kind].append(sp)
        st_s[kind].append(ss)
        xp = _mm_res(op, w_out, xp, 1.0, tm=tp, tn=512)
        xs = _mm_res(os_, w_out, xs, 1.0, tm=ts, tn=512)
        xp = _ffn(xp, norms[i, 2], _W(ffn_wi, (i, 1)), _W(ffn_wo, (i, 1)), tp)
        xs = _ffn(xs, norms[i, 2], _W(ffn_wi, (i, 1)), _W(ffn_wo, (i, 1)), ts)
    y_prompt = _rmsnorm(xp, final_norm, tm=tp).reshape(B, T, D)
    y_sample = _rmsnorm(xs, final_norm, tm=ts).reshape(Bs, Q, D)

    def stack(states, k):
        return jnp.stack([s[k] for s in states])

    return (y_prompt, y_sample,
            stack(st_p[0], 0), stack(st_p[0], 1), stack(st_p[0], 2),
            stack(st_s[0], 0), stack(st_s[0], 1), stack(st_s[0], 2),
            stack(st_p[1], 0), stack(st_s[1], 0),
            stack(st_p[2], 0), stack(st_p[2], 1), stack(st_s[2], 0), stack(st_s[2], 1))
```
